```python
import math
import jax, jax.numpy as jnp
from jax import lax
import numpy as np

D_MODEL = 2048
BATCH = 4
SEQ = 2048
DEPTH = 2
DEC_BATCH = 128
DEC_SEQ = 4
PAST_LEN = 16384
PAGE_SIZE = 128

N_EVEN = (DEPTH + 1) // 2
N_ODD = DEPTH // 2
PLE_DIM = 256
D_FF = 4 * D_MODEL
CONV_W = 4
CHUNK = 64
LN_EPS = 1e-5
RMS_EPS = 1e-6
DN_ALPHA = (2 * DEPTH) ** 0.25
DN_BETA = (8 * DEPTH) ** -0.25

A_HEADS = D_MODEL // 256
A_DK = 128
A_DV = 128
A_QK = A_HEADS * A_DK
A_V = A_HEADS * A_DV
LRU_W = D_MODEL // 2
B_BLOCKS = 8
B_BW = LRU_W // B_BLOCKS
LRU_C = 8.0
CONV_CH = 2 * A_QK + A_V + LRU_W
PROJ_E = CONV_CH + A_V + LRU_W + 2 * A_HEADS
MIX_E = A_V + LRU_W
C_HEADS = D_MODEL // 256
C_DK = 128
C_DV = 256
C_QK = C_HEADS * C_DK
C_V = C_HEADS * C_DV
PROJ_O = 2 * C_QK + 2 * C_V + 2 * C_HEADS

kernel_name = "deltanet_rglru_mlstm_hybrid_step"

F32 = jnp.float32


def layer_norm(x, g, b):
    xf = x.astype(F32)
    mu = jnp.mean(xf, -1, keepdims=True)
    var = jnp.mean(jnp.square(xf - mu), -1, keepdims=True)
    return ((xf - mu) * lax.rsqrt(var + LN_EPS) * g.astype(F32) + b.astype(F32)).astype(x.dtype)


def rms_norm(x, w):
    xf = x.astype(F32)
    return xf * lax.rsqrt(jnp.mean(xf * xf, -1, keepdims=True) + RMS_EPS) * w.astype(F32)


def l2_normalize(x):
    xf = x.astype(F32)
    return xf * lax.rsqrt(jnp.sum(xf * xf, -1, keepdims=True) + 1e-6)


def causal_conv(u, buf, w, b):
    T = u.shape[1]
    full = jnp.concatenate([buf.astype(u.dtype), u], axis=1)
    out = sum(full[:, j:j + T] * w[j] for j in range(CONV_W)) + b
    return out, full[:, T:]


def to_chunks(x, L):
    B, T, H = x.shape[:3]
    x = x.reshape((B, T // L, L, H) + x.shape[3:])
    return jnp.moveaxis(x, (1, 3), (0, 2))


def from_chunks(x):
    x = jnp.moveaxis(x, (0, 2), (1, 3))
    B, NC, L, H = x.shape[:4]
    return x.reshape((B, NC * L, H) + x.shape[4:])


def gated_delta_rule(q, k, v, g, beta, s0, L):
    qc, kc, vc = (to_chunks(t.astype(F32), L) for t in (q, k, v))
    gc = jnp.cumsum(to_chunks(g.astype(F32), L), axis=-1)
    bc = to_chunks(beta.astype(F32), L)
    idx = jnp.arange(L)
    tril = idx[:, None] >= idx[None, :]
    decay = jnp.exp(jnp.where(tril, gc[..., :, None] - gc[..., None, :], -jnp.inf))
    kb = kc * bc[..., None]
    m_strict = jnp.where(idx[:, None] > idx[None, :],
                         jnp.einsum('nbhid,nbhjd->nbhij', kb, kc) * decay, 0.0)
    eye = jnp.eye(L, dtype=F32)
    t_inv = lax.linalg.triangular_solve(eye + m_strict, jnp.broadcast_to(eye, m_strict.shape),
                                        left_side=True, lower=True)
    u = jnp.einsum('nbhij,nbhjd->nbhid', t_inv, vc * bc[..., None])
    w = jnp.einsum('nbhij,nbhjd->nbhid', t_inv, kb * jnp.exp(gc)[..., None])
    qk = jnp.einsum('nbhid,nbhjd->nbhij', qc, kc) * decay

    def step(s, xs):
        q_i, k_i, u_i, w_i, qk_i, g_i = xs
        v_new = u_i - jnp.einsum('bhik,bhkv->bhiv', w_i, s)
        o = (jnp.einsum('bhik,bhkv->bhiv', q_i * jnp.exp(g_i)[..., None], s)
             + jnp.einsum('bhij,bhjv->bhiv', qk_i, v_new))
        g_last = g_i[..., -1:]
        s = (s * jnp.exp(g_last)[..., None]
             + jnp.einsum('bhik,bhiv->bhkv', k_i * jnp.exp(g_last - g_i)[..., None], v_new))
        return s, o

    s_fin, o = lax.scan(step, s0.astype(F32), (qc, kc, u, w, qk, gc))
    return from_chunks(o), s_fin


def rg_lru(x, r_pre, i_pre, lam, h0):
    log_a = -LRU_C * jax.nn.sigmoid(r_pre.astype(F32)) * jax.nn.softplus(-lam.astype(F32))
    a = jnp.exp(log_a)
    bx = jnp.sqrt(-jnp.expm1(2.0 * log_a)) * jax.nn.sigmoid(i_pre.astype(F32)) * x.astype(F32)
    bx = bx.at[:, 0].add(a[:, 0] * h0.astype(F32))

    def combine(left, right):
        a1, b1 = left
        a2, b2 = right
        return a1 * a2, a2 * b1 + b2

    _, h = lax.associative_scan(combine, (a, bx), axis=1)
    return h, h[:, -1]


def mlstm_chunked(q, k, v, ig, fg, c0, n0, m0, L):
    qc, kc, vc = (to_chunks(t.astype(F32), L) for t in (q, k, v))
    igc = to_chunks(ig.astype(F32), L)
    bcum = jnp.cumsum(jax.nn.log_sigmoid(to_chunks(fg.astype(F32), L)), axis=-1)
    idx = jnp.arange(L)
    tril = idx[:, None] >= idx[None, :]
    d_intra = jnp.where(tril, bcum[..., :, None] - bcum[..., None, :] + igc[..., None, :], -jnp.inf)
    g_end = bcum[..., -1:] - bcum + igc
    qk = jnp.einsum('nbhtk,nbhsk->nbhts', qc, kc)

    def step(carry, xs):
        c, n, m = carry
        q_i, k_i, v_i, b_i, d_i, ge_i, qk_i = xs
        inter = b_i + m[..., None]
        m_t = jnp.maximum(inter, jnp.max(d_i, -1))
        e = jnp.exp(inter - m_t)
        s = qk_i * jnp.exp(d_i - m_t[..., None])
        num = (e[..., None] * jnp.einsum('bhtk,bhvk->bhtv', q_i, c)
               + jnp.einsum('bhts,bhsv->bhtv', s, v_i))
        den = e * jnp.einsum('bhtk,bhk->bht', q_i, n) + jnp.sum(s, -1)
        h = num / jnp.maximum(jnp.abs(den), jnp.exp(-m_t))[..., None]
        b_last = b_i[..., -1]
        m_new = jnp.maximum(b_last + m, jnp.max(ge_i, -1))
        sc = jnp.exp(b_last + m - m_new)
        w_s = jnp.exp(ge_i - m_new[..., None])
        c = sc[..., None, None] * c + jnp.einsum('bhs,bhsv,bhsk->bhvk', w_s, v_i, k_i)
        n = sc[..., None] * n + jnp.einsum('bhs,bhsk->bhk', w_s, k_i)
        return (c, n, m_new), h

    (c, n, m), h = lax.scan(step, (c0.astype(F32), n0.astype(F32), m0.astype(F32)),
                            (qc, kc, vc, bcum, d_intra, g_end, qk))
    return from_chunks(h), c, n, m


def delta_lru_mixer(x, conv_buf, s_delta, h_lru, w_in, w_conv, b_conv, a_log, dt_bias, norm_w,
                    w_r, b_r, w_i, b_i, lam, w_out, L):
    B, T, _ = x.shape
    proj = x @ w_in
    s1 = CONV_CH
    s2 = s1 + A_V
    s3 = s2 + LRU_W
    s4 = s3 + A_HEADS
    conv_in, z, gate, a_pre, b_pre = jnp.split(proj, [s1, s2, s3, s4], axis=-1)
    conv_out, conv_new = causal_conv(conv_in, conv_buf, w_conv, b_conv)
    q, k, v, xr = jnp.split(conv_out, [A_QK, 2 * A_QK, 2 * A_QK + A_V], axis=-1)
    q = l2_normalize(jax.nn.silu(q).reshape(B, T, A_HEADS, A_DK)) * (A_DK ** -0.5)
    k = l2_normalize(jax.nn.silu(k).reshape(B, T, A_HEADS, A_DK))
    v = jax.nn.silu(v).reshape(B, T, A_HEADS, A_DV)
    g = -jnp.exp(a_log.astype(F32)) * jax.nn.softplus(a_pre.astype(F32) + dt_bias.astype(F32))
    beta = jax.nn.sigmoid(b_pre.astype(F32))
    o, s_new = gated_delta_rule(q, k, v, g, beta, s_delta, L)
    y_a = (rms_norm(o, norm_w) * jax.nn.silu(z.reshape(B, T, A_HEADS, A_DV).astype(F32))).reshape(B, T, A_V)
    xb = xr.reshape(B, T, B_BLOCKS, B_BW)
    r_pre = jnp.einsum('btnc,ncd->btnd', xb, w_r).reshape(B, T, LRU_W) + b_r
    i_pre = jnp.einsum('btnc,ncd->btnd', xb, w_i).reshape(B, T, LRU_W) + b_i
    h, h_last = rg_lru(xr, r_pre, i_pre, lam, h_lru)
    y_b = h * jax.nn.gelu(gate.astype(F32))
    y = jnp.concatenate([y_a, y_b], axis=-1).astype(x.dtype) @ w_out
    return y, conv_new, s_new, h_last


def mlstm_mixer(x, c0, n0, m0, w_in, b_ig, b_fg, norm_w, w_out, L):
    B, T, _ = x.shape
    proj = x @ w_in
    cuts = [C_QK, 2 * C_QK, 2 * C_QK + C_V, 2 * C_QK + 2 * C_V, 2 * C_QK + 2 * C_V + C_HEADS]
    q, k, v, o_pre, ig, fg = jnp.split(proj, cuts, axis=-1)
    q = q.reshape(B, T, C_HEADS, C_DK)
    k = k.reshape(B, T, C_HEADS, C_DK) * (C_DK ** -0.5)
    v = v.reshape(B, T, C_HEADS, C_DV)
    ig = ig.astype(F32) + b_ig.astype(F32)
    fg = fg.astype(F32) + b_fg.astype(F32)
    h, c, n, m = mlstm_chunked(q, k, v, ig, fg, c0, n0, m0, L)
    y = rms_norm(h, norm_w.reshape(C_HEADS, C_DV)) * jax.nn.sigmoid(o_pre.reshape(B, T, C_HEADS, C_DV).astype(F32))
    y = y.reshape(B, T, C_V).astype(x.dtype) @ w_out
    return y, c, n, m


def run_trunk(x, p, conv, delta, lru, mc, mn, mm, W):
    T = x.shape[1]
    L = math.gcd(T, CHUNK)
    conv_o, delta_o, lru_o, mc_o, mn_o, mm_o = [], [], [], [], [], []
    for layer in range(DEPTH):
        j = layer // 2
        if layer % 2 == 0:
            mix, cb, sd, hl = delta_lru_mixer(
                x, conv[j], delta[j], lru[j], W['w_in_e'][j], W['w_conv_e'][j], W['b_conv_e'][j],
                W['a_log_e'][j], W['dt_bias_e'][j], W['delta_norm_e'][j], W['lru_wr_e'][j],
                W['lru_br_e'][j], W['lru_wi_e'][j], W['lru_bi_e'][j], W['lru_lambda_e'][j],
                W['w_out_e'][j], L)
            conv_o.append(cb)
            delta_o.append(sd)
            lru_o.append(hl)
        else:
            mix, c, n, m = mlstm_mixer(
                x, mc[j], mn[j], mm[j], W['w_in_o'][j], W['b_ig_o'][j], W['b_fg_o'][j],
                W['mlstm_norm_o'][j], W['w_out_o'][j], L)
            mc_o.append(c)
            mn_o.append(n)
            mm_o.append(m)
        h = layer_norm(DN_ALPHA * x + mix.astype(x.dtype), W['ln1_g'][layer], W['ln1_b'][layer])
        ff = jnp.square(jax.nn.relu(h @ W['w_up'][layer])) @ W['w_down'][layer]
        h = layer_norm(DN_ALPHA * h + ff, W['ln2_g'][layer], W['ln2_b'][layer])
        gate = jax.nn.sigmoid((h @ W['w_ple_gate'][layer]).astype(F32))
        x = (h.astype(F32) + gate * (p[layer] @ W['w_ple'][layer]).astype(F32)).astype(x.dtype)
    return (x, jnp.stack(conv_o), jnp.stack(delta_o), jnp.stack(lru_o),
            jnp.stack(mc_o), jnp.stack(mn_o), jnp.stack(mm_o))


def setup_inputs(seed: int = 0) -> dict:
    key = jax.random.key(seed)
    ks = iter(jax.random.split(key, 48))

    def nrm(shape, scale):
        return jax.random.normal(next(ks), shape, F32) * scale

    def unif(shape, lo, hi):
        return jax.random.uniform(next(ks), shape, F32, lo, hi)

    d = {}
    d['x_prompt'] = nrm((BATCH, SEQ, D_MODEL), 1.0)
    d['x_sample'] = nrm((DEC_BATCH, DEC_SEQ, D_MODEL), 1.0)
    d['p_prompt'] = nrm((DEPTH, BATCH, SEQ, PLE_DIM), 1.0)
    d['p_sample'] = nrm((DEPTH, DEC_BATCH, DEC_SEQ, PLE_DIM), 1.0)
    d['state_conv'] = nrm((N_EVEN, DEC_BATCH, CONV_W - 1, CONV_CH), 1.0)
    d['state_delta'] = nrm((N_EVEN, DEC_BATCH, A_HEADS, A_DK, A_DV), 0.3)
    d['state_lru'] = nrm((N_EVEN, DEC_BATCH, LRU_W), 0.5)
    d['state_mlstm_c'] = nrm((N_ODD, DEC_BATCH, C_HEADS, C_DV, C_DK), 0.3)
    d['state_mlstm_n'] = nrm((N_ODD, DEC_BATCH, C_HEADS, C_DK), 0.3)
    d['state_mlstm_m'] = unif((N_ODD, DEC_BATCH, C_HEADS), 0.0, 2.0)
    d['w_in_e'] = nrm((N_EVEN, D_MODEL, PROJ_E), D_MODEL ** -0.5)
    d['w_conv_e'] = nrm((N_EVEN, CONV_W, CONV_CH), CONV_W ** -0.5)
    d['b_conv_e'] = nrm((N_EVEN, CONV_CH), 0.01)
    d['a_log_e'] = jnp.log(unif((N_EVEN, A_HEADS), 1.0, 16.0))
    dt = jnp.exp(unif((N_EVEN, A_HEADS), math.log(1e-3), math.log(1e-1)))
    d['dt_bias_e'] = dt + jnp.log(-jnp.expm1(-dt))
    d['delta_norm_e'] = 1.0 + nrm((N_EVEN, A_DV), 0.02)
    d['lru_wr_e'] = nrm((N_EVEN, B_BLOCKS, B_BW, B_BW), B_BW ** -0.5)
    d['lru_br_e'] = nrm((N_EVEN, LRU_W), 0.01)
    d['lru_wi_e'] = nrm((N_EVEN, B_BLOCKS, B_BW, B_BW), B_BW ** -0.5)
    d['lru_bi_e'] = nrm((N_EVEN, LRU_W), 0.01)
    u = unif((N_EVEN, LRU_W), 0.9, 0.999) ** (1.0 / LRU_C)
    d['lru_lambda_e'] = jnp.log(u) - jnp.log1p(-u)
    d['w_out_e'] = nrm((N_EVEN, MIX_E, D_MODEL), MIX_E ** -0.5 * DN_BETA)
    d['w_in_o'] = nrm((N_ODD, D_MODEL, PROJ_O), D_MODEL ** -0.5)
    d['b_ig_o'] = nrm((N_ODD, C_HEADS), 0.1)
    d['b_fg_o'] = jnp.linspace(3.0, 6.0, C_HEADS, dtype=F32)[None, :] + nrm((N_ODD, C_HEADS), 0.1)
    d['mlstm_norm_o'] = 1.0 + nrm((N_ODD, C_V), 0.02)
    d['w_out_o'] = nrm((N_ODD, C_V, D_MODEL), C_V ** -0.5 * DN_BETA)
    d['ln1_g'] = 1.0 + nrm((DEPTH, D_MODEL), 0.02)
    d['ln1_b'] = nrm((DEPTH, D_MODEL), 0.02)
    d['ln2_g'] = 1.0 + nrm((DEPTH, D_MODEL), 0.02)
    d['ln2_b'] = nrm((DEPTH, D_MODEL), 0.02)
    d['w_up'] = nrm((DEPTH, D_MODEL, D_FF), D_MODEL ** -0.5)
    d['w_down'] = nrm((DEPTH, D_FF, D_MODEL), D_FF ** -0.5 * DN_BETA)
    d['w_ple'] = nrm((DEPTH, PLE_DIM, D_MODEL), PLE_DIM ** -0.5 * 0.5)
    d['w_ple_gate'] = nrm((DEPTH, D_MODEL, D_MODEL), D_MODEL ** -0.5)
    return d


def reference(x_prompt, x_sample, p_prompt, p_sample, state_conv, state_delta, state_lru,
              state_mlstm_c, state_mlstm_n, state_mlstm_m, w_in_e, w_conv_e, b_conv_e, a_log_e,
              dt_bias_e, delta_norm_e, lru_wr_e, lru_br_e, lru_wi_e, lru_bi_e, lru_lambda_e, w_out_e,
              w_in_o, b_ig_o, b_fg_o, mlstm_norm_o, w_out_o, ln1_g, ln1_b, ln2_g, ln2_b, w_up, w_down,
              w_ple, w_ple_gate):
    W = dict(w_in_e=w_in_e, w_conv_e=w_conv_e, b_conv_e=b_conv_e, a_log_e=a_log_e,
             dt_bias_e=dt_bias_e, delta_norm_e=delta_norm_e, lru_wr_e=lru_wr_e, lru_br_e=lru_br_e,
             lru_wi_e=lru_wi_e, lru_bi_e=lru_bi_e, lru_lambda_e=lru_lambda_e, w_out_e=w_out_e,
             w_in_o=w_in_o, b_ig_o=b_ig_o, b_fg_o=b_fg_o, mlstm_norm_o=mlstm_norm_o, w_out_o=w_out_o,
             ln1_g=ln1_g, ln1_b=ln1_b, ln2_g=ln2_g, ln2_b=ln2_b, w_up=w_up, w_down=w_down,
             w_ple=w_ple, w_ple_gate=w_ple_gate)
    bp = x_prompt.shape[0]
    z_conv = jnp.zeros((N_EVEN, bp, CONV_W - 1, CONV_CH), x_prompt.dtype)
    z_delta = jnp.zeros((N_EVEN, bp, A_HEADS, A_DK, A_DV), F32)
    z_lru = jnp.zeros((N_EVEN, bp, LRU_W), F32)
    z_c = jnp.zeros((N_ODD, bp, C_HEADS, C_DV, C_DK), F32)
    z_n = jnp.zeros((N_ODD, bp, C_HEADS, C_DK), F32)
    z_m = jnp.zeros((N_ODD, bp, C_HEADS), F32)
    y_prompt, conv_p, delta_p, lru_p, mc_p, mn_p, mm_p = run_trunk(
        x_prompt, p_prompt, z_conv, z_delta, z_lru, z_c, z_n, z_m, W)
    y_sample, conv_s, delta_s, lru_s, mc_s, mn_s, mm_s = run_trunk(
        x_sample, p_sample, state_conv, state_delta, state_lru,
        state_mlstm_c, state_mlstm_n, state_mlstm_m, W)
    return (y_prompt, y_sample, conv_p, delta_p, lru_p, mc_p, mn_p, mm_p,
            conv_s, delta_s, lru_s, mc_s, mn_s, mm_s)
```

```python
import functools
import math

import jax
import jax.numpy as jnp
from jax import lax
from jax.experimental import pallas as pl
from jax.experimental.pallas import tpu as pltpu

F32 = jnp.float32
BF16 = jnp.bfloat16
HI = lax.Precision.HIGHEST

LANES = 128
SUBLANES = 8
VMEM_LIMIT = 56 * 1024 * 1024

HEADS = 8
DK = 128
A_DV = 128
C_DV = 256
LRU_W = 1024
LRU_BLOCKS = 8
LRU_C = 8.0
CONV_W = 4
CONV_CH = 4096
CHUNK = 64
LN_EPS = 1e-5
RMS_EPS = 1e-6
L2_EPS = 1e-6
PROJ_MAIN = 6144
PROJ_PAD = 6400
PROJ_TN = 1280
CONV_TAIL = SUBLANES
NEG_BIG = -1e30


def _dot(a, b):
    return lax.dot_general(a, b, (((1,), (0,)), ((), ())), precision=HI,
                           preferred_element_type=F32)


def _dot_nt(a, b):
    return lax.dot_general(a, b, (((1,), (1,)), ((), ())), precision=HI,
                           preferred_element_type=F32)


def _dot_tn(a, b):
    return lax.dot_general(a, b, (((0,), (0,)), ((), ())), precision=HI,
                           preferred_element_type=F32)


def _sigmoid(x):
    return 1.0 / (1.0 + jnp.exp(-x))


def _silu(x):
    return x * _sigmoid(x)


def _softplus(x):
    return jnp.maximum(x, 0.0) + jnp.log1p(jnp.exp(-jnp.abs(x)))


def _gelu_tanh(x):
    return 0.5 * x * (1.0 + jnp.tanh(math.sqrt(2.0 / math.pi) * (x + 0.044715 * (x * x * x))))


def _layer_norm(x, g, b):
    mu = jnp.mean(x, axis=-1, keepdims=True)
    xc = x - mu
    var = jnp.mean(xc * xc, axis=-1, keepdims=True)
    return xc * lax.rsqrt(var + LN_EPS) * g + b


def _col_to_row(col, eye):
    return jnp.sum(jnp.where(eye, col, 0.0), axis=0, keepdims=True)


def _masks(L):
    ri = lax.broadcasted_iota(jnp.int32, (L, L), 0)
    ci = lax.broadcasted_iota(jnp.int32, (L, L), 1)
    return ri >= ci, ri > ci, ri == ci


def _params(sem):
    return pltpu.CompilerParams(dimension_semantics=sem, vmem_limit_bytes=VMEM_LIMIT)


def _proj_body(x_ref, w_ref, o_ref):
    o_ref[...] = jnp.dot(x_ref[...].astype(BF16), w_ref[...], preferred_element_type=F32)


def _proj(x, w, tm, tn):
    M, K = x.shape
    N = w.shape[1]
    return pl.pallas_call(
        _proj_body,
        grid=(M // tm, N // tn),
        in_specs=[pl.BlockSpec((tm, K), lambda i, j: (i, 0)),
                  pl.BlockSpec((K, tn), lambda i, j: (0, j))],
        out_specs=pl.BlockSpec((tm, tn), lambda i, j: (i, j)),
        out_shape=jax.ShapeDtypeStruct((M, N), F32),
        compiler_params=_params(("parallel", "arbitrary")),
        name="in_proj",
    )(x, w)


def _outproj_ln_body(y_ref, x_ref, w_ref, g_ref, b_ref, o_ref, *, alpha):
    mix = jnp.dot(y_ref[...].astype(BF16), w_ref[...], preferred_element_type=F32)
    o_ref[...] = _layer_norm(alpha * x_ref[...] + mix, g_ref[...], b_ref[...])


def _outproj_ln(y, x, w, g, b, tm, alpha):
    M, K = y.shape
    D = w.shape[1]
    row = lambda i: (i, 0)
    fixed = lambda i: (0, 0)
    return pl.pallas_call(
        functools.partial(_outproj_ln_body, alpha=alpha),
        grid=(M // tm,),
        in_specs=[pl.BlockSpec((tm, K), row), pl.BlockSpec((tm, D), row),
                  pl.BlockSpec((K, D), fixed), pl.BlockSpec((1, D), fixed),
                  pl.BlockSpec((1, D), fixed)],
        out_specs=pl.BlockSpec((tm, D), row),
        out_shape=jax.ShapeDtypeStruct((M, D), F32),
        compiler_params=_params(("parallel",)),
        name="out_proj_ln",
    )(y, x, w, g, b)


def _mlp_ln_body(h_ref, wu_ref, wd_ref, g_ref, b_ref, o_ref, hb_ref, acc_ref, *, alpha):
    f = pl.program_id(1)

    @pl.when(f == 0)
    def _():
        hb_ref[...] = h_ref[...].astype(BF16)
        acc_ref[...] = jnp.zeros_like(acc_ref)

    up = jnp.dot(hb_ref[...], wu_ref[...], preferred_element_type=F32)
    act = jnp.square(jnp.maximum(up, 0.0)).astype(BF16)
    acc_ref[...] += jnp.dot(act, wd_ref[...], preferred_element_type=F32)

    @pl.when(f == pl.num_programs(1) - 1)
    def _():
        o_ref[...] = _layer_norm(alpha * h_ref[...] + acc_ref[...], g_ref[...], b_ref[...])


def _mlp_ln(h, wu, wd, g, b, tm, tf, alpha):
    M, D = h.shape
    FF = wu.shape[1]
    return pl.pallas_call(
        functools.partial(_mlp_ln_body, alpha=alpha),
        grid=(M // tm, FF // tf),
        in_specs=[pl.BlockSpec((tm, D), lambda i, f: (i, 0)),
                  pl.BlockSpec((D, tf), lambda i, f: (0, f)),
                  pl.BlockSpec((tf, D), lambda i, f: (f, 0)),
                  pl.BlockSpec((1, D), lambda i, f: (0, 0)),
                  pl.BlockSpec((1, D), lambda i, f: (0, 0))],
        out_specs=pl.BlockSpec((tm, D), lambda i, f: (i, 0)),
        out_shape=jax.ShapeDtypeStruct((M, D), F32),
        scratch_shapes=[pltpu.VMEM((tm, D), BF16), pltpu.VMEM((tm, D), F32)],
        compiler_params=_params(("parallel", "arbitrary")),
        name="mlp_ln",
    )(h, wu, wd, g, b)


def _ple_body(h_ref, p_ref, wg_ref, wp_ref, o_ref):
    h = h_ref[...]
    gate = _sigmoid(jnp.dot(h.astype(BF16), wg_ref[...], preferred_element_type=F32))
    emb = jnp.dot(p_ref[...].astype(BF16), wp_ref[...], preferred_element_type=F32)
    o_ref[...] = h + gate * emb


def _ple(h, p, wg, wp, tm):
    M, D = h.shape
    P = p.shape[1]
    row = lambda i: (i, 0)
    fixed = lambda i: (0, 0)
    return pl.pallas_call(
        _ple_body,
        grid=(M // tm,),
        in_specs=[pl.BlockSpec((tm, D), row), pl.BlockSpec((tm, P), row),
                  pl.BlockSpec((D, D), fixed), pl.BlockSpec((P, D), fixed)],
        out_specs=pl.BlockSpec((tm, D), row),
        out_shape=jax.ShapeDtypeStruct((M, D), F32),
        compiler_params=_params(("parallel",)),
        name="ple_gate",
    )(h, p, wg, wp)


def _unit_lower_inverse(n_mat, eye_f, L):
    t = eye_f + n_mat
    p = n_mat
    for _ in range(int(math.log2(L)) - 1):
        p = _dot(p, p)
        t = t + _dot(t, p)
    return t


def _mixer_e_body(proj_ref, conv0_ref, s0_ref, h0_ref, wconv_ref, bconv_ref, alog_ref,
                  dtb_ref, nw_ref, wr_ref, br_ref, wi_ref, bi_ref, lam_ref,
                  y_ref, convn_ref, s_ref, hl_ref,
                  cbuf, a_scr, bx_scr, hseq, *, L, t_valid, conv_win):
    c = pl.program_id(1)
    last = pl.num_programs(1) - 1

    @pl.when(c == 0)
    def _():
        cbuf[0:CONV_TAIL, :] = conv0_ref[0]
        s_ref[0] = s0_ref[0]
        hl_ref[0] = h0_ref[0]

    tril, strict, eye = _masks(L)
    eye_f = jnp.where(eye, 1.0, 0.0).astype(F32)
    tril_f = jnp.where(tril, 1.0, 0.0).astype(F32)
    row_ok = lax.broadcasted_iota(jnp.int32, (L, 1), 0) < t_valid

    cbuf[CONV_TAIL:CONV_TAIL + L, :] = proj_ref[:, 0:CONV_CH]
    acc = bconv_ref[...] + cbuf[CONV_TAIL - 3:CONV_TAIL - 3 + L, :] * wconv_ref[0:1, :]
    for j in range(1, CONV_W):
        acc = acc + cbuf[CONV_TAIL - 3 + j:CONV_TAIL - 3 + j + L, :] * wconv_ref[j:j + 1, :]

    @pl.when(c == last)
    def _():
        convn_ref[0] = cbuf[conv_win:conv_win + SUBLANES, :]

    cbuf[0:CONV_TAIL, :] = cbuf[L:L + CONV_TAIL, :]

    gp = proj_ref[:, PROJ_MAIN:PROJ_MAIN + LANES]
    g_all = -jnp.exp(alog_ref[...]) * _softplus(gp + dtb_ref[...])
    beta_all = _sigmoid(gp)
    if t_valid < L:
        g_all = jnp.where(row_ok, g_all, 0.0)
        beta_all = jnp.where(row_ok, beta_all, 0.0)
    gc_all = _dot(tril_f, g_all)

    for h in range(HEADS):
        sl = slice(h * DK, (h + 1) * DK)
        q = _silu(acc[:, sl])
        k = _silu(acc[:, HEADS * DK + h * DK:HEADS * DK + (h + 1) * DK])
        v = _silu(acc[:, 2 * HEADS * DK + h * A_DV:2 * HEADS * DK + (h + 1) * A_DV])
        q = q * lax.rsqrt(jnp.sum(q * q, axis=-1, keepdims=True) + L2_EPS) * (DK ** -0.5)
        k = k * lax.rsqrt(jnp.sum(k * k, axis=-1, keepdims=True) + L2_EPS)
        gcol = gc_all[:, h:h + 1]
        bcol = beta_all[:, HEADS + h:HEADS + h + 1]
        grow = _col_to_row(gcol, eye)
        decay = jnp.where(tril, jnp.exp(jnp.where(tril, gcol - grow, 0.0)), 0.0)
        kb = k * bcol
        n_mat = jnp.where(strict, -(_dot_nt(kb, k) * decay), 0.0)
        t_inv = _unit_lower_inverse(n_mat, eye_f, L)
        egc = jnp.exp(gcol)
        u = _dot(t_inv, v * bcol)
        w = _dot(t_inv, kb * egc)
        qk = _dot_nt(q, k) * decay
        s = s_ref[0, h]
        v_new = u - _dot(w, s)
        o = _dot(q * egc, s) + _dot(qk, v_new)
        g_last = gcol[L - 1:L, :]
        s_ref[0, h] = s * jnp.exp(g_last) + _dot_tn(k * jnp.exp(g_last - gcol), v_new)
        o = o * lax.rsqrt(jnp.mean(o * o, axis=-1, keepdims=True) + RMS_EPS) * nw_ref[...]
        z = proj_ref[:, CONV_CH + h * A_DV:CONV_CH + (h + 1) * A_DV]
        y_ref[:, h * A_DV:(h + 1) * A_DV] = o * _silu(z)

    bw = LRU_W // LRU_BLOCKS
    xr = acc[:, 3 * HEADS * DK:3 * HEADS * DK + LRU_W]
    sp = _softplus(-lam_ref[...])
    for n in range(LRU_BLOCKS):
        sl = slice(n * bw, (n + 1) * bw)
        xb = xr[:, sl]
        r_pre = _dot(xb, wr_ref[n]) + br_ref[:, sl]
        i_pre = _dot(xb, wi_ref[n]) + bi_ref[:, sl]
        log_a = -LRU_C * _sigmoid(r_pre) * sp[:, sl]
        a = jnp.exp(log_a)
        bx = jnp.sqrt(-jnp.tanh(log_a) * (a * a + 1.0)) * _sigmoid(i_pre) * xb
        if t_valid < L:
            a = jnp.where(row_ok, a, 1.0)
            bx = jnp.where(row_ok, bx, 0.0)
        a_scr[:, sl] = a
        bx_scr[:, sl] = bx

    def step(t, hc):
        hc = a_scr[pl.ds(t, 1), :] * hc + bx_scr[pl.ds(t, 1), :]
        hseq[pl.ds(t, 1), :] = hc
        return hc

    hl_ref[0] = lax.fori_loop(0, L, step, hl_ref[0], unroll=8)
    gate = proj_ref[:, CONV_CH + HEADS * A_DV:CONV_CH + HEADS * A_DV + LRU_W]
    y_ref[:, HEADS * A_DV:HEADS * A_DV + LRU_W] = hseq[...] * _gelu_tanh(gate)


def _mixer_e(proj, conv0, s0, h0, wconv, bconv, alog, dtb, nw, wr, br, wi, bi, lam,
             *, B, nc, L, t_valid):
    conv_row = CONV_TAIL - 3 + (nc - 1) * 0 + t_valid
    conv_win = (conv_row // SUBLANES) * SUBLANES
    assert conv_row - conv_win + 3 <= SUBLANES
    mix = HEADS * A_DV + LRU_W
    chunk = lambda b, c: (b * nc + c, 0)
    fix2 = lambda b, c: (0, 0)
    fix3 = lambda b, c: (0, 0, 0)
    outs = pl.pallas_call(
        functools.partial(_mixer_e_body, L=L, t_valid=t_valid, conv_win=conv_win),
        grid=(B, nc),
        in_specs=[pl.BlockSpec((L, PROJ_PAD), chunk),
                  pl.BlockSpec((1, SUBLANES, CONV_CH), lambda b, c: (b, 0, 0)),
                  pl.BlockSpec((1, HEADS, DK, A_DV), lambda b, c: (b, 0, 0, 0)),
                  pl.BlockSpec((1, 1, LRU_W), lambda b, c: (b, 0, 0)),
                  pl.BlockSpec((CONV_W, CONV_CH), fix2),
                  pl.BlockSpec((1, CONV_CH), fix2),
                  pl.BlockSpec((1, LANES), fix2),
                  pl.BlockSpec((1, LANES), fix2),
                  pl.BlockSpec((1, A_DV), fix2),
                  pl.BlockSpec((LRU_BLOCKS, LRU_W // LRU_BLOCKS, LRU_W // LRU_BLOCKS), fix3),
                  pl.BlockSpec((1, LRU_W), fix2),
                  pl.BlockSpec((LRU_BLOCKS, LRU_W // LRU_BLOCKS, LRU_W // LRU_BLOCKS), fix3),
                  pl.BlockSpec((1, LRU_W), fix2),
                  pl.BlockSpec((1, LRU_W), fix2)],
        out_specs=[pl.BlockSpec((L, mix), chunk),
                   pl.BlockSpec((1, SUBLANES, CONV_CH), lambda b, c: (b, 0, 0)),
                   pl.BlockSpec((1, HEADS, DK, A_DV), lambda b, c: (b, 0, 0, 0)),
                   pl.BlockSpec((1, 1, LRU_W), lambda b, c: (b, 0, 0))],
        out_shape=[jax.ShapeDtypeStruct((B * nc * L, mix), F32),
                   jax.ShapeDtypeStruct((B, SUBLANES, CONV_CH), F32),
                   jax.ShapeDtypeStruct((B, HEADS, DK, A_DV), F32),
                   jax.ShapeDtypeStruct((B, 1, LRU_W), F32)],
        scratch_shapes=[pltpu.VMEM((CONV_TAIL + L, CONV_CH), F32),
                        pltpu.VMEM((L, LRU_W), F32),
                        pltpu.VMEM((L, LRU_W), F32),
                        pltpu.VMEM((L, LRU_W), F32)],
        compiler_params=_params(("parallel", "arbitrary")),
        name="delta_lru_mixer",
    )(proj, conv0, s0, h0, wconv, bconv, alog, dtb, nw, wr, br, wi, bi, lam)
    return outs, conv_row - conv_win


def _mixer_o_body(proj_ref, c0_ref, n0_ref, m0_ref, big_ref, bfg_ref, nw_ref,
                  y_ref, c_ref, n_ref, m_ref, *, L, t_valid):
    ci = pl.program_id(1)

    @pl.when(ci == 0)
    def _():
        c_ref[0] = c0_ref[0]
        n_ref[0] = n0_ref[0]
        m_ref[0] = m0_ref[0]

    tril, _, eye = _masks(L)
    tril_f = jnp.where(tril, 1.0, 0.0).astype(F32)
    row_ok = lax.broadcasted_iota(jnp.int32, (L, 1), 0) < t_valid

    gp = proj_ref[:, PROJ_MAIN:PROJ_MAIN + LANES]
    ig_all = gp + big_ref[...]
    logf_all = -_softplus(-(gp + bfg_ref[...]))
    if t_valid < L:
        ig_all = jnp.where(row_ok, ig_all, NEG_BIG)
        logf_all = jnp.where(row_ok, logf_all, 0.0)
    bcum_all = _dot(tril_f, logf_all)
    m_all = m_ref[0]

    m_new_all = m_all
    lane = lax.broadcasted_iota(jnp.int32, (1, LANES), 1)
    for h in range(HEADS):
        q = proj_ref[:, h * DK:(h + 1) * DK]
        k = proj_ref[:, HEADS * DK + h * DK:HEADS * DK + (h + 1) * DK] * (DK ** -0.5)
        v = proj_ref[:, 2 * HEADS * DK + h * C_DV:2 * HEADS * DK + (h + 1) * C_DV]
        o_pre = proj_ref[:, 2 * HEADS * DK + HEADS * C_DV + h * C_DV:
                         2 * HEADS * DK + HEADS * C_DV + (h + 1) * C_DV]
        bcol = bcum_all[:, HEADS + h:HEADS + h + 1]
        icol = ig_all[:, h:h + 1]
        brow = _col_to_row(bcol, eye)
        irow = _col_to_row(icol, eye)
        m_old = m_all[:, h:h + 1]
        c_old = c_ref[0, h]
        n_old = n_ref[0, h:h + 1, :]

        d_intra = jnp.where(tril, bcol - brow + irow, NEG_BIG)
        b_last = bcol[L - 1:L, :]
        ge_row = b_last - brow + irow
        ge_col = b_last - bcol + icol
        qk = _dot_nt(q, k)
        inter = bcol + m_old
        m_t = jnp.maximum(inter, jnp.max(d_intra, axis=-1, keepdims=True))
        e = jnp.exp(inter - m_t)
        s = qk * jnp.exp(d_intra - m_t)
        num = e * _dot_nt(q, c_old) + _dot(s, v)
        den = e * jnp.sum(q * n_old, axis=-1, keepdims=True) + jnp.sum(s, axis=-1, keepdims=True)
        hh = num / jnp.maximum(jnp.abs(den), jnp.exp(-m_t))
        m_new = jnp.maximum(b_last + m_old, jnp.max(ge_row, axis=-1, keepdims=True))
        sc = jnp.exp(b_last + m_old - m_new)
        w_s = jnp.exp(ge_col - m_new)
        c_ref[0, h] = sc * c_old + _dot_tn(v * w_s, k)
        n_ref[0, h:h + 1, :] = sc * n_old + jnp.sum(k * w_s, axis=0, keepdims=True)
        m_new_all = jnp.where(lane == h, m_new, m_new_all)

        yh = hh * lax.rsqrt(jnp.mean(hh * hh, axis=-1, keepdims=True) + RMS_EPS)
        yh = yh * nw_ref[:, h * C_DV:(h + 1) * C_DV] * _sigmoid(o_pre)
        y_ref[:, h * C_DV:(h + 1) * C_DV] = yh
    m_ref[0] = m_new_all


def _mixer_o(proj, c0, n0, m0, big, bfg, nw, *, B, nc, L, t_valid):
    cv = HEADS * C_DV
    chunk = lambda b, c: (b * nc + c, 0)
    fix2 = lambda b, c: (0, 0)
    return pl.pallas_call(
        functools.partial(_mixer_o_body, L=L, t_valid=t_valid),
        grid=(B, nc),
        in_specs=[pl.BlockSpec((L, PROJ_PAD), chunk),
                  pl.BlockSpec((1, HEADS, C_DV, DK), lambda b, c: (b, 0, 0, 0)),
                  pl.BlockSpec((1, HEADS, DK), lambda b, c: (b, 0, 0)),
                  pl.BlockSpec((1, 1, LANES), lambda b, c: (b, 0, 0)),
                  pl.BlockSpec((1, LANES), fix2),
                  pl.BlockSpec((1, LANES), fix2),
                  pl.BlockSpec((1, cv), fix2)],
        out_specs=[pl.BlockSpec((L, cv), chunk),
                   pl.BlockSpec((1, HEADS, C_DV, DK), lambda b, c: (b, 0, 0, 0)),
                   pl.BlockSpec((1, HEADS, DK), lambda b, c: (b, 0, 0)),
                   pl.BlockSpec((1, 1, LANES), lambda b, c: (b, 0, 0))],
        out_shape=[jax.ShapeDtypeStruct((B * nc * L, cv), F32),
                   jax.ShapeDtypeStruct((B, HEADS, C_DV, DK), F32),
                   jax.ShapeDtypeStruct((B, HEADS, DK), F32),
                   jax.ShapeDtypeStruct((B, 1, LANES), F32)],
        compiler_params=_params(("parallel", "arbitrary")),
        name="mlstm_mixer",
    )(proj, c0, n0, m0, big, bfg, nw)


def _pad_lanes(v, offset=0):
    return jnp.pad(v.astype(F32), (offset, LANES - offset - v.shape[0]))[None, :]


def _pad_proj_weight(w):
    return jnp.pad(w, ((0, 0), (0, PROJ_PAD - w.shape[1]))).astype(BF16)


def _row_tile(m):
    for t in (512, 256, 128, 64, 32, 16, 8):
        if m % t == 0:
            return t
    raise ValueError(f"row count {m} is not a multiple of {SUBLANES}")


def _chunking(T):
    L = math.gcd(T, CHUNK)
    nc = T // L
    Lp = -(-L // SUBLANES) * SUBLANES
    assert nc == 1 or Lp == L
    return nc, Lp, L


def _to_chunks2d(a, nc, Lp, L):
    B, T, N = a.shape
    if Lp != L:
        a = jnp.pad(a, ((0, 0), (0, Lp - L), (0, 0)))
    return a.reshape(B * nc * Lp, N)


def _from_chunks2d(a, B, nc, Lp, L):
    a = a.reshape(B, nc * Lp, a.shape[-1])
    return a[:, :L] if Lp != L else a


def _trunk(x, p, conv, delta, lru, mc, mn, mm, W, depth, alpha):
    B, T, D = x.shape
    M = B * T
    tm = _row_tile(M)
    nc, Lp, L = _chunking(T)
    x2 = x.reshape(M, D)
    conv_o, delta_o, lru_o, mc_o, mn_o, mm_o = [], [], [], [], [], []
    for layer in range(depth):
        j = layer // 2
        if layer % 2 == 0:
            proj = _proj(x2, W['w_in_e'][j], tm, PROJ_TN)
            proj = _to_chunks2d(proj.reshape(B, T, PROJ_PAD), nc, Lp, L)
            conv0 = jnp.pad(conv[j], ((0, 0), (CONV_TAIL - (CONV_W - 1), 0), (0, 0)))
            (y, convn, s_new, h_new), conv_off = _mixer_e(
                proj, conv0, delta[j], lru[j][:, None, :],
                W['w_conv_e'][j], W['b_conv_e'][j][None, :],
                _pad_lanes(W['a_log_e'][j]), _pad_lanes(W['dt_bias_e'][j]),
                W['delta_norm_e'][j][None, :], W['lru_wr_e'][j], W['lru_br_e'][j][None, :],
                W['lru_wi_e'][j], W['lru_bi_e'][j][None, :], W['lru_lambda_e'][j][None, :],
                B=B, nc=nc, L=Lp, t_valid=L)
            conv_o.append(convn[:, conv_off:conv_off + CONV_W - 1])
            delta_o.append(s_new)
            lru_o.append(h_new[:, 0])
            w_out = W['w_out_e'][j]
        else:
            proj = _proj(x2, W['w_in_o'][j], tm, PROJ_TN)
            proj = _to_chunks2d(proj.reshape(B, T, PROJ_PAD), nc, Lp, L)
            m0 = jnp.pad(mm[j], ((0, 0), (0, LANES - HEADS)))[:, None, :]
            y, c_new, n_new, m_new = _mixer_o(
                proj, mc[j], mn[j], m0,
                _pad_lanes(W['b_ig_o'][j]), _pad_lanes(W['b_fg_o'][j], HEADS),
                W['mlstm_norm_o'][j][None, :], B=B, nc=nc, L=Lp, t_valid=L)
            mc_o.append(c_new)
            mn_o.append(n_new)
            mm_o.append(m_new[:, 0, :HEADS])
            w_out = W['w_out_o'][j]
        y = _from_chunks2d(y, B, nc, Lp, L).reshape(M, -1)
        h = _outproj_ln(y, x2, w_out, W['ln1_g'][layer][None, :], W['ln1_b'][layer][None, :],
                        tm, alpha)
        h = _mlp_ln(h, W['w_up'][layer], W['w_down'][layer], W['ln2_g'][layer][None, :],
                    W['ln2_b'][layer][None, :], tm, 512, alpha)
        x2 = _ple(h, p[layer].reshape(M, -1), W['w_ple_gate'][layer], W['w_ple'][layer], tm)
    return (x2.reshape(B, T, D), jnp.stack(conv_o), jnp.stack(delta_o), jnp.stack(lru_o),
            jnp.stack(mc_o), jnp.stack(mn_o), jnp.stack(mm_o))


def kernel(x_prompt, x_sample, p_prompt, p_sample, state_conv, state_delta, state_lru,
           state_mlstm_c, state_mlstm_n, state_mlstm_m, w_in_e, w_conv_e, b_conv_e, a_log_e,
           dt_bias_e, delta_norm_e, lru_wr_e, lru_br_e, lru_wi_e, lru_bi_e, lru_lambda_e, w_out_e,
           w_in_o, b_ig_o, b_fg_o, mlstm_norm_o, w_out_o, ln1_g, ln1_b, ln2_g, ln2_b, w_up, w_down,
           w_ple, w_ple_gate):
    depth = ln1_g.shape[0]
    n_even, n_odd = w_in_e.shape[0], w_in_o.shape[0]
    alpha = (2 * depth) ** 0.25
    W = dict(
        w_in_e=[_pad_proj_weight(w_in_e[j]) for j in range(n_even)],
        w_in_o=[_pad_proj_weight(w_in_o[j]) for j in range(n_odd)],
        w_out_e=w_out_e.astype(BF16), w_out_o=w_out_o.astype(BF16),
        w_up=w_up.astype(BF16), w_down=w_down.astype(BF16),
        w_ple=w_ple.astype(BF16), w_ple_gate=w_ple_gate.astype(BF16),
        w_conv_e=w_conv_e, b_conv_e=b_conv_e, a_log_e=a_log_e, dt_bias_e=dt_bias_e,
        delta_norm_e=delta_norm_e, lru_wr_e=lru_wr_e, lru_br_e=lru_br_e, lru_wi_e=lru_wi_e,
        lru_bi_e=lru_bi_e, lru_lambda_e=lru_lambda_e, b_ig_o=b_ig_o, b_fg_o=b_fg_o,
        mlstm_norm_o=mlstm_norm_o, ln1_g=ln1_g, ln1_b=ln1_b, ln2_g=ln2_g, ln2_b=ln2_b)
    bp = x_prompt.shape[0]
    zeros = lambda a: jnp.zeros((a.shape[0], bp) + a.shape[2:], F32)
    out_p = _trunk(x_prompt, p_prompt, zeros(state_conv), zeros(state_delta), zeros(state_lru),
                   zeros(state_mlstm_c), zeros(state_mlstm_n), zeros(state_mlstm_m), W, depth, alpha)
    out_s = _trunk(x_sample, p_sample, state_conv, state_delta, state_lru,
                   state_mlstm_c, state_mlstm_n, state_mlstm_m, W, depth, alpha)
    return (out_p[0], out_s[0]) + tuple(out_p[1:]) + tuple(out_s[1:])
```

```python
import functools
import math

import jax
import jax.numpy as jnp
from jax import lax
from jax.experimental import pallas as pl
from jax.experimental.pallas import tpu as pltpu

F32 = jnp.float32
BF16 = jnp.bfloat16
HI = lax.Precision.HIGHEST

LANES = 128
SUBLANES = 8
VMEM_LIMIT = 56 * 1024 * 1024

HEADS = 8
DK = 128
A_DV = 128
C_DV = 256
LRU_W = 1024
LRU_BLOCKS = 8
LRU_C = 8.0
CONV_W = 4
CONV_CH = 4096
CHUNK = 64
LN_EPS = 1e-5
RMS_EPS = 1e-6
L2_EPS = 1e-6
PROJ_MAIN = 6144
PROJ_PAD = 6400
PROJ_TN = 1280
CONV_TAIL = SUBLANES
NEG_BIG = -1e30


def _dot_f32(a, b):
    return lax.dot_general(a, b, (((1,), (0,)), ((), ())), precision=HI,
                           preferred_element_type=F32)


def _mm(a, b):
    return lax.dot_general(a.astype(BF16), b.astype(BF16), (((1,), (0,)), ((), ())),
                           preferred_element_type=F32)


def _mm_nt(a, b):
    return lax.dot_general(a.astype(BF16), b.astype(BF16), (((1,), (1,)), ((), ())),
                           preferred_element_type=F32)


def _mm_tn(a, b):
    return lax.dot_general(a.astype(BF16), b.astype(BF16), (((0,), (0,)), ((), ())),
                           preferred_element_type=F32)


def _sigmoid(x):
    return 1.0 / (1.0 + jnp.exp(-x))


def _silu(x):
    return x * _sigmoid(x)


def _softplus(x):
    return jnp.maximum(x, 0.0) + jnp.log1p(jnp.exp(-jnp.abs(x)))


def _gelu_tanh(x):
    return 0.5 * x * (1.0 + jnp.tanh(math.sqrt(2.0 / math.pi) * (x + 0.044715 * (x * x * x))))


def _layer_norm(x, g, b):
    mu = jnp.mean(x, axis=-1, keepdims=True)
    xc = x - mu
    var = jnp.mean(xc * xc, axis=-1, keepdims=True)
    return xc * lax.rsqrt(var + LN_EPS) * g + b


def _col_to_row(col, eye):
    return jnp.sum(jnp.where(eye, col, 0.0), axis=0, keepdims=True)


def _masks(L):
    ri = lax.broadcasted_iota(jnp.int32, (L, L), 0)
    ci = lax.broadcasted_iota(jnp.int32, (L, L), 1)
    return ri >= ci, ri > ci, ri == ci


def _params(sem):
    return pltpu.CompilerParams(dimension_semantics=sem, vmem_limit_bytes=VMEM_LIMIT)


def _proj_body(x_ref, w_ref, o_ref):
    o_ref[...] = jnp.dot(x_ref[...].astype(BF16), w_ref[...], preferred_element_type=F32)


def _proj(x, w, tm, tn):
    M, K = x.shape
    N = w.shape[1]
    return pl.pallas_call(
        _proj_body,
        grid=(M // tm, N // tn),
        in_specs=[pl.BlockSpec((tm, K), lambda i, j: (i, 0)),
                  pl.BlockSpec((K, tn), lambda i, j: (0, j))],
        out_specs=pl.BlockSpec((tm, tn), lambda i, j: (i, j)),
        out_shape=jax.ShapeDtypeStruct((M, N), F32),
        compiler_params=_params(("parallel", "arbitrary")),
        name="in_proj",
    )(x, w)


def _outproj_ln_body(y_ref, x_ref, w_ref, g_ref, b_ref, o_ref, *, alpha):
    mix = jnp.dot(y_ref[...].astype(BF16), w_ref[...], preferred_element_type=F32)
    o_ref[...] = _layer_norm(alpha * x_ref[...] + mix, g_ref[...], b_ref[...])


def _outproj_ln(y, x, w, g, b, tm, alpha):
    M, K = y.shape
    D = w.shape[1]
    row = lambda i: (i, 0)
    fixed = lambda i: (0, 0)
    return pl.pallas_call(
        functools.partial(_outproj_ln_body, alpha=alpha),
        grid=(M // tm,),
        in_specs=[pl.BlockSpec((tm, K), row), pl.BlockSpec((tm, D), row),
                  pl.BlockSpec((K, D), fixed), pl.BlockSpec((1, D), fixed),
                  pl.BlockSpec((1, D), fixed)],
        out_specs=pl.BlockSpec((tm, D), row),
        out_shape=jax.ShapeDtypeStruct((M, D), F32),
        compiler_params=_params(("parallel",)),
        name="out_proj_ln",
    )(y, x, w, g, b)


def _mlp_ln_body(h_ref, wu_ref, wd_ref, g_ref, b_ref, o_ref, hb_ref, acc_ref, *, alpha):
    f = pl.program_id(1)

    @pl.when(f == 0)
    def _():
        hb_ref[...] = h_ref[...].astype(BF16)
        acc_ref[...] = jnp.zeros_like(acc_ref)

    up = jnp.dot(hb_ref[...], wu_ref[...], preferred_element_type=F32)
    act = jnp.square(jnp.maximum(up, 0.0)).astype(BF16)
    acc_ref[...] += jnp.dot(act, wd_ref[...], preferred_element_type=F32)

    @pl.when(f == pl.num_programs(1) - 1)
    def _():
        o_ref[...] = _layer_norm(alpha * h_ref[...] + acc_ref[...], g_ref[...], b_ref[...])


def _mlp_ln(h, wu, wd, g, b, tm, tf, alpha):
    M, D = h.shape
    FF = wu.shape[1]
    return pl.pallas_call(
        functools.partial(_mlp_ln_body, alpha=alpha),
        grid=(M // tm, FF // tf),
        in_specs=[pl.BlockSpec((tm, D), lambda i, f: (i, 0)),
                  pl.BlockSpec((D, tf), lambda i, f: (0, f)),
                  pl.BlockSpec((tf, D), lambda i, f: (f, 0)),
                  pl.BlockSpec((1, D), lambda i, f: (0, 0)),
                  pl.BlockSpec((1, D), lambda i, f: (0, 0))],
        out_specs=pl.BlockSpec((tm, D), lambda i, f: (i, 0)),
        out_shape=jax.ShapeDtypeStruct((M, D), F32),
        scratch_shapes=[pltpu.VMEM((tm, D), BF16), pltpu.VMEM((tm, D), F32)],
        compiler_params=_params(("parallel", "arbitrary")),
        name="mlp_ln",
    )(h, wu, wd, g, b)


def _ple_body(h_ref, p_ref, wg_ref, wp_ref, o_ref):
    h = h_ref[...]
    gate = _sigmoid(jnp.dot(h.astype(BF16), wg_ref[...], preferred_element_type=F32))
    emb = jnp.dot(p_ref[...].astype(BF16), wp_ref[...], preferred_element_type=F32)
    o_ref[...] = h + gate * emb


def _ple(h, p, wg, wp, tm):
    M, D = h.shape
    P = p.shape[1]
    row = lambda i: (i, 0)
    fixed = lambda i: (0, 0)
    return pl.pallas_call(
        _ple_body,
        grid=(M // tm,),
        in_specs=[pl.BlockSpec((tm, D), row), pl.BlockSpec((tm, P), row),
                  pl.BlockSpec((D, D), fixed), pl.BlockSpec((P, D), fixed)],
        out_specs=pl.BlockSpec((tm, D), row),
        out_shape=jax.ShapeDtypeStruct((M, D), F32),
        compiler_params=_params(("parallel",)),
        name="ple_gate",
    )(h, p, wg, wp)


def _mixer_e_body(proj_ref, conv0_ref, s0_ref, h0_ref, wconv_ref, bconv_ref, alog_ref,
                  dtb_ref, nw_ref, wri_ref, br_ref, bi_ref, lam_ref,
                  y_ref, convn_ref, s_ref, hl_ref,
                  cbuf, a_scr, bx_scr, hseq, *, L, t_valid, conv_win):
    c = pl.program_id(1)
    heads = range(HEADS)

    @pl.when(c == 0)
    def _():
        cbuf[0:CONV_TAIL, :] = conv0_ref[0]
        s_ref[0] = s0_ref[0]
        hl_ref[0] = h0_ref[0]

    tril, strict, eye = _masks(L)
    tril_f = jnp.where(tril, 1.0, 0.0).astype(F32)
    row_ok = lax.broadcasted_iota(jnp.int32, (L, 1), 0) < t_valid
    s_old = [s_ref[0, h] for h in heads]
    h_old = hl_ref[0]

    cbuf[CONV_TAIL:CONV_TAIL + L, :] = proj_ref[:, 0:CONV_CH]
    full = cbuf[...]
    acc = bconv_ref[...] + full[CONV_TAIL:, :] * wconv_ref[CONV_W - 1:CONV_W, :]
    for j in range(1, CONV_W):
        shifted = pltpu.roll(full, j, axis=0)[CONV_TAIL:, :]
        acc = acc + shifted * wconv_ref[CONV_W - 1 - j:CONV_W - j, :]

    gp = proj_ref[:, PROJ_MAIN:PROJ_MAIN + LANES]
    g_all = -jnp.exp(alog_ref[...]) * _softplus(gp + dtb_ref[...])
    beta_all = _sigmoid(gp)
    if t_valid < L:
        g_all = jnp.where(row_ok, g_all, 0.0)
        beta_all = jnp.where(row_ok, beta_all, 0.0)
    gc_all = _dot_f32(tril_f, g_all)

    q_l, k_l, kb_l, gcol_l, egc_l, y_l = [], [], [], [], [], []
    for h in heads:
        q = _silu(acc[:, h * DK:(h + 1) * DK])
        k = _silu(acc[:, HEADS * DK + h * DK:HEADS * DK + (h + 1) * DK])
        v = _silu(acc[:, 2 * HEADS * DK + h * A_DV:2 * HEADS * DK + (h + 1) * A_DV])
        q = q * lax.rsqrt(jnp.sum(q * q, axis=-1, keepdims=True) + L2_EPS) * (DK ** -0.5)
        k = k * lax.rsqrt(jnp.sum(k * k, axis=-1, keepdims=True) + L2_EPS)
        gcol = gc_all[:, h:h + 1]
        bcol = beta_all[:, HEADS + h:HEADS + h + 1]
        egc = jnp.exp(gcol)
        kb = k * bcol
        q_l.append(q)
        k_l.append(k)
        kb_l.append(kb)
        gcol_l.append(gcol)
        egc_l.append(egc)
        y_l.append(jnp.concatenate([v * bcol, kb * egc], axis=1))
    kq_k = [_mm_nt(jnp.concatenate([kb_l[h], q_l[h]], axis=0), k_l[h]) for h in heads]
    p_l, qk_l = [], []
    for h in heads:
        grow = _col_to_row(gcol_l[h], eye)
        decay = jnp.where(tril, jnp.exp(jnp.where(tril, gcol_l[h] - grow, 0.0)), 0.0)
        p_l.append(jnp.where(strict, -(kq_k[h][0:L] * decay), 0.0))
        qk_l.append(kq_k[h][L:2 * L] * decay)

    n_levels = max(1, math.ceil(math.log2(t_valid)))
    r_l = p_l
    if n_levels >= 2:
        p_l = [_mm(p_l[h], p_l[h]) for h in heads]
        for _ in range(n_levels - 2):
            out = [_mm(jnp.concatenate([p_l[h], r_l[h]], axis=0), p_l[h]) for h in heads]
            r_l = [r_l[h] + p_l[h] + out[h][L:2 * L] for h in heads]
            p_l = [out[h][0:L] for h in heads]
        out = [_mm(r_l[h], p_l[h]) for h in heads]
        r_l = [r_l[h] + p_l[h] + out[h] for h in heads]
    y_l = [y_l[h] + _mm(r_l[h], y_l[h]) for h in heads]

    wq_s = [_mm(jnp.concatenate([y_l[h][:, A_DV:A_DV + DK], q_l[h] * egc_l[h]], axis=0), s_old[h])
            for h in heads]
    v_new = [y_l[h][:, 0:A_DV] - wq_s[h][0:L] for h in heads]
    o_intra = [_mm(qk_l[h], v_new[h]) for h in heads]
    s_upd = []
    for h in heads:
        g_last = gcol_l[h][L - 1:L, :]
        s_upd.append(s_old[h] * jnp.exp(g_last)
                     + _mm_tn(k_l[h] * jnp.exp(g_last - gcol_l[h]), v_new[h]))

    bw = LRU_W // LRU_BLOCKS
    xr = acc[:, 3 * HEADS * DK:3 * HEADS * DK + LRU_W]
    sp = _softplus(-lam_ref[...])
    for n in range(LRU_BLOCKS):
        sl = slice(n * bw, (n + 1) * bw)
        xb = xr[:, sl]
        ri_pre = _mm(xb, wri_ref[n])
        r_pre = ri_pre[:, 0:bw] + br_ref[:, sl]
        i_pre = ri_pre[:, bw:2 * bw] + bi_ref[:, sl]
        log_a = -LRU_C * _sigmoid(r_pre) * sp[:, sl]
        a = jnp.exp(log_a)
        a_scr[:, sl] = a
        bx_scr[:, sl] = jnp.sqrt(-jnp.tanh(log_a) * (a * a + 1.0)) * _sigmoid(i_pre) * xb

    hc = h_old
    for t in range(L):
        if t < t_valid:
            hc = a_scr[t:t + 1, :] * hc + bx_scr[t:t + 1, :]
        hseq[t:t + 1, :] = hc

    for h in heads:
        o = wq_s[h][L:2 * L] + o_intra[h]
        o = o * lax.rsqrt(jnp.mean(o * o, axis=-1, keepdims=True) + RMS_EPS) * nw_ref[...]
        z = proj_ref[:, CONV_CH + h * A_DV:CONV_CH + (h + 1) * A_DV]
        y_ref[:, h * A_DV:(h + 1) * A_DV] = o * _silu(z)
        s_ref[0, h] = s_upd[h]
    hl_ref[0] = hc
    gate = proj_ref[:, CONV_CH + HEADS * A_DV:CONV_CH + HEADS * A_DV + LRU_W]
    y_ref[:, HEADS * A_DV:HEADS * A_DV + LRU_W] = hseq[...] * _gelu_tanh(gate)

    @pl.when(c == pl.num_programs(1) - 1)
    def _():
        convn_ref[0] = cbuf[conv_win:conv_win + SUBLANES, :]

    cbuf[0:CONV_TAIL, :] = cbuf[L:L + CONV_TAIL, :]


def _mixer_e(proj, conv0, s0, h0, wconv, bconv, alog, dtb, nw, wri, br, bi, lam,
             *, B, nc, L, t_valid):
    conv_row = CONV_TAIL - (CONV_W - 1) + t_valid
    conv_win = (conv_row // SUBLANES) * SUBLANES
    assert conv_row - conv_win + 3 <= SUBLANES
    mix = HEADS * A_DV + LRU_W
    chunk = lambda b, c: (b * nc + c, 0)
    fix2 = lambda b, c: (0, 0)
    fix3 = lambda b, c: (0, 0, 0)
    outs = pl.pallas_call(
        functools.partial(_mixer_e_body, L=L, t_valid=t_valid, conv_win=conv_win),
        grid=(B, nc),
        in_specs=[pl.BlockSpec((L, PROJ_PAD), chunk),
                  pl.BlockSpec((1, SUBLANES, CONV_CH), lambda b, c: (b, 0, 0)),
                  pl.BlockSpec((1, HEADS, DK, A_DV), lambda b, c: (b, 0, 0, 0)),
                  pl.BlockSpec((1, 1, LRU_W), lambda b, c: (b, 0, 0)),
                  pl.BlockSpec((CONV_W, CONV_CH), fix2),
                  pl.BlockSpec((1, CONV_CH), fix2),
                  pl.BlockSpec((1, LANES), fix2),
                  pl.BlockSpec((1, LANES), fix2),
                  pl.BlockSpec((1, A_DV), fix2),
                  pl.BlockSpec((LRU_BLOCKS, LRU_W // LRU_BLOCKS, 2 * LRU_W // LRU_BLOCKS), fix3),
                  pl.BlockSpec((1, LRU_W), fix2),
                  pl.BlockSpec((1, LRU_W), fix2),
                  pl.BlockSpec((1, LRU_W), fix2)],
        out_specs=[pl.BlockSpec((L, mix), chunk),
                   pl.BlockSpec((1, SUBLANES, CONV_CH), lambda b, c: (b, 0, 0)),
                   pl.BlockSpec((1, HEADS, DK, A_DV), lambda b, c: (b, 0, 0, 0)),
                   pl.BlockSpec((1, 1, LRU_W), lambda b, c: (b, 0, 0))],
        out_shape=[jax.ShapeDtypeStruct((B * nc * L, mix), F32),
                   jax.ShapeDtypeStruct((B, SUBLANES, CONV_CH), F32),
                   jax.ShapeDtypeStruct((B, HEADS, DK, A_DV), F32),
                   jax.ShapeDtypeStruct((B, 1, LRU_W), F32)],
        scratch_shapes=[pltpu.VMEM((CONV_TAIL + L, CONV_CH), F32),
                        pltpu.VMEM((L, LRU_W), F32),
                        pltpu.VMEM((L, LRU_W), F32),
                        pltpu.VMEM((L, LRU_W), F32)],
        compiler_params=_params(("parallel", "arbitrary")),
        name="delta_lru_mixer",
    )(proj, conv0, s0, h0, wconv, bconv, alog, dtb, nw, wri, br, bi, lam)
    return outs, conv_row - conv_win


def _mixer_o_body(proj_ref, c0_ref, n0_ref, m0_ref, big_ref, bfg_ref, nw_ref,
                  y_ref, c_ref, n_ref, m_ref, *, L, t_valid):
    ci = pl.program_id(1)
    heads = range(HEADS)

    @pl.when(ci == 0)
    def _():
        c_ref[0] = c0_ref[0]
        n_ref[0] = n0_ref[0]
        m_ref[0] = m0_ref[0]

    tril, _, eye = _masks(L)
    tril_f = jnp.where(tril, 1.0, 0.0).astype(F32)
    row_ok = lax.broadcasted_iota(jnp.int32, (L, 1), 0) < t_valid
    c_old = [c_ref[0, h] for h in heads]
    n_all = n_ref[0]
    m_all = m_ref[0]

    gp = proj_ref[:, PROJ_MAIN:PROJ_MAIN + LANES]
    ig_all = gp + big_ref[...]
    logf_all = -_softplus(-(gp + bfg_ref[...]))
    if t_valid < L:
        ig_all = jnp.where(row_ok, ig_all, NEG_BIG)
        logf_all = jnp.where(row_ok, logf_all, 0.0)
    bcum_all = _dot_f32(tril_f, logf_all)

    q_l = [proj_ref[:, h * DK:(h + 1) * DK] for h in heads]
    k_l = [proj_ref[:, HEADS * DK + h * DK:HEADS * DK + (h + 1) * DK] * (DK ** -0.5) for h in heads]
    v_l = [proj_ref[:, 2 * HEADS * DK + h * C_DV:2 * HEADS * DK + (h + 1) * C_DV] for h in heads]
    q_ck = [_mm_nt(q_l[h], jnp.concatenate([c_old[h], k_l[h]], axis=0)) for h in heads]

    s_l, e_l, mt_l, sc_l, ws_l, mnew_l = [], [], [], [], [], []
    for h in heads:
        bcol = bcum_all[:, HEADS + h:HEADS + h + 1]
        icol = ig_all[:, h:h + 1]
        brow = _col_to_row(bcol, eye)
        irow = _col_to_row(icol, eye)
        m_old = m_all[:, h:h + 1]
        d_intra = jnp.where(tril, bcol - brow + irow, NEG_BIG)
        b_last = bcol[L - 1:L, :]
        ge_row = b_last - brow + irow
        ge_col = b_last - bcol + icol
        inter = bcol + m_old
        m_t = jnp.maximum(inter, jnp.max(d_intra, axis=-1, keepdims=True))
        m_new = jnp.maximum(b_last + m_old, jnp.max(ge_row, axis=-1, keepdims=True))
        e_l.append(jnp.exp(inter - m_t))
        mt_l.append(m_t)
        s_l.append(q_ck[h][:, C_DV:C_DV + L] * jnp.exp(d_intra - m_t))
        sc_l.append(jnp.exp(b_last + m_old - m_new))
        ws_l.append(jnp.exp(ge_col - m_new))
        mnew_l.append(m_new)

    sv = [_mm(s_l[h], v_l[h]) for h in heads]
    kv = [_mm_tn(v_l[h] * ws_l[h], k_l[h]) for h in heads]

    lane = lax.broadcasted_iota(jnp.int32, (1, LANES), 1)
    m_new_all = m_all
    n_rows = []
    for h in heads:
        n_old = n_all[h:h + 1, :]
        num = e_l[h] * q_ck[h][:, 0:C_DV] + sv[h]
        den = (e_l[h] * jnp.sum(q_l[h] * n_old, axis=-1, keepdims=True)
               + jnp.sum(s_l[h], axis=-1, keepdims=True))
        hh = num / jnp.maximum(jnp.abs(den), jnp.exp(-mt_l[h]))
        yh = hh * lax.rsqrt(jnp.mean(hh * hh, axis=-1, keepdims=True) + RMS_EPS)
        o_pre = proj_ref[:, 2 * HEADS * DK + HEADS * C_DV + h * C_DV:
                         2 * HEADS * DK + HEADS * C_DV + (h + 1) * C_DV]
        y_ref[:, h * C_DV:(h + 1) * C_DV] = yh * nw_ref[:, h * C_DV:(h + 1) * C_DV] * _sigmoid(o_pre)
        c_ref[0, h] = sc_l[h] * c_old[h] + kv[h]
        n_rows.append(sc_l[h] * n_old + jnp.sum(k_l[h] * ws_l[h], axis=0, keepdims=True))
        m_new_all = jnp.where(lane == h, mnew_l[h], m_new_all)
    n_ref[0] = jnp.concatenate(n_rows, axis=0)
    m_ref[0] = m_new_all


def _mixer_o(proj, c0, n0, m0, big, bfg, nw, *, B, nc, L, t_valid):
    cv = HEADS * C_DV
    chunk = lambda b, c: (b * nc + c, 0)
    fix2 = lambda b, c: (0, 0)
    return pl.pallas_call(
        functools.partial(_mixer_o_body, L=L, t_valid=t_valid),
        grid=(B, nc),
        in_specs=[pl.BlockSpec((L, PROJ_PAD), chunk),
                  pl.BlockSpec((1, HEADS, C_DV, DK), lambda b, c: (b, 0, 0, 0)),
                  pl.BlockSpec((1, HEADS, DK), lambda b, c: (b, 0, 0)),
                  pl.BlockSpec((1, 1, LANES), lambda b, c: (b, 0, 0)),
                  pl.BlockSpec((1, LANES), fix2),
                  pl.BlockSpec((1, LANES), fix2),
                  pl.BlockSpec((1, cv), fix2)],
        out_specs=[pl.BlockSpec((L, cv), chunk),
                   pl.BlockSpec((1, HEADS, C_DV, DK), lambda b, c: (b, 0, 0, 0)),
                   pl.BlockSpec((1, HEADS, DK), lambda b, c: (b, 0, 0)),
                   pl.BlockSpec((1, 1, LANES), lambda b, c: (b, 0, 0))],
        out_shape=[jax.ShapeDtypeStruct((B * nc * L, cv), F32),
                   jax.ShapeDtypeStruct((B, HEADS, C_DV, DK), F32),
                   jax.ShapeDtypeStruct((B, HEADS, DK), F32),
                   jax.ShapeDtypeStruct((B, 1, LANES), F32)],
        compiler_params=_params(("parallel", "arbitrary")),
        name="mlstm_mixer",
    )(proj, c0, n0, m0, big, bfg, nw)


def _pad_lanes(v, offset=0):
    return jnp.pad(v.astype(F32), (offset, LANES - offset - v.shape[0]))[None, :]


def _pad_proj_weight(w):
    return jnp.pad(w, ((0, 0), (0, PROJ_PAD - w.shape[1]))).astype(BF16)


def _row_tile(m):
    for t in (512, 256, 128, 64, 32, 16, 8):
        if m % t == 0:
            return t
    raise ValueError(f"row count {m} is not a multiple of {SUBLANES}")


def _chunking(T):
    L = math.gcd(T, CHUNK)
    nc = T // L
    Lp = -(-L // SUBLANES) * SUBLANES
    assert nc == 1 or Lp == L
    return nc, Lp, L


def _to_chunks2d(a, nc, Lp, L):
    B, T, N = a.shape
    if Lp != L:
        a = jnp.pad(a, ((0, 0), (0, Lp - L), (0, 0)))
    return a.reshape(B * nc * Lp, N)


def _from_chunks2d(a, B, nc, Lp, L):
    a = a.reshape(B, nc * Lp, a.shape[-1])
    return a[:, :L] if Lp != L else a


def _trunk(x, p, conv, delta, lru, mc, mn, mm, W, depth, alpha):
    B, T, D = x.shape
    M = B * T
    tm = _row_tile(M)
    nc, Lp, L = _chunking(T)
    x2 = x.reshape(M, D)
    conv_o, delta_o, lru_o, mc_o, mn_o, mm_o = [], [], [], [], [], []
    for layer in range(depth):
        j = layer // 2
        if layer % 2 == 0:
            proj = _proj(x2, W['w_in_e'][j], tm, PROJ_TN)
            proj = _to_chunks2d(proj.reshape(B, T, PROJ_PAD), nc, Lp, L)
            conv0 = jnp.pad(conv[j], ((0, 0), (CONV_TAIL - (CONV_W - 1), 0), (0, 0)))
            (y, convn, s_new, h_new), conv_off = _mixer_e(
                proj, conv0, delta[j], lru[j][:, None, :],
                W['w_conv_e'][j], W['b_conv_e'][j][None, :],
                _pad_lanes(W['a_log_e'][j]), _pad_lanes(W['dt_bias_e'][j]),
                W['delta_norm_e'][j][None, :], W['lru_wri_e'][j], W['lru_br_e'][j][None, :],
                W['lru_bi_e'][j][None, :], W['lru_lambda_e'][j][None, :],
                B=B, nc=nc, L=Lp, t_valid=L)
            conv_o.append(convn[:, conv_off:conv_off + CONV_W - 1])
            delta_o.append(s_new)
            lru_o.append(h_new[:, 0])
            w_out = W['w_out_e'][j]
        else:
            proj = _proj(x2, W['w_in_o'][j], tm, PROJ_TN)
            proj = _to_chunks2d(proj.reshape(B, T, PROJ_PAD), nc, Lp, L)
            m0 = jnp.pad(mm[j], ((0, 0), (0, LANES - HEADS)))[:, None, :]
            y, c_new, n_new, m_new = _mixer_o(
                proj, mc[j], mn[j], m0,
                _pad_lanes(W['b_ig_o'][j]), _pad_lanes(W['b_fg_o'][j], HEADS),
                W['mlstm_norm_o'][j][None, :], B=B, nc=nc, L=Lp, t_valid=L)
            mc_o.append(c_new)
            mn_o.append(n_new)
            mm_o.append(m_new[:, 0, :HEADS])
            w_out = W['w_out_o'][j]
        y = _from_chunks2d(y, B, nc, Lp, L).reshape(M, -1)
        h = _outproj_ln(y, x2, w_out, W['ln1_g'][layer][None, :], W['ln1_b'][layer][None, :],
                        tm, alpha)
        h = _mlp_ln(h, W['w_up'][layer], W['w_down'][layer], W['ln2_g'][layer][None, :],
                    W['ln2_b'][layer][None, :], tm, 512, alpha)
        x2 = _ple(h, p[layer].reshape(M, -1), W['w_ple_gate'][layer], W['w_ple'][layer], tm)
    return (x2.reshape(B, T, D), jnp.stack(conv_o), jnp.stack(delta_o), jnp.stack(lru_o),
            jnp.stack(mc_o), jnp.stack(mn_o), jnp.stack(mm_o))


def kernel(x_prompt, x_sample, p_prompt, p_sample, state_conv, state_delta, state_lru,
           state_mlstm_c, state_mlstm_n, state_mlstm_m, w_in_e, w_conv_e, b_conv_e, a_log_e,
           dt_bias_e, delta_norm_e, lru_wr_e, lru_br_e, lru_wi_e, lru_bi_e, lru_lambda_e, w_out_e,
           w_in_o, b_ig_o, b_fg_o, mlstm_norm_o, w_out_o, ln1_g, ln1_b, ln2_g, ln2_b, w_up, w_down,
           w_ple, w_ple_gate):
    depth = ln1_g.shape[0]
    n_even, n_odd = w_in_e.shape[0], w_in_o.shape[0]
    alpha = (2 * depth) ** 0.25
    W = dict(
        w_in_e=[_pad_proj_weight(w_in_e[j]) for j in range(n_even)],
        w_in_o=[_pad_proj_weight(w_in_o[j]) for j in range(n_odd)],
        w_out_e=w_out_e.astype(BF16), w_out_o=w_out_o.astype(BF16),
        w_up=w_up.astype(BF16), w_down=w_down.astype(BF16),
        w_ple=w_ple.astype(BF16), w_ple_gate=w_ple_gate.astype(BF16),
        w_conv_e=w_conv_e, b_conv_e=b_conv_e, a_log_e=a_log_e, dt_bias_e=dt_bias_e,
        delta_norm_e=delta_norm_e, lru_br_e=lru_br_e,
        lru_wri_e=jnp.concatenate([lru_wr_e, lru_wi_e], axis=-1).astype(BF16),
        lru_bi_e=lru_bi_e, lru_lambda_e=lru_lambda_e, b_ig_o=b_ig_o, b_fg_o=b_fg_o,
        mlstm_norm_o=mlstm_norm_o, ln1_g=ln1_g, ln1_b=ln1_b, ln2_g=ln2_g, ln2_b=ln2_b)
    bp = x_prompt.shape[0]
    zeros = lambda a: jnp.zeros((a.shape[0], bp) + a.shape[2:], F32)
    out_p = _trunk(x_prompt, p_prompt, zeros(state_conv), zeros(state_delta), zeros(state_lru),
                   zeros(state_mlstm_c), zeros(state_mlstm_n), zeros(state_mlstm_m), W, depth, alpha)
    out_s = _trunk(x_sample, p_sample, state_conv, state_delta, state_lru,
                   state_mlstm_c, state_mlstm_n, state_mlstm_m, W, depth, alpha)
    return (out_p[0], out_s[0]) + tuple(out_p[1:]) + tuple(out_s[1:])
```

```python
import functools
import math

import jax
import jax.numpy as jnp
from jax import lax
from jax.experimental import pallas as pl
from jax.experimental.pallas import tpu as pltpu

F32 = jnp.float32
BF16 = jnp.bfloat16
HI = lax.Precision.HIGHEST

LANES = 128
SUBLANES = 8
VMEM_LIMIT = 56 * 1024 * 1024

HEADS = 8
DK = 128
A_DV = 128
C_DV = 256
LRU_W = 1024
LRU_BLOCKS = 8
LRU_C = 8.0
CONV_W = 4
CONV_CH = 4096
CHUNK = 64
LN_EPS = 1e-5
RMS_EPS = 1e-6
L2_EPS = 1e-6
PROJ_MAIN = 6144
PROJ_TN = 768
MLP_TF = 512
TM_BIG = 1024
TM_SMALL = 512
CONV_TAIL = SUBLANES
NEG_BIG = -1e30


def _dot_f32(a, b):
    return lax.dot_general(a, b, (((1,), (0,)), ((), ())), precision=HI,
                           preferred_element_type=F32)


def _mm(a, b):
    return lax.dot_general(a.astype(BF16), b.astype(BF16), (((1,), (0,)), ((), ())),
                           preferred_element_type=F32)


def _mm_nt(a, b):
    return lax.dot_general(a.astype(BF16), b.astype(BF16), (((1,), (1,)), ((), ())),
                           preferred_element_type=F32)


def _mm_tn(a, b):
    return lax.dot_general(a.astype(BF16), b.astype(BF16), (((0,), (0,)), ((), ())),
                           preferred_element_type=F32)


def _sigmoid(x):
    return 1.0 / (1.0 + jnp.exp(-x))


def _silu(x):
    return x * _sigmoid(x)


def _softplus(x):
    return jnp.maximum(x, 0.0) + jnp.log1p(jnp.exp(-jnp.abs(x)))


def _gelu_tanh(x):
    return 0.5 * x * (1.0 + jnp.tanh(math.sqrt(2.0 / math.pi) * (x + 0.044715 * (x * x * x))))


def _layer_norm(x, g, b):
    mu = jnp.mean(x, axis=-1, keepdims=True)
    xc = x - mu
    var = jnp.mean(xc * xc, axis=-1, keepdims=True)
    return xc * lax.rsqrt(var + LN_EPS) * g + b


def _col_to_row(col, eye):
    return jnp.sum(jnp.where(eye, col, 0.0), axis=0, keepdims=True)


def _masks(L):
    ri = lax.broadcasted_iota(jnp.int32, (L, L), 0)
    ci = lax.broadcasted_iota(jnp.int32, (L, L), 1)
    return ri >= ci, ri > ci, ri == ci


def _params(sem):
    return pltpu.CompilerParams(dimension_semantics=sem, vmem_limit_bytes=VMEM_LIMIT)


def _proj_body(x_ref, w_ref, wg_ref, o_ref, og_ref, xb_ref):
    @pl.when(pl.program_id(1) == 0)
    def _():
        xb = x_ref[...].astype(BF16)
        xb_ref[...] = xb
        og_ref[...] = jnp.dot(xb, wg_ref[...], preferred_element_type=F32)

    o_ref[...] = jnp.dot(xb_ref[...], w_ref[...].astype(BF16), preferred_element_type=F32)


def _proj(x, w_stack, layer, wg, tm, tn):
    M, K = x.shape
    return pl.pallas_call(
        _proj_body,
        grid=(M // tm, PROJ_MAIN // tn),
        in_specs=[pl.BlockSpec((tm, K), lambda i, j: (i, 0)),
                  pl.BlockSpec((None, K, tn), lambda i, j: (layer, 0, j)),
                  pl.BlockSpec((K, LANES), lambda i, j: (0, 0))],
        out_specs=[pl.BlockSpec((tm, tn), lambda i, j: (i, j)),
                   pl.BlockSpec((tm, LANES), lambda i, j: (i, 0))],
        out_shape=[jax.ShapeDtypeStruct((M, PROJ_MAIN), F32),
                   jax.ShapeDtypeStruct((M, LANES), F32)],
        scratch_shapes=[pltpu.VMEM((tm, K), BF16)],
        compiler_params=_params(("parallel", "arbitrary")),
        name="in_proj",
    )(x, w_stack, wg)


def _outproj_ln_body(y_ref, x_ref, w_ref, g_ref, b_ref, o_ref, *, alpha):
    mix = jnp.dot(y_ref[...].astype(BF16), w_ref[...], preferred_element_type=F32)
    o_ref[...] = _layer_norm(alpha * x_ref[...] + mix, g_ref[...], b_ref[...])


def _outproj_ln(y, x, w_stack, layer, g, b, tm, alpha):
    M, K = y.shape
    D = w_stack.shape[2]
    row = lambda i: (i, 0)
    fixed = lambda i: (0, 0)
    return pl.pallas_call(
        functools.partial(_outproj_ln_body, alpha=alpha),
        grid=(M // tm,),
        in_specs=[pl.BlockSpec((tm, K), row), pl.BlockSpec((tm, D), row),
                  pl.BlockSpec((None, K, D), lambda i: (layer, 0, 0), pipeline_mode=pl.Buffered(1)),
                  pl.BlockSpec((1, D), fixed), pl.BlockSpec((1, D), fixed)],
        out_specs=pl.BlockSpec((tm, D), row),
        out_shape=jax.ShapeDtypeStruct((M, D), F32),
        compiler_params=_params(("parallel",)),
        name="out_proj_ln",
    )(y, x, w_stack, g, b)


def _mlp_ln_body(h_ref, wu_ref, wd_ref, g_ref, b_ref, o_ref, hb_ref, *, alpha):
    f = pl.program_id(1)

    @pl.when(f == 0)
    def _():
        hb_ref[...] = h_ref[...].astype(BF16)

    up = jnp.dot(hb_ref[...], wu_ref[...].astype(BF16), preferred_element_type=F32)
    act = jnp.square(jnp.maximum(up, 0.0)).astype(BF16)
    part = jnp.dot(act, wd_ref[...].astype(BF16), preferred_element_type=F32)

    @pl.when(f == 0)
    def _():
        o_ref[...] = part

    @pl.when(f > 0)
    def _():
        o_ref[...] += part

    @pl.when(f == pl.num_programs(1) - 1)
    def _():
        o_ref[...] = _layer_norm(alpha * h_ref[...] + o_ref[...], g_ref[...], b_ref[...])


def _mlp_ln(h, wu_stack, wd_stack, layer, g, b, tm, tf, alpha):
    M, D = h.shape
    FF = wu_stack.shape[2]
    return pl.pallas_call(
        functools.partial(_mlp_ln_body, alpha=alpha),
        grid=(M // tm, FF // tf),
        in_specs=[pl.BlockSpec((tm, D), lambda i, f: (i, 0), pipeline_mode=pl.Buffered(1)),
                  pl.BlockSpec((None, D, tf), lambda i, f: (layer, 0, f)),
                  pl.BlockSpec((None, tf, D), lambda i, f: (layer, f, 0)),
                  pl.BlockSpec((1, D), lambda i, f: (0, 0)),
                  pl.BlockSpec((1, D), lambda i, f: (0, 0))],
        out_specs=pl.BlockSpec((tm, D), lambda i, f: (i, 0), pipeline_mode=pl.Buffered(1)),
        out_shape=jax.ShapeDtypeStruct((M, D), F32),
        scratch_shapes=[pltpu.VMEM((tm, D), BF16)],
        compiler_params=_params(("parallel", "arbitrary")),
        name="mlp_ln",
    )(h, wu_stack, wd_stack, g, b)


def _ple_body(h_ref, p_ref, wg_ref, wp_ref, o_ref):
    h = h_ref[...]
    gate = _sigmoid(jnp.dot(h.astype(BF16), wg_ref[...], preferred_element_type=F32))
    emb = jnp.dot(p_ref[...].astype(BF16), wp_ref[...], preferred_element_type=F32)
    o_ref[...] = h + gate * emb


def _ple(h, p_stack, layer, wg_stack, wp_stack, tm):
    M, D = h.shape
    P = p_stack.shape[2]
    row = lambda i: (i, 0)
    return pl.pallas_call(
        _ple_body,
        grid=(M // tm,),
        in_specs=[pl.BlockSpec((tm, D), row),
                  pl.BlockSpec((None, tm, P), lambda i: (layer, i, 0)),
                  pl.BlockSpec((None, D, D), lambda i: (layer, 0, 0), pipeline_mode=pl.Buffered(1)),
                  pl.BlockSpec((None, P, D), lambda i: (layer, 0, 0), pipeline_mode=pl.Buffered(1))],
        out_specs=pl.BlockSpec((tm, D), row),
        out_shape=jax.ShapeDtypeStruct((M, D), F32),
        compiler_params=_params(("parallel",)),
        name="ple_gate",
    )(h, p_stack, wg_stack, wp_stack)


def _mixer_e_body(proj_ref, gate_ref, conv0_ref, s0_ref, h0_ref, wconv_ref, bconv_ref, alog_ref,
                  dtb_ref, nw_ref, wri_ref, br_ref, bi_ref, lam_ref,
                  y_ref, convn_ref, s_ref, hl_ref,
                  cbuf, a_scr, bx_scr, hseq, *, L, t_valid, conv_win):
    c = pl.program_id(1)
    heads = range(HEADS)

    @pl.when(c == 0)
    def _():
        cbuf[0:CONV_TAIL, :] = conv0_ref[0]
        s_ref[0] = s0_ref[0]
        hl_ref[0] = h0_ref[0]

    tril, strict, eye = _masks(L)
    tril_f = jnp.where(tril, 1.0, 0.0).astype(F32)
    row_ok = lax.broadcasted_iota(jnp.int32, (L, 1), 0) < t_valid
    s_old = [s_ref[0, h] for h in heads]
    h_old = hl_ref[0]

    cbuf[CONV_TAIL:CONV_TAIL + L, :] = proj_ref[:, 0:CONV_CH]
    full = cbuf[...]
    acc = bconv_ref[...] + full[CONV_TAIL:, :] * wconv_ref[CONV_W - 1:CONV_W, :]
    for j in range(1, CONV_W):
        shifted = pltpu.roll(full, j, axis=0)[CONV_TAIL:, :]
        acc = acc + shifted * wconv_ref[CONV_W - 1 - j:CONV_W - j, :]

    gp = gate_ref[...]
    g_all = -jnp.exp(alog_ref[...]) * _softplus(gp + dtb_ref[...])
    beta_all = _sigmoid(gp)
    if t_valid < L:
        g_all = jnp.where(row_ok, g_all, 0.0)
        beta_all = jnp.where(row_ok, beta_all, 0.0)
    gc_all = _dot_f32(tril_f, g_all)

    q_l, k_l, kb_l, gcol_l, egc_l, y_l = [], [], [], [], [], []
    for h in heads:
        q = _silu(acc[:, h * DK:(h + 1) * DK])
        k = _silu(acc[:, HEADS * DK + h * DK:HEADS * DK + (h + 1) * DK])
        v = _silu(acc[:, 2 * HEADS * DK + h * A_DV:2 * HEADS * DK + (h + 1) * A_DV])
        q = q * lax.rsqrt(jnp.sum(q * q, axis=-1, keepdims=True) + L2_EPS) * (DK ** -0.5)
        k = k * lax.rsqrt(jnp.sum(k * k, axis=-1, keepdims=True) + L2_EPS)
        gcol = gc_all[:, h:h + 1]
        bcol = beta_all[:, HEADS + h:HEADS + h + 1]
        egc = jnp.exp(gcol)
        kb = k * bcol
        q_l.append(q)
        k_l.append(k)
        kb_l.append(kb)
        gcol_l.append(gcol)
        egc_l.append(egc)
        y_l.append(jnp.concatenate([v * bcol, kb * egc], axis=1))
    kq_k = [_mm_nt(jnp.concatenate([kb_l[h], q_l[h]], axis=0), k_l[h]) for h in heads]
    p_l, qk_l = [], []
    for h in heads:
        grow = _col_to_row(gcol_l[h], eye)
        decay = jnp.where(tril, jnp.exp(jnp.where(tril, gcol_l[h] - grow, 0.0)), 0.0)
        p_l.append(jnp.where(strict, -(kq_k[h][0:L] * decay), 0.0))
        qk_l.append(kq_k[h][L:2 * L] * decay)

    n_levels = max(1, math.ceil(math.log2(t_valid)))
    r_l = p_l
    if n_levels >= 2:
        p_l = [_mm(p_l[h], p_l[h]) for h in heads]
        for _ in range(n_levels - 2):
            out = [_mm(jnp.concatenate([p_l[h], r_l[h]], axis=0), p_l[h]) for h in heads]
            r_l = [r_l[h] + p_l[h] + out[h][L:2 * L] for h in heads]
            p_l = [out[h][0:L] for h in heads]
        out = [_mm(r_l[h], p_l[h]) for h in heads]
        r_l = [r_l[h] + p_l[h] + out[h] for h in heads]
    y_l = [y_l[h] + _mm(r_l[h], y_l[h]) for h in heads]

    wq_s = [_mm(jnp.concatenate([y_l[h][:, A_DV:A_DV + DK], q_l[h] * egc_l[h]], axis=0), s_old[h])
            for h in heads]
    v_new = [y_l[h][:, 0:A_DV] - wq_s[h][0:L] for h in heads]
    o_intra = [_mm(qk_l[h], v_new[h]) for h in heads]
    s_upd = []
    for h in heads:
        g_last = gcol_l[h][L - 1:L, :]
        s_upd.append(s_old[h] * jnp.exp(g_last)
                     + _mm_tn(k_l[h] * jnp.exp(g_last - gcol_l[h]), v_new[h]))

    bw = LRU_W // LRU_BLOCKS
    xr = acc[:, 3 * HEADS * DK:3 * HEADS * DK + LRU_W]
    sp = _softplus(-lam_ref[...])
    for n in range(LRU_BLOCKS):
        sl = slice(n * bw, (n + 1) * bw)
        xb = xr[:, sl]
        ri_pre = _mm(xb, wri_ref[n])
        r_pre = ri_pre[:, 0:bw] + br_ref[:, sl]
        i_pre = ri_pre[:, bw:2 * bw] + bi_ref[:, sl]
        log_a = -LRU_C * _sigmoid(r_pre) * sp[:, sl]
        a = jnp.exp(log_a)
        a_scr[:, sl] = a
        bx_scr[:, sl] = jnp.sqrt(-jnp.tanh(log_a) * (a * a + 1.0)) * _sigmoid(i_pre) * xb

    hc = h_old
    for t in range(L):
        if t < t_valid:
            hc = a_scr[t:t + 1, :] * hc + bx_scr[t:t + 1, :]
        hseq[t:t + 1, :] = hc

    for h in heads:
        o = wq_s[h][L:2 * L] + o_intra[h]
        o = o * lax.rsqrt(jnp.mean(o * o, axis=-1, keepdims=True) + RMS_EPS) * nw_ref[...]
        z = proj_ref[:, CONV_CH + h * A_DV:CONV_CH + (h + 1) * A_DV]
        y_ref[:, h * A_DV:(h + 1) * A_DV] = o * _silu(z)
        s_ref[0, h] = s_upd[h]
    hl_ref[0] = hc
    gate = proj_ref[:, CONV_CH + HEADS * A_DV:CONV_CH + HEADS * A_DV + LRU_W]
    y_ref[:, HEADS * A_DV:HEADS * A_DV + LRU_W] = hseq[...] * _gelu_tanh(gate)

    @pl.when(c == pl.num_programs(1) - 1)
    def _():
        convn_ref[0] = cbuf[conv_win:conv_win + SUBLANES, :]

    cbuf[0:CONV_TAIL, :] = cbuf[L:L + CONV_TAIL, :]


def _mixer_e(proj, gates, conv0, s0, h0, wconv, bconv, alog, dtb, nw, wri, br, bi, lam,
             *, B, nc, L, t_valid):
    conv_row = CONV_TAIL - (CONV_W - 1) + t_valid
    conv_win = (conv_row // SUBLANES) * SUBLANES
    assert conv_row - conv_win + 3 <= SUBLANES
    mix = HEADS * A_DV + LRU_W
    chunk = lambda b, c: (b * nc + c, 0)
    fix2 = lambda b, c: (0, 0)
    fix3 = lambda b, c: (0, 0, 0)
    outs = pl.pallas_call(
        functools.partial(_mixer_e_body, L=L, t_valid=t_valid, conv_win=conv_win),
        grid=(B, nc),
        in_specs=[pl.BlockSpec((L, PROJ_MAIN), chunk),
                  pl.BlockSpec((L, LANES), chunk),
                  pl.BlockSpec((1, SUBLANES, CONV_CH), lambda b, c: (b, 0, 0)),
                  pl.BlockSpec((1, HEADS, DK, A_DV), lambda b, c: (b, 0, 0, 0)),
                  pl.BlockSpec((1, 1, LRU_W), lambda b, c: (b, 0, 0)),
                  pl.BlockSpec((CONV_W, CONV_CH), fix2),
                  pl.BlockSpec((1, CONV_CH), fix2),
                  pl.BlockSpec((1, LANES), fix2),
                  pl.BlockSpec((1, LANES), fix2),
                  pl.BlockSpec((1, A_DV), fix2),
                  pl.BlockSpec((LRU_BLOCKS, LRU_W // LRU_BLOCKS, 2 * LRU_W // LRU_BLOCKS), fix3),
                  pl.BlockSpec((1, LRU_W), fix2),
                  pl.BlockSpec((1, LRU_W), fix2),
                  pl.BlockSpec((1, LRU_W), fix2)],
        out_specs=[pl.BlockSpec((L, mix), chunk),
                   pl.BlockSpec((1, SUBLANES, CONV_CH), lambda b, c: (b, 0, 0)),
                   pl.BlockSpec((1, HEADS, DK, A_DV), lambda b, c: (b, 0, 0, 0)),
                   pl.BlockSpec((1, 1, LRU_W), lambda b, c: (b, 0, 0))],
        out_shape=[jax.ShapeDtypeStruct((B * nc * L, mix), F32),
                   jax.ShapeDtypeStruct((B, SUBLANES, CONV_CH), F32),
                   jax.ShapeDtypeStruct((B, HEADS, DK, A_DV), F32),
                   jax.ShapeDtypeStruct((B, 1, LRU_W), F32)],
        scratch_shapes=[pltpu.VMEM((CONV_TAIL + L, CONV_CH), F32),
                        pltpu.VMEM((L, LRU_W), F32),
                        pltpu.VMEM((L, LRU_W), F32),
                        pltpu.VMEM((L, LRU_W), F32)],
        compiler_params=_params(("parallel", "arbitrary")),
        name="delta_lru_mixer",
    )(proj, gates, conv0, s0, h0, wconv, bconv, alog, dtb, nw, wri, br, bi, lam)
    return outs, conv_row - conv_win


def _mixer_o_body(proj_ref, gate_ref, c0_ref, n0_ref, m0_ref, big_ref, bfg_ref, nw_ref,
                  y_ref, c_ref, n_ref, m_ref, *, L, t_valid):
    ci = pl.program_id(1)
    heads = range(HEADS)

    @pl.when(ci == 0)
    def _():
        c_ref[0] = c0_ref[0]
        n_ref[0] = n0_ref[0]
        m_ref[0] = m0_ref[0]

    tril, _, eye = _masks(L)
    tril_f = jnp.where(tril, 1.0, 0.0).astype(F32)
    row_ok = lax.broadcasted_iota(jnp.int32, (L, 1), 0) < t_valid
    c_old = [c_ref[0, h] for h in heads]
    n_all = n_ref[0]
    m_all = m_ref[0]

    gp = gate_ref[...]
    ig_all = gp + big_ref[...]
    logf_all = -_softplus(-(gp + bfg_ref[...]))
    if t_valid < L:
        ig_all = jnp.where(row_ok, ig_all, NEG_BIG)
        logf_all = jnp.where(row_ok, logf_all, 0.0)
    bcum_all = _dot_f32(tril_f, logf_all)

    q_l = [proj_ref[:, h * DK:(h + 1) * DK] for h in heads]
    k_l = [proj_ref[:, HEADS * DK + h * DK:HEADS * DK + (h + 1) * DK] * (DK ** -0.5) for h in heads]
    v_l = [proj_ref[:, 2 * HEADS * DK + h * C_DV:2 * HEADS * DK + (h + 1) * C_DV] for h in heads]
    q_ck = [_mm_nt(q_l[h], jnp.concatenate([c_old[h], k_l[h]], axis=0)) for h in heads]

    s_l, e_l, mt_l, sc_l, ws_l, mnew_l = [], [], [], [], [], []
    for h in heads:
        bcol = bcum_all[:, HEADS + h:HEADS + h + 1]
        icol = ig_all[:, h:h + 1]
        brow = _col_to_row(bcol, eye)
        irow = _col_to_row(icol, eye)
        m_old = m_all[:, h:h + 1]
        d_intra = jnp.where(tril, bcol - brow + irow, NEG_BIG)
        b_last = bcol[L - 1:L, :]
        ge_row = b_last - brow + irow
        ge_col = b_last - bcol + icol
        inter = bcol + m_old
        m_t = jnp.maximum(inter, jnp.max(d_intra, axis=-1, keepdims=True))
        m_new = jnp.maximum(b_last + m_old, jnp.max(ge_row, axis=-1, keepdims=True))
        e_l.append(jnp.exp(inter - m_t))
        mt_l.append(m_t)
        s_l.append(q_ck[h][:, C_DV:C_DV + L] * jnp.exp(d_intra - m_t))
        sc_l.append(jnp.exp(b_last + m_old - m_new))
        ws_l.append(jnp.exp(ge_col - m_new))
        mnew_l.append(m_new)

    sv = [_mm(s_l[h], v_l[h]) for h in heads]
    kv = [_mm_tn(v_l[h] * ws_l[h], k_l[h]) for h in heads]

    lane = lax.broadcasted_iota(jnp.int32, (1, LANES), 1)
    m_new_all = m_all
    n_rows = []
    for h in heads:
        n_old = n_all[h:h + 1, :]
        num = e_l[h] * q_ck[h][:, 0:C_DV] + sv[h]
        den = (e_l[h] * jnp.sum(q_l[h] * n_old, axis=-1, keepdims=True)
               + jnp.sum(s_l[h], axis=-1, keepdims=True))
        hh = num / jnp.maximum(jnp.abs(den), jnp.exp(-mt_l[h]))
        yh = hh * lax.rsqrt(jnp.mean(hh * hh, axis=-1, keepdims=True) + RMS_EPS)
        o_pre = proj_ref[:, 2 * HEADS * DK + HEADS * C_DV + h * C_DV:
                         2 * HEADS * DK + HEADS * C_DV + (h + 1) * C_DV]
        y_ref[:, h * C_DV:(h + 1) * C_DV] = yh * nw_ref[:, h * C_DV:(h + 1) * C_DV] * _sigmoid(o_pre)
        c_ref[0, h] = sc_l[h] * c_old[h] + kv[h]
        n_rows.append(sc_l[h] * n_old + jnp.sum(k_l[h] * ws_l[h], axis=0, keepdims=True))
        m_new_all = jnp.where(lane == h, mnew_l[h], m_new_all)
    n_ref[0] = jnp.concatenate(n_rows, axis=0)
    m_ref[0] = m_new_all


def _mixer_o(proj, gates, c0, n0, m0, big, bfg, nw, *, B, nc, L, t_valid):
    cv = HEADS * C_DV
    chunk = lambda b, c: (b * nc + c, 0)
    fix2 = lambda b, c: (0, 0)
    return pl.pallas_call(
        functools.partial(_mixer_o_body, L=L, t_valid=t_valid),
        grid=(B, nc),
        in_specs=[pl.BlockSpec((L, PROJ_MAIN), chunk),
                  pl.BlockSpec((L, LANES), chunk),
                  pl.BlockSpec((1, HEADS, C_DV, DK), lambda b, c: (b, 0, 0, 0)),
                  pl.BlockSpec((1, HEADS, DK), lambda b, c: (b, 0, 0)),
                  pl.BlockSpec((1, 1, LANES), lambda b, c: (b, 0, 0)),
                  pl.BlockSpec((1, LANES), fix2),
                  pl.BlockSpec((1, LANES), fix2),
                  pl.BlockSpec((1, cv), fix2)],
        out_specs=[pl.BlockSpec((L, cv), chunk),
                   pl.BlockSpec((1, HEADS, C_DV, DK), lambda b, c: (b, 0, 0, 0)),
                   pl.BlockSpec((1, HEADS, DK), lambda b, c: (b, 0, 0)),
                   pl.BlockSpec((1, 1, LANES), lambda b, c: (b, 0, 0))],
        out_shape=[jax.ShapeDtypeStruct((B * nc * L, cv), F32),
                   jax.ShapeDtypeStruct((B, HEADS, C_DV, DK), F32),
                   jax.ShapeDtypeStruct((B, HEADS, DK), F32),
                   jax.ShapeDtypeStruct((B, 1, LANES), F32)],
        compiler_params=_params(("parallel", "arbitrary")),
        name="mlstm_mixer",
    )(proj, gates, c0, n0, m0, big, bfg, nw)


def _pad_lanes(v, offset=0):
    return jnp.pad(v.astype(F32), (offset, LANES - offset - v.shape[0]))[None, :]


def _gate_weight(w):
    wg = w[:, PROJ_MAIN:]
    return jnp.pad(wg, ((0, 0), (0, LANES - wg.shape[1]))).astype(BF16)


def _row_tile(m, cap):
    t = cap
    while t >= SUBLANES:
        if m % t == 0:
            return t
        t //= 2
    raise ValueError(f"row count {m} is not a multiple of {SUBLANES}")


def _chunking(T):
    L = math.gcd(T, CHUNK)
    nc = T // L
    Lp = -(-L // SUBLANES) * SUBLANES
    assert nc == 1 or Lp == L
    return nc, Lp, L


def _to_chunks2d(a, nc, Lp, L):
    B, T, N = a.shape
    if Lp != L:
        a = jnp.pad(a, ((0, 0), (0, Lp - L), (0, 0)))
    return a.reshape(B * nc * Lp, N)


def _from_chunks2d(a, B, nc, Lp, L):
    a = a.reshape(B, nc * Lp, a.shape[-1])
    return a[:, :L] if Lp != L else a


def _trunk(x, p, conv, delta, lru, mc, mn, mm, W, depth, alpha):
    B, T, D = x.shape
    M = B * T
    tm = _row_tile(M, TM_SMALL)
    tm_big = _row_tile(M, TM_BIG)
    nc, Lp, L = _chunking(T)
    x2 = x.reshape(M, D)
    p2 = p.reshape(depth, M, -1)
    chunks = lambda a: _to_chunks2d(a.reshape(B, T, -1), nc, Lp, L)
    conv_o, delta_o, lru_o, mc_o, mn_o, mm_o = [], [], [], [], [], []
    for layer in range(depth):
        j = layer // 2
        if layer % 2 == 0:
            proj, gates = _proj(x2, W['w_in_e'], j, W['wg_e'][j], tm_big, PROJ_TN)
            conv0 = jnp.pad(conv[j], ((0, 0), (CONV_TAIL - (CONV_W - 1), 0), (0, 0)))
            (y, convn, s_new, h_new), conv_off = _mixer_e(
                chunks(proj), chunks(gates), conv0, delta[j], lru[j][:, None, :],
                W['w_conv_e'][j], W['b_conv_e'][j][None, :],
                _pad_lanes(W['a_log_e'][j]), _pad_lanes(W['dt_bias_e'][j]),
                W['delta_norm_e'][j][None, :], W['lru_wri_e'][j], W['lru_br_e'][j][None, :],
                W['lru_bi_e'][j][None, :], W['lru_lambda_e'][j][None, :],
                B=B, nc=nc, L=Lp, t_valid=L)
            conv_o.append(convn[:, conv_off:conv_off + CONV_W - 1])
            delta_o.append(s_new)
            lru_o.append(h_new[:, 0])
            w_out = W['w_out_e']
        else:
            proj, gates = _proj(x2, W['w_in_o'], j, W['wg_o'][j], tm_big, PROJ_TN)
            m0 = jnp.pad(mm[j], ((0, 0), (0, LANES - HEADS)))[:, None, :]
            y, c_new, n_new, m_new = _mixer_o(
                chunks(proj), chunks(gates), mc[j], mn[j], m0,
                _pad_lanes(W['b_ig_o'][j]), _pad_lanes(W['b_fg_o'][j], HEADS),
                W['mlstm_norm_o'][j][None, :], B=B, nc=nc, L=Lp, t_valid=L)
            mc_o.append(c_new)
            mn_o.append(n_new)
            mm_o.append(m_new[:, 0, :HEADS])
            w_out = W['w_out_o']
        y = _from_chunks2d(y, B, nc, Lp, L).reshape(M, -1)
        h = _outproj_ln(y, x2, w_out, j, W['ln1_g'][layer][None, :], W['ln1_b'][layer][None, :],
                        tm, alpha)
        h = _mlp_ln(h, W['w_up'], W['w_down'], layer, W['ln2_g'][layer][None, :],
                    W['ln2_b'][layer][None, :], tm_big, MLP_TF, alpha)
        x2 = _ple(h, p2, layer, W['w_ple_gate'], W['w_ple'], tm)
    return (x2.reshape(B, T, D), jnp.stack(conv_o), jnp.stack(delta_o), jnp.stack(lru_o),
            jnp.stack(mc_o), jnp.stack(mn_o), jnp.stack(mm_o))


def kernel(x_prompt, x_sample, p_prompt, p_sample, state_conv, state_delta, state_lru,
           state_mlstm_c, state_mlstm_n, state_mlstm_m, w_in_e, w_conv_e, b_conv_e, a_log_e,
           dt_bias_e, delta_norm_e, lru_wr_e, lru_br_e, lru_wi_e, lru_bi_e, lru_lambda_e, w_out_e,
           w_in_o, b_ig_o, b_fg_o, mlstm_norm_o, w_out_o, ln1_g, ln1_b, ln2_g, ln2_b, w_up, w_down,
           w_ple, w_ple_gate):
    depth = ln1_g.shape[0]
    n_even, n_odd = w_in_e.shape[0], w_in_o.shape[0]
    alpha = (2 * depth) ** 0.25
    W = dict(
        w_in_e=w_in_e, w_in_o=w_in_o,
        wg_e=[_gate_weight(w_in_e[j]) for j in range(n_even)],
        wg_o=[_gate_weight(w_in_o[j]) for j in range(n_odd)],
        w_out_e=w_out_e.astype(BF16), w_out_o=w_out_o.astype(BF16),
        w_up=w_up, w_down=w_down,
        w_ple=w_ple.astype(BF16), w_ple_gate=w_ple_gate.astype(BF16),
        w_conv_e=w_conv_e, b_conv_e=b_conv_e, a_log_e=a_log_e, dt_bias_e=dt_bias_e,
        delta_norm_e=delta_norm_e, lru_br_e=lru_br_e,
        lru_wri_e=jnp.concatenate([lru_wr_e, lru_wi_e], axis=-1).astype(BF16),
        lru_bi_e=lru_bi_e, lru_lambda_e=lru_lambda_e, b_ig_o=b_ig_o, b_fg_o=b_fg_o,
        mlstm_norm_o=mlstm_norm_o, ln1_g=ln1_g, ln1_b=ln1_b, ln2_g=ln2_g, ln2_b=ln2_b)
    bp = x_prompt.shape[0]
    zeros = lambda a: jnp.zeros((a.shape[0], bp) + a.shape[2:], F32)
    out_p = _trunk(x_prompt, p_prompt, zeros(state_conv), zeros(state_delta), zeros(state_lru),
                   zeros(state_mlstm_c), zeros(state_mlstm_n), zeros(state_mlstm_m), W, depth, alpha)
    out_s = _trunk(x_sample, p_sample, state_conv, state_delta, state_lru,
                   state_mlstm_c, state_mlstm_n, state_mlstm_m, W, depth, alpha)
    return (out_p[0], out_s[0]) + tuple(out_p[1:]) + tuple(out_s[1:])
```

```python
import functools
import math

import jax
import jax.numpy as jnp
from jax import lax
from jax.experimental import pallas as pl
from jax.experimental.pallas import tpu as pltpu

F32 = jnp.float32
BF16 = jnp.bfloat16
HI = lax.Precision.HIGHEST

LANES = 128
SUBLANES = 8
VMEM_LIMIT = 56 * 1024 * 1024

HEADS = 8
DK = 128
A_DV = 128
C_DV = 256
LRU_W = 1024
LRU_BLOCKS = 8
LRU_C = 8.0
CONV_W = 4
CONV_CH = 4096
CHUNK = 64
LN_EPS = 1e-5
RMS_EPS = 1e-6
L2_EPS = 1e-6
PROJ_MAIN = 6144
PROJ_TN = 768
MLP_TF = 512
TM_BIG = 1024
TM_SMALL = 512
CONV_TAIL = SUBLANES
NEG_BIG = -1e30


def _dot_f32(a, b):
    return lax.dot_general(a, b, (((1,), (0,)), ((), ())), precision=HI,
                           preferred_element_type=F32)


def _mm(a, b):
    return lax.dot_general(a.astype(BF16), b.astype(BF16), (((1,), (0,)), ((), ())),
                           preferred_element_type=F32)


def _mm_nt(a, b):
    return lax.dot_general(a.astype(BF16), b.astype(BF16), (((1,), (1,)), ((), ())),
                           preferred_element_type=F32)


def _mm_tn(a, b):
    return lax.dot_general(a.astype(BF16), b.astype(BF16), (((0,), (0,)), ((), ())),
                           preferred_element_type=F32)


def _sigmoid(x):
    return 1.0 / (1.0 + jnp.exp(-x))


def _silu(x):
    return x * _sigmoid(x)


def _softplus(x):
    return jnp.maximum(x, 0.0) + jnp.log1p(jnp.exp(-jnp.abs(x)))


def _gelu_tanh(x):
    return 0.5 * x * (1.0 + jnp.tanh(math.sqrt(2.0 / math.pi) * (x + 0.044715 * (x * x * x))))


def _layer_norm(x, g, b):
    mu = jnp.mean(x, axis=-1, keepdims=True)
    xc = x - mu
    var = jnp.mean(xc * xc, axis=-1, keepdims=True)
    return xc * lax.rsqrt(var + LN_EPS) * g + b


def _col_to_row(col, eye):
    return jnp.sum(jnp.where(eye, col, 0.0), axis=0, keepdims=True)


def _masks(L):
    ri = lax.broadcasted_iota(jnp.int32, (L, L), 0)
    ci = lax.broadcasted_iota(jnp.int32, (L, L), 1)
    return ri >= ci, ri > ci, ri == ci


def _params(sem):
    return pltpu.CompilerParams(dimension_semantics=sem, vmem_limit_bytes=VMEM_LIMIT)


def _proj_body(x_ref, w_ref, wg_ref, o_ref, og_ref, xb_ref):
    @pl.when(pl.program_id(1) == 0)
    def _():
        xb = x_ref[...].astype(BF16)
        xb_ref[...] = xb
        og_ref[...] = jnp.dot(xb, wg_ref[...], preferred_element_type=F32)

    o_ref[...] = jnp.dot(xb_ref[...], w_ref[...], preferred_element_type=F32)


def _proj(x, w_stack, layer, wg, tm, tn):
    M, K = x.shape
    return pl.pallas_call(
        _proj_body,
        grid=(M // tm, PROJ_MAIN // tn),
        in_specs=[pl.BlockSpec((tm, K), lambda i, j: (i, 0)),
                  pl.BlockSpec((None, K, tn), lambda i, j: (layer, 0, j)),
                  pl.BlockSpec((K, LANES), lambda i, j: (0, 0))],
        out_specs=[pl.BlockSpec((tm, tn), lambda i, j: (i, j)),
                   pl.BlockSpec((tm, LANES), lambda i, j: (i, 0))],
        out_shape=[jax.ShapeDtypeStruct((M, PROJ_MAIN), F32),
                   jax.ShapeDtypeStruct((M, LANES), F32)],
        scratch_shapes=[pltpu.VMEM((tm, K), BF16)],
        compiler_params=_params(("parallel", "arbitrary")),
        name="in_proj",
    )(x, w_stack, wg)


def _outproj_ln_body(y_ref, x_ref, w_ref, g_ref, b_ref, o_ref, *, alpha):
    mix = jnp.dot(y_ref[...].astype(BF16), w_ref[...], preferred_element_type=F32)
    o_ref[...] = _layer_norm(alpha * x_ref[...] + mix, g_ref[...], b_ref[...])


def _outproj_ln(y, x, w_stack, layer, g, b, tm, alpha):
    M, K = y.shape
    D = w_stack.shape[2]
    row = lambda i: (i, 0)
    fixed = lambda i: (0, 0)
    return pl.pallas_call(
        functools.partial(_outproj_ln_body, alpha=alpha),
        grid=(M // tm,),
        in_specs=[pl.BlockSpec((tm, K), row), pl.BlockSpec((tm, D), row),
                  pl.BlockSpec((None, K, D), lambda i: (layer, 0, 0), pipeline_mode=pl.Buffered(1)),
                  pl.BlockSpec((1, D), fixed), pl.BlockSpec((1, D), fixed)],
        out_specs=pl.BlockSpec((tm, D), row),
        out_shape=jax.ShapeDtypeStruct((M, D), F32),
        compiler_params=_params(("parallel",)),
        name="out_proj_ln",
    )(y, x, w_stack, g, b)


def _mlp_ln_body(h_ref, wu_ref, wd_ref, g_ref, b_ref, o_ref, hb_ref, *, alpha):
    f = pl.program_id(1)

    @pl.when(f == 0)
    def _():
        hb_ref[...] = h_ref[...].astype(BF16)
        o_ref[...] = jnp.zeros_like(o_ref)

    up = jnp.dot(hb_ref[...], wu_ref[...], preferred_element_type=F32)
    act = jnp.square(jnp.maximum(up, 0.0)).astype(BF16)
    o_ref[...] += jnp.dot(act, wd_ref[...], preferred_element_type=F32)

    @pl.when(f == pl.num_programs(1) - 1)
    def _():
        o_ref[...] = _layer_norm(alpha * h_ref[...] + o_ref[...], g_ref[...], b_ref[...])


def _mlp_ln(h, wu_stack, wd_stack, layer, g, b, tm, tf, alpha):
    M, D = h.shape
    FF = wu_stack.shape[2]
    return pl.pallas_call(
        functools.partial(_mlp_ln_body, alpha=alpha),
        grid=(M // tm, FF // tf),
        in_specs=[pl.BlockSpec((tm, D), lambda i, f: (i, 0), pipeline_mode=pl.Buffered(1)),
                  pl.BlockSpec((None, D, tf), lambda i, f: (layer, 0, f)),
                  pl.BlockSpec((None, tf, D), lambda i, f: (layer, f, 0)),
                  pl.BlockSpec((1, D), lambda i, f: (0, 0)),
                  pl.BlockSpec((1, D), lambda i, f: (0, 0))],
        out_specs=pl.BlockSpec((tm, D), lambda i, f: (i, 0), pipeline_mode=pl.Buffered(1)),
        out_shape=jax.ShapeDtypeStruct((M, D), F32),
        scratch_shapes=[pltpu.VMEM((tm, D), BF16)],
        compiler_params=_params(("parallel", "arbitrary")),
        name="mlp_ln",
    )(h, wu_stack, wd_stack, g, b)


def _ple_body(h_ref, p_ref, wg_ref, wp_ref, o_ref):
    h = h_ref[...]
    gate = _sigmoid(jnp.dot(h.astype(BF16), wg_ref[...], preferred_element_type=F32))
    emb = jnp.dot(p_ref[...].astype(BF16), wp_ref[...], preferred_element_type=F32)
    o_ref[...] = h + gate * emb


def _ple(h, p_stack, layer, wg_stack, wp_stack, tm):
    M, D = h.shape
    P = p_stack.shape[2]
    row = lambda i: (i, 0)
    return pl.pallas_call(
        _ple_body,
        grid=(M // tm,),
        in_specs=[pl.BlockSpec((tm, D), row),
                  pl.BlockSpec((None, tm, P), lambda i: (layer, i, 0)),
                  pl.BlockSpec((None, D, D), lambda i: (layer, 0, 0), pipeline_mode=pl.Buffered(1)),
                  pl.BlockSpec((None, P, D), lambda i: (layer, 0, 0), pipeline_mode=pl.Buffered(1))],
        out_specs=pl.BlockSpec((tm, D), row),
        out_shape=jax.ShapeDtypeStruct((M, D), F32),
        compiler_params=_params(("parallel",)),
        name="ple_gate",
    )(h, p_stack, wg_stack, wp_stack)


def _mixer_e_body(proj_ref, gate_ref, conv0_ref, s0_ref, h0_ref, wconv_ref, bconv_ref, alog_ref,
                  dtb_ref, nw_ref, wri_ref, br_ref, bi_ref, lam_ref,
                  y_ref, convn_ref, s_ref, hl_ref,
                  cbuf, a_scr, bx_scr, hseq, *, L, t_valid, conv_win):
    c = pl.program_id(1)
    heads = range(HEADS)

    @pl.when(c == 0)
    def _():
        cbuf[0:CONV_TAIL, :] = conv0_ref[0]
        s_ref[0] = s0_ref[0]
        hl_ref[0] = h0_ref[0]

    tril, strict, eye = _masks(L)
    tril_f = jnp.where(tril, 1.0, 0.0).astype(F32)
    row_ok = lax.broadcasted_iota(jnp.int32, (L, 1), 0) < t_valid
    s_old = [s_ref[0, h] for h in heads]
    h_old = hl_ref[0]

    cbuf[CONV_TAIL:CONV_TAIL + L, :] = proj_ref[:, 0:CONV_CH]
    full = cbuf[...]
    acc = bconv_ref[...] + full[CONV_TAIL:, :] * wconv_ref[CONV_W - 1:CONV_W, :]
    for j in range(1, CONV_W):
        shifted = pltpu.roll(full, j, axis=0)[CONV_TAIL:, :]
        acc = acc + shifted * wconv_ref[CONV_W - 1 - j:CONV_W - j, :]

    gp = gate_ref[...]
    g_all = -jnp.exp(alog_ref[...]) * _softplus(gp + dtb_ref[...])
    beta_all = _sigmoid(gp)
    if t_valid < L:
        g_all = jnp.where(row_ok, g_all, 0.0)
        beta_all = jnp.where(row_ok, beta_all, 0.0)
    gc_all = _dot_f32(tril_f, g_all)

    q_l, k_l, kb_l, gcol_l, egc_l, y_l = [], [], [], [], [], []
    qa_l = [_silu(acc[:, h * DK:(h + 1) * DK]) for h in heads]
    ka_l = [_silu(acc[:, HEADS * DK + h * DK:HEADS * DK + (h + 1) * DK]) for h in heads]
    qn_l = [lax.rsqrt(jnp.sum(qa_l[h] * qa_l[h], axis=-1, keepdims=True) + L2_EPS) * (DK ** -0.5)
            for h in heads]
    kn_l = [lax.rsqrt(jnp.sum(ka_l[h] * ka_l[h], axis=-1, keepdims=True) + L2_EPS) for h in heads]
    for h in heads:
        q = qa_l[h] * qn_l[h]
        k = ka_l[h] * kn_l[h]
        v = _silu(acc[:, 2 * HEADS * DK + h * A_DV:2 * HEADS * DK + (h + 1) * A_DV])
        gcol = gc_all[:, h:h + 1]
        bcol = beta_all[:, HEADS + h:HEADS + h + 1]
        egc = jnp.exp(gcol)
        kb = k * bcol
        q_l.append(q)
        k_l.append(k)
        kb_l.append(kb)
        gcol_l.append(gcol)
        egc_l.append(egc)
        y_l.append(jnp.concatenate([v * bcol, kb * egc], axis=1))
    kq_k = [_mm_nt(jnp.concatenate([kb_l[h], q_l[h]], axis=0), k_l[h]) for h in heads]
    p_l, qk_l = [], []
    for h in heads:
        grow = _col_to_row(gcol_l[h], eye)
        decay = jnp.where(tril, jnp.exp(jnp.where(tril, gcol_l[h] - grow, 0.0)), 0.0)
        p_l.append(jnp.where(strict, -(kq_k[h][0:L] * decay), 0.0))
        qk_l.append(kq_k[h][L:2 * L] * decay)

    n_levels = max(1, math.ceil(math.log2(t_valid)))
    r_l = p_l
    if n_levels >= 2:
        p_l = [_mm(p_l[h], p_l[h]) for h in heads]
        for _ in range(n_levels - 2):
            out = [_mm(jnp.concatenate([p_l[h], r_l[h]], axis=0), p_l[h]) for h in heads]
            r_l = [r_l[h] + p_l[h] + out[h][L:2 * L] for h in heads]
            p_l = [out[h][0:L] for h in heads]
        out = [_mm(r_l[h], p_l[h]) for h in heads]
        r_l = [r_l[h] + p_l[h] + out[h] for h in heads]
    y_l = [y_l[h] + _mm(r_l[h], y_l[h]) for h in heads]

    wq_s = [_mm(jnp.concatenate([y_l[h][:, A_DV:A_DV + DK], q_l[h] * egc_l[h]], axis=0), s_old[h])
            for h in heads]
    v_new = [y_l[h][:, 0:A_DV] - wq_s[h][0:L] for h in heads]
    o_intra = [_mm(qk_l[h], v_new[h]) for h in heads]
    s_upd = []
    for h in heads:
        g_last = gcol_l[h][L - 1:L, :]
        s_upd.append(s_old[h] * jnp.exp(g_last)
                     + _mm_tn(k_l[h] * jnp.exp(g_last - gcol_l[h]), v_new[h]))

    bw = LRU_W // LRU_BLOCKS
    xr = acc[:, 3 * HEADS * DK:3 * HEADS * DK + LRU_W]
    sp = _softplus(-lam_ref[...])
    for n in range(LRU_BLOCKS):
        sl = slice(n * bw, (n + 1) * bw)
        xb = xr[:, sl]
        ri_pre = _mm(xb, wri_ref[n])
        r_pre = ri_pre[:, 0:bw] + br_ref[:, sl]
        i_pre = ri_pre[:, bw:2 * bw] + bi_ref[:, sl]
        log_a = -LRU_C * _sigmoid(r_pre) * sp[:, sl]
        a = jnp.exp(log_a)
        a_scr[:, sl] = a
        bx_scr[:, sl] = jnp.sqrt(-jnp.tanh(log_a) * (a * a + 1.0)) * _sigmoid(i_pre) * xb

    hc = h_old
    for t in range(L):
        if t < t_valid:
            hc = a_scr[t:t + 1, :] * hc + bx_scr[t:t + 1, :]
        hseq[t:t + 1, :] = hc

    o_l = [wq_s[h][L:2 * L] + o_intra[h] for h in heads]
    rms_l = [lax.rsqrt(jnp.mean(o_l[h] * o_l[h], axis=-1, keepdims=True) + RMS_EPS) for h in heads]
    for h in heads:
        z = proj_ref[:, CONV_CH + h * A_DV:CONV_CH + (h + 1) * A_DV]
        y_ref[:, h * A_DV:(h + 1) * A_DV] = o_l[h] * rms_l[h] * nw_ref[...] * _silu(z)
    for h in heads:
        s_ref[0, h] = s_upd[h]
    hl_ref[0] = hc
    gate = proj_ref[:, CONV_CH + HEADS * A_DV:CONV_CH + HEADS * A_DV + LRU_W]
    y_ref[:, HEADS * A_DV:HEADS * A_DV + LRU_W] = hseq[...] * _gelu_tanh(gate)

    @pl.when(c == pl.num_programs(1) - 1)
    def _():
        convn_ref[0] = cbuf[conv_win:conv_win + SUBLANES, :]

    cbuf[0:CONV_TAIL, :] = cbuf[L:L + CONV_TAIL, :]


def _mixer_e(proj, gates, conv0, s0, h0, wconv, bconv, alog, dtb, nw, wri, br, bi, lam,
             *, B, nc, L, t_valid):
    conv_row = CONV_TAIL - (CONV_W - 1) + t_valid
    conv_win = (conv_row // SUBLANES) * SUBLANES
    assert conv_row - conv_win + 3 <= SUBLANES
    mix = HEADS * A_DV + LRU_W
    chunk = lambda b, c: (b * nc + c, 0)
    fix2 = lambda b, c: (0, 0)
    fix3 = lambda b, c: (0, 0, 0)
    outs = pl.pallas_call(
        functools.partial(_mixer_e_body, L=L, t_valid=t_valid, conv_win=conv_win),
        grid=(B, nc),
        in_specs=[pl.BlockSpec((L, PROJ_MAIN), chunk),
                  pl.BlockSpec((L, LANES), chunk),
                  pl.BlockSpec((1, SUBLANES, CONV_CH), lambda b, c: (b, 0, 0)),
                  pl.BlockSpec((1, HEADS, DK, A_DV), lambda b, c: (b, 0, 0, 0)),
                  pl.BlockSpec((1, 1, LRU_W), lambda b, c: (b, 0, 0)),
                  pl.BlockSpec((CONV_W, CONV_CH), fix2),
                  pl.BlockSpec((1, CONV_CH), fix2),
                  pl.BlockSpec((1, LANES), fix2),
                  pl.BlockSpec((1, LANES), fix2),
                  pl.BlockSpec((1, A_DV), fix2),
                  pl.BlockSpec((LRU_BLOCKS, LRU_W // LRU_BLOCKS, 2 * LRU_W // LRU_BLOCKS), fix3),
                  pl.BlockSpec((1, LRU_W), fix2),
                  pl.BlockSpec((1, LRU_W), fix2),
                  pl.BlockSpec((1, LRU_W), fix2)],
        out_specs=[pl.BlockSpec((L, mix), chunk),
                   pl.BlockSpec((1, SUBLANES, CONV_CH), lambda b, c: (b, 0, 0)),
                   pl.BlockSpec((1, HEADS, DK, A_DV), lambda b, c: (b, 0, 0, 0)),
                   pl.BlockSpec((1, 1, LRU_W), lambda b, c: (b, 0, 0))],
        out_shape=[jax.ShapeDtypeStruct((B * nc * L, mix), F32),
                   jax.ShapeDtypeStruct((B, SUBLANES, CONV_CH), F32),
                   jax.ShapeDtypeStruct((B, HEADS, DK, A_DV), F32),
                   jax.ShapeDtypeStruct((B, 1, LRU_W), F32)],
        scratch_shapes=[pltpu.VMEM((CONV_TAIL + L, CONV_CH), F32),
                        pltpu.VMEM((L, LRU_W), F32),
                        pltpu.VMEM((L, LRU_W), F32),
                        pltpu.VMEM((L, LRU_W), F32)],
        compiler_params=_params(("parallel", "arbitrary")),
        name="delta_lru_mixer",
    )(proj, gates, conv0, s0, h0, wconv, bconv, alog, dtb, nw, wri, br, bi, lam)
    return outs, conv_row - conv_win


def _mixer_o_body(proj_ref, gate_ref, c0_ref, n0_ref, m0_ref, big_ref, bfg_ref, nw_ref,
                  y_ref, c_ref, n_ref, m_ref, *, L, t_valid):
    ci = pl.program_id(1)
    heads = range(HEADS)

    @pl.when(ci == 0)
    def _():
        c_ref[0] = c0_ref[0]
        n_ref[0] = n0_ref[0]
        m_ref[0] = m0_ref[0]

    tril, _, eye = _masks(L)
    tril_f = jnp.where(tril, 1.0, 0.0).astype(F32)
    row_ok = lax.broadcasted_iota(jnp.int32, (L, 1), 0) < t_valid
    c_old = [c_ref[0, h] for h in heads]
    n_all = n_ref[0]
    m_all = m_ref[0]

    gp = gate_ref[...]
    ig_all = gp + big_ref[...]
    logf_all = -_softplus(-(gp + bfg_ref[...]))
    if t_valid < L:
        ig_all = jnp.where(row_ok, ig_all, NEG_BIG)
        logf_all = jnp.where(row_ok, logf_all, 0.0)
    bcum_all = _dot_f32(tril_f, logf_all)

    q_l = [proj_ref[:, h * DK:(h + 1) * DK] for h in heads]
    k_l = [proj_ref[:, HEADS * DK + h * DK:HEADS * DK + (h + 1) * DK] * (DK ** -0.5) for h in heads]
    v_l = [proj_ref[:, 2 * HEADS * DK + h * C_DV:2 * HEADS * DK + (h + 1) * C_DV] for h in heads]
    q_ck = [_mm_nt(q_l[h], jnp.concatenate([c_old[h], k_l[h]], axis=0)) for h in heads]

    s_l, e_l, mt_l, sc_l, ws_l, mnew_l = [], [], [], [], [], []
    for h in heads:
        bcol = bcum_all[:, HEADS + h:HEADS + h + 1]
        icol = ig_all[:, h:h + 1]
        brow = _col_to_row(bcol, eye)
        irow = _col_to_row(icol, eye)
        m_old = m_all[:, h:h + 1]
        d_intra = jnp.where(tril, bcol - brow + irow, NEG_BIG)
        b_last = bcol[L - 1:L, :]
        ge_row = b_last - brow + irow
        ge_col = b_last - bcol + icol
        inter = bcol + m_old
        m_t = jnp.maximum(inter, jnp.max(d_intra, axis=-1, keepdims=True))
        m_new = jnp.maximum(b_last + m_old, jnp.max(ge_row, axis=-1, keepdims=True))
        e_l.append(jnp.exp(inter - m_t))
        mt_l.append(m_t)
        s_l.append(q_ck[h][:, C_DV:C_DV + L] * jnp.exp(d_intra - m_t))
        sc_l.append(jnp.exp(b_last + m_old - m_new))
        ws_l.append(jnp.exp(ge_col - m_new))
        mnew_l.append(m_new)

    sv = [_mm(s_l[h], v_l[h]) for h in heads]
    kv = [_mm_tn(v_l[h] * ws_l[h], k_l[h]) for h in heads]

    den_l = [e_l[h] * jnp.sum(q_l[h] * n_all[h:h + 1, :], axis=-1, keepdims=True)
             + jnp.sum(s_l[h], axis=-1, keepdims=True) for h in heads]
    inv_l = [1.0 / jnp.maximum(jnp.abs(den_l[h]), jnp.exp(-mt_l[h])) for h in heads]
    hh_l = [(e_l[h] * q_ck[h][:, 0:C_DV] + sv[h]) * inv_l[h] for h in heads]
    rms_l = [lax.rsqrt(jnp.mean(hh_l[h] * hh_l[h], axis=-1, keepdims=True) + RMS_EPS) for h in heads]
    for h in heads:
        o_pre = proj_ref[:, 2 * HEADS * DK + HEADS * C_DV + h * C_DV:
                         2 * HEADS * DK + HEADS * C_DV + (h + 1) * C_DV]
        y_ref[:, h * C_DV:(h + 1) * C_DV] = (hh_l[h] * rms_l[h] * nw_ref[:, h * C_DV:(h + 1) * C_DV]
                                             * _sigmoid(o_pre))
    lane = lax.broadcasted_iota(jnp.int32, (1, LANES), 1)
    m_new_all = m_all
    n_rows = []
    for h in heads:
        c_ref[0, h] = sc_l[h] * c_old[h] + kv[h]
        n_rows.append(sc_l[h] * n_all[h:h + 1, :] + jnp.sum(k_l[h] * ws_l[h], axis=0, keepdims=True))
        m_new_all = jnp.where(lane == h, mnew_l[h], m_new_all)
    n_ref[0] = jnp.concatenate(n_rows, axis=0)
    m_ref[0] = m_new_all


def _mixer_o(proj, gates, c0, n0, m0, big, bfg, nw, *, B, nc, L, t_valid):
    cv = HEADS * C_DV
    chunk = lambda b, c: (b * nc + c, 0)
    fix2 = lambda b, c: (0, 0)
    return pl.pallas_call(
        functools.partial(_mixer_o_body, L=L, t_valid=t_valid),
        grid=(B, nc),
        in_specs=[pl.BlockSpec((L, PROJ_MAIN), chunk),
                  pl.BlockSpec((L, LANES), chunk),
                  pl.BlockSpec((1, HEADS, C_DV, DK), lambda b, c: (b, 0, 0, 0)),
                  pl.BlockSpec((1, HEADS, DK), lambda b, c: (b, 0, 0)),
                  pl.BlockSpec((1, 1, LANES), lambda b, c: (b, 0, 0)),
                  pl.BlockSpec((1, LANES), fix2),
                  pl.BlockSpec((1, LANES), fix2),
                  pl.BlockSpec((1, cv), fix2)],
        out_specs=[pl.BlockSpec((L, cv), chunk),
                   pl.BlockSpec((1, HEADS, C_DV, DK), lambda b, c: (b, 0, 0, 0)),
                   pl.BlockSpec((1, HEADS, DK), lambda b, c: (b, 0, 0)),
                   pl.BlockSpec((1, 1, LANES), lambda b, c: (b, 0, 0))],
        out_shape=[jax.ShapeDtypeStruct((B * nc * L, cv), F32),
                   jax.ShapeDtypeStruct((B, HEADS, C_DV, DK), F32),
                   jax.ShapeDtypeStruct((B, HEADS, DK), F32),
                   jax.ShapeDtypeStruct((B, 1, LANES), F32)],
        compiler_params=_params(("parallel", "arbitrary")),
        name="mlstm_mixer",
    )(proj, gates, c0, n0, m0, big, bfg, nw)


def _pad_lanes(v, offset=0):
    return jnp.pad(v.astype(F32), (offset, LANES - offset - v.shape[0]))[None, :]


def _gate_weight(w):
    wg = w[:, PROJ_MAIN:]
    return jnp.pad(wg, ((0, 0), (0, LANES - wg.shape[1]))).astype(BF16)


def _row_tile(m, cap):
    t = cap
    while t >= SUBLANES:
        if m % t == 0:
            return t
        t //= 2
    raise ValueError(f"row count {m} is not a multiple of {SUBLANES}")


def _chunking(T):
    L = math.gcd(T, CHUNK)
    nc = T // L
    Lp = -(-L // SUBLANES) * SUBLANES
    assert nc == 1 or Lp == L
    return nc, Lp, L


def _to_chunks2d(a, nc, Lp, L):
    B, T, N = a.shape
    if Lp != L:
        a = jnp.pad(a, ((0, 0), (0, Lp - L), (0, 0)))
    return a.reshape(B * nc * Lp, N)


def _from_chunks2d(a, B, nc, Lp, L):
    a = a.reshape(B, nc * Lp, a.shape[-1])
    return a[:, :L] if Lp != L else a


def _trunk(x, p, conv, delta, lru, mc, mn, mm, W, depth, alpha):
    B, T, D = x.shape
    M = B * T
    tm = _row_tile(M, TM_SMALL)
    tm_big = _row_tile(M, TM_BIG)
    nc, Lp, L = _chunking(T)
    x2 = x.reshape(M, D)
    p2 = p.reshape(depth, M, -1)
    chunks = lambda a: _to_chunks2d(a.reshape(B, T, -1), nc, Lp, L)
    conv_o, delta_o, lru_o, mc_o, mn_o, mm_o = [], [], [], [], [], []
    for layer in range(depth):
        j = layer // 2
        if layer % 2 == 0:
            proj, gates = _proj(x2, W['w_in_e'], j, W['wg_e'][j], tm_big, PROJ_TN)
            conv0 = jnp.pad(conv[j], ((0, 0), (CONV_TAIL - (CONV_W - 1), 0), (0, 0)))
            (y, convn, s_new, h_new), conv_off = _mixer_e(
                chunks(proj), chunks(gates), conv0, delta[j], lru[j][:, None, :],
                W['w_conv_e'][j], W['b_conv_e'][j][None, :],
                _pad_lanes(W['a_log_e'][j]), _pad_lanes(W['dt_bias_e'][j]),
                W['delta_norm_e'][j][None, :], W['lru_wri_e'][j], W['lru_br_e'][j][None, :],
                W['lru_bi_e'][j][None, :], W['lru_lambda_e'][j][None, :],
                B=B, nc=nc, L=Lp, t_valid=L)
            conv_o.append(convn[:, conv_off:conv_off + CONV_W - 1])
            delta_o.append(s_new)
            lru_o.append(h_new[:, 0])
            w_out = W['w_out_e']
        else:
            proj, gates = _proj(x2, W['w_in_o'], j, W['wg_o'][j], tm_big, PROJ_TN)
            m0 = jnp.pad(mm[j], ((0, 0), (0, LANES - HEADS)))[:, None, :]
            y, c_new, n_new, m_new = _mixer_o(
                chunks(proj), chunks(gates), mc[j], mn[j], m0,
                _pad_lanes(W['b_ig_o'][j]), _pad_lanes(W['b_fg_o'][j], HEADS),
                W['mlstm_norm_o'][j][None, :], B=B, nc=nc, L=Lp, t_valid=L)
            mc_o.append(c_new)
            mn_o.append(n_new)
            mm_o.append(m_new[:, 0, :HEADS])
            w_out = W['w_out_o']
        y = _from_chunks2d(y, B, nc, Lp, L).reshape(M, -1)
        h = _outproj_ln(y, x2, w_out, j, W['ln1_g'][layer][None, :], W['ln1_b'][layer][None, :],
                        tm, alpha)
        h = _mlp_ln(h, W['w_up'], W['w_down'], layer, W['ln2_g'][layer][None, :],
                    W['ln2_b'][layer][None, :], tm_big, MLP_TF, alpha)
        x2 = _ple(h, p2, layer, W['w_ple_gate'], W['w_ple'], tm)
    return (x2.reshape(B, T, D), jnp.stack(conv_o), jnp.stack(delta_o), jnp.stack(lru_o),
            jnp.stack(mc_o), jnp.stack(mn_o), jnp.stack(mm_o))


def kernel(x_prompt, x_sample, p_prompt, p_sample, state_conv, state_delta, state_lru,
           state_mlstm_c, state_mlstm_n, state_mlstm_m, w_in_e, w_conv_e, b_conv_e, a_log_e,
           dt_bias_e, delta_norm_e, lru_wr_e, lru_br_e, lru_wi_e, lru_bi_e, lru_lambda_e, w_out_e,
           w_in_o, b_ig_o, b_fg_o, mlstm_norm_o, w_out_o, ln1_g, ln1_b, ln2_g, ln2_b, w_up, w_down,
           w_ple, w_ple_gate):
    depth = ln1_g.shape[0]
    n_even, n_odd = w_in_e.shape[0], w_in_o.shape[0]
    alpha = (2 * depth) ** 0.25
    W = dict(
        w_in_e=w_in_e[:, :, :PROJ_MAIN].astype(BF16), w_in_o=w_in_o[:, :, :PROJ_MAIN].astype(BF16),
        wg_e=[_gate_weight(w_in_e[j]) for j in range(n_even)],
        wg_o=[_gate_weight(w_in_o[j]) for j in range(n_odd)],
        w_out_e=w_out_e.astype(BF16), w_out_o=w_out_o.astype(BF16),
        w_up=w_up.astype(BF16), w_down=w_down.astype(BF16),
        w_ple=w_ple.astype(BF16), w_ple_gate=w_ple_gate.astype(BF16),
        w_conv_e=w_conv_e, b_conv_e=b_conv_e, a_log_e=a_log_e, dt_bias_e=dt_bias_e,
        delta_norm_e=delta_norm_e, lru_br_e=lru_br_e,
        lru_wri_e=jnp.concatenate([lru_wr_e, lru_wi_e], axis=-1).astype(BF16),
        lru_bi_e=lru_bi_e, lru_lambda_e=lru_lambda_e, b_ig_o=b_ig_o, b_fg_o=b_fg_o,
        mlstm_norm_o=mlstm_norm_o, ln1_g=ln1_g, ln1_b=ln1_b, ln2_g=ln2_g, ln2_b=ln2_b)
    bp = x_prompt.shape[0]
    zeros = lambda a: jnp.zeros((a.shape[0], bp) + a.shape[2:], F32)
    out_p = _trunk(x_prompt, p_prompt, zeros(state_conv), zeros(state_delta), zeros(state_lru),
                   zeros(state_mlstm_c), zeros(state_mlstm_n), zeros(state_mlstm_m), W, depth, alpha)
    out_s = _trunk(x_sample, p_sample, state_conv, state_delta, state_lru,
                   state_mlstm_c, state_mlstm_n, state_mlstm_m, W, depth, alpha)
    return (out_p[0], out_s[0]) + tuple(out_p[1:]) + tuple(out_s[1:])
```

```python
import functools
import math

import jax
import jax.numpy as jnp
from jax import lax
from jax.experimental import pallas as pl
from jax.experimental.pallas import tpu as pltpu

F32 = jnp.float32
BF16 = jnp.bfloat16
HI = lax.Precision.HIGHEST

LANES = 128
SUBLANES = 8
VMEM_LIMIT = 56 * 1024 * 1024

HEADS = 8
DK = 128
A_DV = 128
C_DV = 256
LRU_W = 1024
LRU_BLOCKS = 8
LRU_C = 8.0
CONV_W = 4
CONV_CH = 4096
CHUNK = 64
LN_EPS = 1e-5
RMS_EPS = 1e-6
L2_EPS = 1e-6
PROJ_MAIN = 6144
PROJ_TN = 512
MLP_TF = 512
TM_PROJ = 2048
TM_MLP = 1024
TM_SMALL = 512
CONV_TAIL = SUBLANES
NEG_BIG = -1e30


def _dot_f32(a, b):
    return lax.dot_general(a, b, (((1,), (0,)), ((), ())), precision=HI,
                           preferred_element_type=F32)


def _mm(a, b):
    return lax.dot_general(a.astype(BF16), b.astype(BF16), (((1,), (0,)), ((), ())),
                           preferred_element_type=F32)


def _mm_nt(a, b):
    return lax.dot_general(a.astype(BF16), b.astype(BF16), (((1,), (1,)), ((), ())),
                           preferred_element_type=F32)


def _mm_tn(a, b):
    return lax.dot_general(a.astype(BF16), b.astype(BF16), (((0,), (0,)), ((), ())),
                           preferred_element_type=F32)


def _sigmoid(x):
    return 1.0 / (1.0 + jnp.exp(-x))


def _silu(x):
    return x * _sigmoid(x)


def _softplus(x):
    return jnp.maximum(x, 0.0) + jnp.log1p(jnp.exp(-jnp.abs(x)))


def _gelu_tanh(x):
    return 0.5 * x * (1.0 + jnp.tanh(math.sqrt(2.0 / math.pi) * (x + 0.044715 * (x * x * x))))


def _layer_norm(x, g, b):
    mu = jnp.mean(x, axis=-1, keepdims=True)
    xc = x - mu
    var = jnp.mean(xc * xc, axis=-1, keepdims=True)
    return xc * lax.rsqrt(var + LN_EPS) * g + b


def _col_to_row(col, eye):
    return jnp.sum(jnp.where(eye, col, 0.0), axis=0, keepdims=True)


def _masks(L):
    ri = lax.broadcasted_iota(jnp.int32, (L, L), 0)
    ci = lax.broadcasted_iota(jnp.int32, (L, L), 1)
    return ri >= ci, ri > ci, ri == ci


def _params(sem):
    return pltpu.CompilerParams(dimension_semantics=sem, vmem_limit_bytes=VMEM_LIMIT)


def _proj_body(x_ref, wt_ref, wgt_ref, o_ref, og_ref, xb_ref):
    @pl.when(pl.program_id(1) == 0)
    def _():
        xb = x_ref[...].astype(BF16)
        xb_ref[...] = xb
        og_ref[...] = lax.dot_general(xb, wgt_ref[...].astype(BF16), (((1,), (1,)), ((), ())),
                                      preferred_element_type=F32)

    o_ref[...] = lax.dot_general(xb_ref[...], wt_ref[...].astype(BF16), (((1,), (1,)), ((), ())),
                                 preferred_element_type=F32)


def _proj(x, wt_stack, layer, wgt, tm, tn):
    M, K = x.shape
    return pl.pallas_call(
        _proj_body,
        grid=(M // tm, PROJ_MAIN // tn),
        in_specs=[pl.BlockSpec((tm, K), lambda i, j: (i, 0), pipeline_mode=pl.Buffered(1)),
                  pl.BlockSpec((None, tn, K), lambda i, j: (layer, j, 0)),
                  pl.BlockSpec((LANES, K), lambda i, j: (0, 0))],
        out_specs=[pl.BlockSpec((tm, tn), lambda i, j: (i, j)),
                   pl.BlockSpec((tm, LANES), lambda i, j: (i, 0))],
        out_shape=[jax.ShapeDtypeStruct((M, PROJ_MAIN), F32),
                   jax.ShapeDtypeStruct((M, LANES), F32)],
        scratch_shapes=[pltpu.VMEM((tm, K), BF16)],
        compiler_params=_params(("parallel", "arbitrary")),
        name="in_proj",
    )(x, wt_stack, wgt)


def _outproj_ln_body(y_ref, x_ref, w_ref, g_ref, b_ref, o_ref, *, alpha):
    mix = jnp.dot(y_ref[...].astype(BF16), w_ref[...], preferred_element_type=F32)
    o_ref[...] = _layer_norm(alpha * x_ref[...] + mix, g_ref[...], b_ref[...])


def _outproj_ln(y, x, w_stack, layer, g, b, tm, alpha):
    M, K = y.shape
    D = w_stack.shape[2]
    row = lambda i: (i, 0)
    fixed = lambda i: (0, 0)
    return pl.pallas_call(
        functools.partial(_outproj_ln_body, alpha=alpha),
        grid=(M // tm,),
        in_specs=[pl.BlockSpec((tm, K), row), pl.BlockSpec((tm, D), row),
                  pl.BlockSpec((None, K, D), lambda i: (layer, 0, 0), pipeline_mode=pl.Buffered(1)),
                  pl.BlockSpec((1, D), fixed), pl.BlockSpec((1, D), fixed)],
        out_specs=pl.BlockSpec((tm, D), row),
        out_shape=jax.ShapeDtypeStruct((M, D), F32),
        compiler_params=_params(("parallel",)),
        name="out_proj_ln",
    )(y, x, w_stack, g, b)


def _mlp_ln_body(h_ref, wu_ref, wd_ref, g_ref, b_ref, o_ref, hb_ref, *, alpha):
    f = pl.program_id(1)

    @pl.when(f == 0)
    def _():
        hb_ref[...] = h_ref[...].astype(BF16)
        o_ref[...] = jnp.zeros_like(o_ref)

    up = jnp.dot(hb_ref[...], wu_ref[...].astype(BF16), preferred_element_type=F32)
    act = jnp.square(jnp.maximum(up, 0.0)).astype(BF16)
    o_ref[...] += jnp.dot(act, wd_ref[...].astype(BF16), preferred_element_type=F32)

    @pl.when(f == pl.num_programs(1) - 1)
    def _():
        o_ref[...] = _layer_norm(alpha * h_ref[...] + o_ref[...], g_ref[...], b_ref[...])


def _mlp_ln(h, wu_stack, wd_stack, layer, g, b, tm, tf, alpha):
    M, D = h.shape
    FF = wu_stack.shape[2]
    return pl.pallas_call(
        functools.partial(_mlp_ln_body, alpha=alpha),
        grid=(M // tm, FF // tf),
        in_specs=[pl.BlockSpec((tm, D), lambda i, f: (i, 0), pipeline_mode=pl.Buffered(1)),
                  pl.BlockSpec((None, D, tf), lambda i, f: (layer, 0, f)),
                  pl.BlockSpec((None, tf, D), lambda i, f: (layer, f, 0)),
                  pl.BlockSpec((1, D), lambda i, f: (0, 0)),
                  pl.BlockSpec((1, D), lambda i, f: (0, 0))],
        out_specs=pl.BlockSpec((tm, D), lambda i, f: (i, 0), pipeline_mode=pl.Buffered(1)),
        out_shape=jax.ShapeDtypeStruct((M, D), F32),
        scratch_shapes=[pltpu.VMEM((tm, D), BF16)],
        compiler_params=_params(("parallel", "arbitrary")),
        name="mlp_ln",
    )(h, wu_stack, wd_stack, g, b)


def _ple_body(h_ref, p_ref, wg_ref, wp_ref, o_ref):
    h = h_ref[...]
    gate = _sigmoid(jnp.dot(h.astype(BF16), wg_ref[...], preferred_element_type=F32))
    emb = jnp.dot(p_ref[...].astype(BF16), wp_ref[...], preferred_element_type=F32)
    o_ref[...] = h + gate * emb


def _ple(h, p_stack, layer, wg_stack, wp_stack, tm):
    M, D = h.shape
    P = p_stack.shape[2]
    row = lambda i: (i, 0)
    return pl.pallas_call(
        _ple_body,
        grid=(M // tm,),
        in_specs=[pl.BlockSpec((tm, D), row),
                  pl.BlockSpec((None, tm, P), lambda i: (layer, i, 0)),
                  pl.BlockSpec((None, D, D), lambda i: (layer, 0, 0), pipeline_mode=pl.Buffered(1)),
                  pl.BlockSpec((None, P, D), lambda i: (layer, 0, 0), pipeline_mode=pl.Buffered(1))],
        out_specs=pl.BlockSpec((tm, D), row),
        out_shape=jax.ShapeDtypeStruct((M, D), F32),
        compiler_params=_params(("parallel",)),
        name="ple_gate",
    )(h, p_stack, wg_stack, wp_stack)


def _mixer_e_body(proj_ref, gate_ref, conv0_ref, s0_ref, h0_ref, wconv_ref, bconv_ref, alog_ref,
                  dtb_ref, nw_ref, wri_ref, br_ref, bi_ref, lam_ref,
                  y_ref, convn_ref, s_ref, hl_ref,
                  cbuf, a_scr, bx_scr, hseq, *, L, t_valid, conv_win):
    c = pl.program_id(1)
    heads = range(HEADS)

    @pl.when(c == 0)
    def _():
        cbuf[0:CONV_TAIL, :] = conv0_ref[0]
        s_ref[0] = s0_ref[0]
        hl_ref[0] = h0_ref[0]

    tril, strict, eye = _masks(L)
    tril_f = jnp.where(tril, 1.0, 0.0).astype(F32)
    row_ok = lax.broadcasted_iota(jnp.int32, (L, 1), 0) < t_valid
    s_old = [s_ref[0, h] for h in heads]
    h_old = hl_ref[0]

    cbuf[CONV_TAIL:CONV_TAIL + L, :] = proj_ref[:, 0:CONV_CH]
    full = cbuf[...]
    prev = pltpu.roll(full, 1, axis=0)
    pair_new = full[CONV_TAIL:, :] * wconv_ref[3:4, :] + prev[CONV_TAIL:, :] * wconv_ref[2:3, :]
    pair_old = full * wconv_ref[1:2, :] + prev * wconv_ref[0:1, :]
    acc = bconv_ref[...] + pair_new + pltpu.roll(pair_old, 2, axis=0)[CONV_TAIL:, :]

    gp = gate_ref[...]
    g_all = -jnp.exp(alog_ref[...]) * _softplus(gp + dtb_ref[...])
    beta_all = _sigmoid(gp)
    if t_valid < L:
        g_all = jnp.where(row_ok, g_all, 0.0)
        beta_all = jnp.where(row_ok, beta_all, 0.0)
    gc_all = _dot_f32(tril_f, g_all)

    q_l, k_l, kb_l, gcol_l, egc_l, y_l = [], [], [], [], [], []
    qa_l = [_silu(acc[:, h * DK:(h + 1) * DK]) for h in heads]
    ka_l = [_silu(acc[:, HEADS * DK + h * DK:HEADS * DK + (h + 1) * DK]) for h in heads]
    qn_l = [lax.rsqrt(jnp.sum(qa_l[h] * qa_l[h], axis=-1, keepdims=True) + L2_EPS) * (DK ** -0.5)
            for h in heads]
    kn_l = [lax.rsqrt(jnp.sum(ka_l[h] * ka_l[h], axis=-1, keepdims=True) + L2_EPS) for h in heads]
    for h in heads:
        q = qa_l[h] * qn_l[h]
        k = ka_l[h] * kn_l[h]
        v = _silu(acc[:, 2 * HEADS * DK + h * A_DV:2 * HEADS * DK + (h + 1) * A_DV])
        gcol = gc_all[:, h:h + 1]
        bcol = beta_all[:, HEADS + h:HEADS + h + 1]
        egc = jnp.exp(gcol)
        kb = k * bcol
        q_l.append(q)
        k_l.append(k)
        kb_l.append(kb)
        gcol_l.append(gcol)
        egc_l.append(egc)
        y_l.append(jnp.concatenate([v * bcol, kb * egc], axis=1))
    kq_k = [_mm_nt(jnp.concatenate([kb_l[h], q_l[h]], axis=0), k_l[h]) for h in heads]
    p_l, qk_l = [], []
    for h in heads:
        grow = _col_to_row(gcol_l[h], eye)
        decay = jnp.where(tril, jnp.exp(jnp.where(tril, gcol_l[h] - grow, 0.0)), 0.0)
        p_l.append(jnp.where(strict, -(kq_k[h][0:L] * decay), 0.0))
        qk_l.append(kq_k[h][L:2 * L] * decay)

    n_levels = max(1, math.ceil(math.log2(t_valid)))
    r_l = p_l
    if n_levels >= 2:
        p_l = [_mm(p_l[h], p_l[h]) for h in heads]
        for _ in range(n_levels - 2):
            out = [_mm(jnp.concatenate([p_l[h], r_l[h]], axis=0), p_l[h]) for h in heads]
            r_l = [r_l[h] + p_l[h] + out[h][L:2 * L] for h in heads]
            p_l = [out[h][0:L] for h in heads]
        out = [_mm(r_l[h], p_l[h]) for h in heads]
        r_l = [r_l[h] + p_l[h] + out[h] for h in heads]
    y_l = [y_l[h] + _mm(r_l[h], y_l[h]) for h in heads]

    wq_s = [_mm(jnp.concatenate([y_l[h][:, A_DV:A_DV + DK], q_l[h] * egc_l[h]], axis=0), s_old[h])
            for h in heads]
    v_new = [y_l[h][:, 0:A_DV] - wq_s[h][0:L] for h in heads]
    o_intra = [_mm(qk_l[h], v_new[h]) for h in heads]
    s_upd = []
    for h in heads:
        g_last = gcol_l[h][L - 1:L, :]
        s_upd.append(s_old[h] * jnp.exp(g_last)
                     + _mm_tn(k_l[h] * jnp.exp(g_last - gcol_l[h]), v_new[h]))

    bw = LRU_W // LRU_BLOCKS
    xr = acc[:, 3 * HEADS * DK:3 * HEADS * DK + LRU_W]
    sp = _softplus(-lam_ref[...])
    for n in range(LRU_BLOCKS):
        sl = slice(n * bw, (n + 1) * bw)
        xb = xr[:, sl]
        ri_pre = _mm(xb, wri_ref[n])
        r_pre = ri_pre[:, 0:bw] + br_ref[:, sl]
        i_pre = ri_pre[:, bw:2 * bw] + bi_ref[:, sl]
        log_a = -LRU_C * _sigmoid(r_pre) * sp[:, sl]
        a = jnp.exp(log_a)
        a_scr[:, sl] = a
        bx_scr[:, sl] = jnp.sqrt(-jnp.tanh(log_a) * (a * a + 1.0)) * _sigmoid(i_pre) * xb

    hc = h_old
    for t in range(L):
        if t < t_valid:
            hc = a_scr[t:t + 1, :] * hc + bx_scr[t:t + 1, :]
        hseq[t:t + 1, :] = hc

    o_l = [wq_s[h][L:2 * L] + o_intra[h] for h in heads]
    rms_l = [lax.rsqrt(jnp.mean(o_l[h] * o_l[h], axis=-1, keepdims=True) + RMS_EPS) for h in heads]
    for h in heads:
        z = proj_ref[:, CONV_CH + h * A_DV:CONV_CH + (h + 1) * A_DV]
        y_ref[:, h * A_DV:(h + 1) * A_DV] = o_l[h] * rms_l[h] * nw_ref[...] * _silu(z)
    for h in heads:
        s_ref[0, h] = s_upd[h]
    hl_ref[0] = hc
    gate = proj_ref[:, CONV_CH + HEADS * A_DV:CONV_CH + HEADS * A_DV + LRU_W]
    y_ref[:, HEADS * A_DV:HEADS * A_DV + LRU_W] = hseq[...] * _gelu_tanh(gate)

    @pl.when(c == pl.num_programs(1) - 1)
    def _():
        convn_ref[0] = cbuf[conv_win:conv_win + SUBLANES, :]

    cbuf[0:CONV_TAIL, :] = cbuf[L:L + CONV_TAIL, :]


def _mixer_e(proj, gates, conv0, s0, h0, wconv, bconv, alog, dtb, nw, wri, br, bi, lam,
             *, B, nc, L, t_valid):
    conv_row = CONV_TAIL - (CONV_W - 1) + t_valid
    conv_win = (conv_row // SUBLANES) * SUBLANES
    assert conv_row - conv_win + 3 <= SUBLANES
    mix = HEADS * A_DV + LRU_W
    chunk = lambda b, c: (b * nc + c, 0)
    fix2 = lambda b, c: (0, 0)
    fix3 = lambda b, c: (0, 0, 0)
    outs = pl.pallas_call(
        functools.partial(_mixer_e_body, L=L, t_valid=t_valid, conv_win=conv_win),
        grid=(B, nc),
        in_specs=[pl.BlockSpec((L, PROJ_MAIN), chunk),
                  pl.BlockSpec((L, LANES), chunk),
                  pl.BlockSpec((1, SUBLANES, CONV_CH), lambda b, c: (b, 0, 0)),
                  pl.BlockSpec((1, HEADS, DK, A_DV), lambda b, c: (b, 0, 0, 0)),
                  pl.BlockSpec((1, 1, LRU_W), lambda b, c: (b, 0, 0)),
                  pl.BlockSpec((CONV_W, CONV_CH), fix2),
                  pl.BlockSpec((1, CONV_CH), fix2),
                  pl.BlockSpec((1, LANES), fix2),
                  pl.BlockSpec((1, LANES), fix2),
                  pl.BlockSpec((1, A_DV), fix2),
                  pl.BlockSpec((LRU_BLOCKS, LRU_W // LRU_BLOCKS, 2 * LRU_W // LRU_BLOCKS), fix3),
                  pl.BlockSpec((1, LRU_W), fix2),
                  pl.BlockSpec((1, LRU_W), fix2),
                  pl.BlockSpec((1, LRU_W), fix2)],
        out_specs=[pl.BlockSpec((L, mix), chunk),
                   pl.BlockSpec((1, SUBLANES, CONV_CH), lambda b, c: (b, 0, 0)),
                   pl.BlockSpec((1, HEADS, DK, A_DV), lambda b, c: (b, 0, 0, 0)),
                   pl.BlockSpec((1, 1, LRU_W), lambda b, c: (b, 0, 0))],
        out_shape=[jax.ShapeDtypeStruct((B * nc * L, mix), F32),
                   jax.ShapeDtypeStruct((B, SUBLANES, CONV_CH), F32),
                   jax.ShapeDtypeStruct((B, HEADS, DK, A_DV), F32),
                   jax.ShapeDtypeStruct((B, 1, LRU_W), F32)],
        scratch_shapes=[pltpu.VMEM((CONV_TAIL + L, CONV_CH), F32),
                        pltpu.VMEM((L, LRU_W), F32),
                        pltpu.VMEM((L, LRU_W), F32),
                        pltpu.VMEM((L, LRU_W), F32)],
        compiler_params=_params(("parallel", "arbitrary")),
        name="delta_lru_mixer",
    )(proj, gates, conv0, s0, h0, wconv, bconv, alog, dtb, nw, wri, br, bi, lam)
    return outs, conv_row - conv_win


def _mixer_o_body(proj_ref, gate_ref, c0_ref, n0_ref, m0_ref, big_ref, bfg_ref, nw_ref,
                  y_ref, c_ref, n_ref, m_ref, *, L, t_valid):
    ci = pl.program_id(1)
    heads = range(HEADS)

    @pl.when(ci == 0)
    def _():
        c_ref[0] = c0_ref[0]
        n_ref[0] = n0_ref[0]
        m_ref[0] = m0_ref[0]

    tril, _, eye = _masks(L)
    tril_f = jnp.where(tril, 1.0, 0.0).astype(F32)
    row_ok = lax.broadcasted_iota(jnp.int32, (L, 1), 0) < t_valid
    c_old = [c_ref[0, h] for h in heads]
    n_all = n_ref[0]
    m_all = m_ref[0]

    gp = gate_ref[...]
    ig_all = gp + big_ref[...]
    logf_all = -_softplus(-(gp + bfg_ref[...]))
    if t_valid < L:
        ig_all = jnp.where(row_ok, ig_all, NEG_BIG)
        logf_all = jnp.where(row_ok, logf_all, 0.0)
    bcum_all = _dot_f32(tril_f, logf_all)

    q_l = [proj_ref[:, h * DK:(h + 1) * DK] for h in heads]
    k_l = [proj_ref[:, HEADS * DK + h * DK:HEADS * DK + (h + 1) * DK] * (DK ** -0.5) for h in heads]
    v_l = [proj_ref[:, 2 * HEADS * DK + h * C_DV:2 * HEADS * DK + (h + 1) * C_DV] for h in heads]
    q_ck = [_mm_nt(q_l[h], jnp.concatenate([c_old[h], k_l[h]], axis=0)) for h in heads]

    s_l, e_l, mt_l, sc_l, ws_l, mnew_l = [], [], [], [], [], []
    for h in heads:
        bcol = bcum_all[:, HEADS + h:HEADS + h + 1]
        icol = ig_all[:, h:h + 1]
        brow = _col_to_row(bcol, eye)
        irow = _col_to_row(icol, eye)
        m_old = m_all[:, h:h + 1]
        d_intra = jnp.where(tril, bcol - brow + irow, NEG_BIG)
        b_last = bcol[L - 1:L, :]
        ge_row = b_last - brow + irow
        ge_col = b_last - bcol + icol
        inter = bcol + m_old
        m_t = jnp.maximum(inter, jnp.max(d_intra, axis=-1, keepdims=True))
        m_new = jnp.maximum(b_last + m_old, jnp.max(ge_row, axis=-1, keepdims=True))
        e_l.append(jnp.exp(inter - m_t))
        mt_l.append(m_t)
        s_l.append(q_ck[h][:, C_DV:C_DV + L] * jnp.exp(d_intra - m_t))
        sc_l.append(jnp.exp(b_last + m_old - m_new))
        ws_l.append(jnp.exp(ge_col - m_new))
        mnew_l.append(m_new)

    sv = [_mm(s_l[h], v_l[h]) for h in heads]
    kv = [_mm_tn(v_l[h] * ws_l[h], k_l[h]) for h in heads]

    den_l = [e_l[h] * jnp.sum(q_l[h] * n_all[h:h + 1, :], axis=-1, keepdims=True)
             + jnp.sum(s_l[h], axis=-1, keepdims=True) for h in heads]
    inv_l = [1.0 / jnp.maximum(jnp.abs(den_l[h]), jnp.exp(-mt_l[h])) for h in heads]
    hh_l = [(e_l[h] * q_ck[h][:, 0:C_DV] + sv[h]) * inv_l[h] for h in heads]
    rms_l = [lax.rsqrt(jnp.mean(hh_l[h] * hh_l[h], axis=-1, keepdims=True) + RMS_EPS) for h in heads]
    for h in heads:
        o_pre = proj_ref[:, 2 * HEADS * DK + HEADS * C_DV + h * C_DV:
                         2 * HEADS * DK + HEADS * C_DV + (h + 1) * C_DV]
        y_ref[:, h * C_DV:(h + 1) * C_DV] = (hh_l[h] * rms_l[h] * nw_ref[:, h * C_DV:(h + 1) * C_DV]
                                             * _sigmoid(o_pre))
    lane = lax.broadcasted_iota(jnp.int32, (1, LANES), 1)
    m_new_all = m_all
    n_rows = []
    for h in heads:
        c_ref[0, h] = sc_l[h] * c_old[h] + kv[h]
        n_rows.append(sc_l[h] * n_all[h:h + 1, :] + jnp.sum(k_l[h] * ws_l[h], axis=0, keepdims=True))
        m_new_all = jnp.where(lane == h, mnew_l[h], m_new_all)
    n_ref[0] = jnp.concatenate(n_rows, axis=0)
    m_ref[0] = m_new_all


def _mixer_o(proj, gates, c0, n0, m0, big, bfg, nw, *, B, nc, L, t_valid):
    cv = HEADS * C_DV
    chunk = lambda b, c: (b * nc + c, 0)
    fix2 = lambda b, c: (0, 0)
    return pl.pallas_call(
        functools.partial(_mixer_o_body, L=L, t_valid=t_valid),
        grid=(B, nc),
        in_specs=[pl.BlockSpec((L, PROJ_MAIN), chunk),
                  pl.BlockSpec((L, LANES), chunk),
                  pl.BlockSpec((1, HEADS, C_DV, DK), lambda b, c: (b, 0, 0, 0)),
                  pl.BlockSpec((1, HEADS, DK), lambda b, c: (b, 0, 0)),
                  pl.BlockSpec((1, 1, LANES), lambda b, c: (b, 0, 0)),
                  pl.BlockSpec((1, LANES), fix2),
                  pl.BlockSpec((1, LANES), fix2),
                  pl.BlockSpec((1, cv), fix2)],
        out_specs=[pl.BlockSpec((L, cv), chunk),
                   pl.BlockSpec((1, HEADS, C_DV, DK), lambda b, c: (b, 0, 0, 0)),
                   pl.BlockSpec((1, HEADS, DK), lambda b, c: (b, 0, 0)),
                   pl.BlockSpec((1, 1, LANES), lambda b, c: (b, 0, 0))],
        out_shape=[jax.ShapeDtypeStruct((B * nc * L, cv), F32),
                   jax.ShapeDtypeStruct((B, HEADS, C_DV, DK), F32),
                   jax.ShapeDtypeStruct((B, HEADS, DK), F32),
                   jax.ShapeDtypeStruct((B, 1, LANES), F32)],
        compiler_params=_params(("parallel", "arbitrary")),
        name="mlstm_mixer",
    )(proj, gates, c0, n0, m0, big, bfg, nw)


def _pad_lanes(v, offset=0):
    return jnp.pad(v.astype(F32), (offset, LANES - offset - v.shape[0]))[None, :]


def _gate_weight_t(wt):
    wg = wt[PROJ_MAIN:, :]
    return jnp.pad(wg, ((0, LANES - wg.shape[0]), (0, 0)))


def _row_tile(m, cap):
    t = cap
    while t >= SUBLANES:
        if m % t == 0:
            return t
        t //= 2
    raise ValueError(f"row count {m} is not a multiple of {SUBLANES}")


def _chunking(T):
    L = math.gcd(T, CHUNK)
    nc = T // L
    Lp = -(-L // SUBLANES) * SUBLANES
    assert nc == 1 or Lp == L
    return nc, Lp, L


def _to_chunks2d(a, nc, Lp, L):
    B, T, N = a.shape
    if Lp != L:
        a = jnp.pad(a, ((0, 0), (0, Lp - L), (0, 0)))
    return a.reshape(B * nc * Lp, N)


def _from_chunks2d(a, B, nc, Lp, L):
    a = a.reshape(B, nc * Lp, a.shape[-1])
    return a[:, :L] if Lp != L else a


def _trunk(x, p, conv, delta, lru, mc, mn, mm, W, depth, alpha):
    B, T, D = x.shape
    M = B * T
    tm = _row_tile(M, TM_SMALL)
    tm_proj = _row_tile(M, TM_PROJ)
    tm_mlp = _row_tile(M, TM_MLP)
    nc, Lp, L = _chunking(T)
    x2 = x.reshape(M, D)
    p2 = p.reshape(depth, M, -1)
    chunks = lambda a: _to_chunks2d(a.reshape(B, T, -1), nc, Lp, L)
    conv_o, delta_o, lru_o, mc_o, mn_o, mm_o = [], [], [], [], [], []
    for layer in range(depth):
        j = layer // 2
        if layer % 2 == 0:
            proj, gates = _proj(x2, W['w_in_e'], j, W['wg_e'][j], tm_proj, PROJ_TN)
            conv0 = jnp.pad(conv[j], ((0, 0), (CONV_TAIL - (CONV_W - 1), 0), (0, 0)))
            (y, convn, s_new, h_new), conv_off = _mixer_e(
                chunks(proj), chunks(gates), conv0, delta[j], lru[j][:, None, :],
                W['w_conv_e'][j], W['b_conv_e'][j][None, :],
                _pad_lanes(W['a_log_e'][j]), _pad_lanes(W['dt_bias_e'][j]),
                W['delta_norm_e'][j][None, :], W['lru_wri_e'][j], W['lru_br_e'][j][None, :],
                W['lru_bi_e'][j][None, :], W['lru_lambda_e'][j][None, :],
                B=B, nc=nc, L=Lp, t_valid=L)
            conv_o.append(convn[:, conv_off:conv_off + CONV_W - 1])
            delta_o.append(s_new)
            lru_o.append(h_new[:, 0])
            w_out = W['w_out_e']
        else:
            proj, gates = _proj(x2, W['w_in_o'], j, W['wg_o'][j], tm_proj, PROJ_TN)
            m0 = jnp.pad(mm[j], ((0, 0), (0, LANES - HEADS)))[:, None, :]
            y, c_new, n_new, m_new = _mixer_o(
                chunks(proj), chunks(gates), mc[j], mn[j], m0,
                _pad_lanes(W['b_ig_o'][j]), _pad_lanes(W['b_fg_o'][j], HEADS),
                W['mlstm_norm_o'][j][None, :], B=B, nc=nc, L=Lp, t_valid=L)
            mc_o.append(c_new)
            mn_o.append(n_new)
            mm_o.append(m_new[:, 0, :HEADS])
            w_out = W['w_out_o']
        y = _from_chunks2d(y, B, nc, Lp, L).reshape(M, -1)
        h = _outproj_ln(y, x2, w_out, j, W['ln1_g'][layer][None, :], W['ln1_b'][layer][None, :],
                        tm, alpha)
        h = _mlp_ln(h, W['w_up'], W['w_down'], layer, W['ln2_g'][layer][None, :],
                    W['ln2_b'][layer][None, :], tm_mlp, MLP_TF, alpha)
        x2 = _ple(h, p2, layer, W['w_ple_gate'], W['w_ple'], tm)
    return (x2.reshape(B, T, D), jnp.stack(conv_o), jnp.stack(delta_o), jnp.stack(lru_o),
            jnp.stack(mc_o), jnp.stack(mn_o), jnp.stack(mm_o))


def kernel(x_prompt, x_sample, p_prompt, p_sample, state_conv, state_delta, state_lru,
           state_mlstm_c, state_mlstm_n, state_mlstm_m, w_in_e, w_conv_e, b_conv_e, a_log_e,
           dt_bias_e, delta_norm_e, lru_wr_e, lru_br_e, lru_wi_e, lru_bi_e, lru_lambda_e, w_out_e,
           w_in_o, b_ig_o, b_fg_o, mlstm_norm_o, w_out_o, ln1_g, ln1_b, ln2_g, ln2_b, w_up, w_down,
           w_ple, w_ple_gate):
    depth = ln1_g.shape[0]
    n_even, n_odd = w_in_e.shape[0], w_in_o.shape[0]
    alpha = (2 * depth) ** 0.25
    wt_e = jnp.swapaxes(w_in_e, 1, 2)
    wt_o = jnp.swapaxes(w_in_o, 1, 2)
    W = dict(
        w_in_e=wt_e, w_in_o=wt_o,
        wg_e=[_gate_weight_t(wt_e[j]) for j in range(n_even)],
        wg_o=[_gate_weight_t(wt_o[j]) for j in range(n_odd)],
        w_out_e=w_out_e.astype(BF16), w_out_o=w_out_o.astype(BF16),
        w_up=w_up, w_down=w_down,
        w_ple=w_ple.astype(BF16), w_ple_gate=w_ple_gate.astype(BF16),
        w_conv_e=w_conv_e, b_conv_e=b_conv_e, a_log_e=a_log_e, dt_bias_e=dt_bias_e,
        delta_norm_e=delta_norm_e, lru_br_e=lru_br_e,
        lru_wri_e=jnp.concatenate([lru_wr_e, lru_wi_e], axis=-1).astype(BF16),
        lru_bi_e=lru_bi_e, lru_lambda_e=lru_lambda_e, b_ig_o=b_ig_o, b_fg_o=b_fg_o,
        mlstm_norm_o=mlstm_norm_o, ln1_g=ln1_g, ln1_b=ln1_b, ln2_g=ln2_g, ln2_b=ln2_b)
    bp = x_prompt.shape[0]
    zeros = lambda a: jnp.zeros((a.shape[0], bp) + a.shape[2:], F32)
    out_p = _trunk(x_prompt, p_prompt, zeros(state_conv), zeros(state_delta), zeros(state_lru),
                   zeros(state_mlstm_c), zeros(state_mlstm_n), zeros(state_mlstm_m), W, depth, alpha)
    out_s = _trunk(x_sample, p_sample, state_conv, state_delta, state_lru,
                   state_mlstm_c, state_mlstm_n, state_mlstm_m, W, depth, alpha)
    return (out_p[0], out_s[0]) + tuple(out_p[1:]) + tuple(out_s[1:])
```

```python
import functools
import math

import jax
import jax.numpy as jnp
from jax import lax
from jax.experimental import pallas as pl
from jax.experimental.pallas import tpu as pltpu

F32 = jnp.float32
BF16 = jnp.bfloat16
HI = lax.Precision.HIGHEST

LANES = 128
SUBLANES = 8
VMEM_LIMIT = 56 * 1024 * 1024

HEADS = 8
DK = 128
A_DV = 128
C_DV = 256
LRU_W = 1024
LRU_BLOCKS = 8
LRU_C = 8.0
CONV_W = 4
CONV_CH = 4096
CHUNK = 64
LN_EPS = 1e-5
RMS_EPS = 1e-6
L2_EPS = 1e-6
PROJ_MAIN = 6144
PROJ_TN = 512
MLP_TF = 512
TM_PROJ = 2048
TM_MLP = 1024
TM_SMALL = 512
CONV_TAIL = SUBLANES
NEG_BIG = -1e30


def _dot_f32(a, b):
    return lax.dot_general(a, b, (((1,), (0,)), ((), ())), precision=HI,
                           preferred_element_type=F32)


def _mm(a, b):
    return lax.dot_general(a.astype(BF16), b.astype(BF16), (((1,), (0,)), ((), ())),
                           preferred_element_type=F32)


def _mm_nt(a, b):
    return lax.dot_general(a.astype(BF16), b.astype(BF16), (((1,), (1,)), ((), ())),
                           preferred_element_type=F32)


def _mm_tn(a, b):
    return lax.dot_general(a.astype(BF16), b.astype(BF16), (((0,), (0,)), ((), ())),
                           preferred_element_type=F32)


def _sigmoid(x):
    return 1.0 / (1.0 + jnp.exp(-x))


def _silu(x):
    return x * _sigmoid(x)


def _softplus(x):
    return jnp.maximum(x, 0.0) + jnp.log1p(jnp.exp(-jnp.abs(x)))


def _gelu_tanh(x):
    return 0.5 * x * (1.0 + jnp.tanh(math.sqrt(2.0 / math.pi) * (x + 0.044715 * (x * x * x))))


def _layer_norm(x, g, b):
    mu = jnp.mean(x, axis=-1, keepdims=True)
    xc = x - mu
    var = jnp.mean(xc * xc, axis=-1, keepdims=True)
    return xc * lax.rsqrt(var + LN_EPS) * g + b


def _col_to_row(col, eye):
    return jnp.sum(jnp.where(eye, col, 0.0), axis=0, keepdims=True)


def _seg_masks(L, seg):
    ri = lax.broadcasted_iota(jnp.int32, (L, L), 0)
    ci = lax.broadcasted_iota(jnp.int32, (L, L), 1)
    same = (ri // seg) == (ci // seg)
    return same, same & (ri >= ci), same & (ri > ci), ri == ci


def _params(sem):
    return pltpu.CompilerParams(dimension_semantics=sem, vmem_limit_bytes=VMEM_LIMIT)


def _proj_body(x_ref, wt_ref, wgt_ref, o_ref, og_ref, xb_ref):
    @pl.when(pl.program_id(1) == 0)
    def _():
        xb = x_ref[...].astype(BF16)
        xb_ref[...] = xb
        og_ref[...] = lax.dot_general(xb, wgt_ref[...].astype(BF16), (((1,), (1,)), ((), ())),
                                      preferred_element_type=F32)

    o_ref[...] = lax.dot_general(xb_ref[...], wt_ref[...].astype(BF16), (((1,), (1,)), ((), ())),
                                 preferred_element_type=F32)


def _proj(x, wt_stack, layer, wgt, tm, tn):
    M, K = x.shape
    return pl.pallas_call(
        _proj_body,
        grid=(M // tm, PROJ_MAIN // tn),
        in_specs=[pl.BlockSpec((tm, K), lambda i, j: (i, 0), pipeline_mode=pl.Buffered(1)),
                  pl.BlockSpec((None, tn, K), lambda i, j: (layer, j, 0)),
                  pl.BlockSpec((LANES, K), lambda i, j: (0, 0))],
        out_specs=[pl.BlockSpec((tm, tn), lambda i, j: (i, j)),
                   pl.BlockSpec((tm, LANES), lambda i, j: (i, 0))],
        out_shape=[jax.ShapeDtypeStruct((M, PROJ_MAIN), F32),
                   jax.ShapeDtypeStruct((M, LANES), F32)],
        scratch_shapes=[pltpu.VMEM((tm, K), BF16)],
        compiler_params=_params(("parallel", "arbitrary")),
        name="in_proj",
    )(x, wt_stack, wgt)


def _outproj_ln_body(y_ref, x_ref, w_ref, g_ref, b_ref, o_ref, *, alpha):
    mix = jnp.dot(y_ref[...].astype(BF16), w_ref[...], preferred_element_type=F32)
    o_ref[...] = _layer_norm(alpha * x_ref[...] + mix, g_ref[...], b_ref[...])


def _outproj_ln(y, x, w_stack, layer, g, b, tm, alpha):
    M, K = y.shape
    D = w_stack.shape[2]
    row = lambda i: (i, 0)
    fixed = lambda i: (0, 0)
    return pl.pallas_call(
        functools.partial(_outproj_ln_body, alpha=alpha),
        grid=(M // tm,),
        in_specs=[pl.BlockSpec((tm, K), row), pl.BlockSpec((tm, D), row),
                  pl.BlockSpec((None, K, D), lambda i: (layer, 0, 0), pipeline_mode=pl.Buffered(1)),
                  pl.BlockSpec((1, D), fixed), pl.BlockSpec((1, D), fixed)],
        out_specs=pl.BlockSpec((tm, D), row),
        out_shape=jax.ShapeDtypeStruct((M, D), F32),
        compiler_params=_params(("parallel",)),
        name="out_proj_ln",
    )(y, x, w_stack, g, b)


def _mlp_ln_body(h_ref, wu_ref, wd_ref, g_ref, b_ref, o_ref, hb_ref, *, alpha):
    f = pl.program_id(1)

    @pl.when(f == 0)
    def _():
        hb_ref[...] = h_ref[...].astype(BF16)
        o_ref[...] = jnp.zeros_like(o_ref)

    up = jnp.dot(hb_ref[...], wu_ref[...].astype(BF16), preferred_element_type=F32)
    act = jnp.square(jnp.maximum(up, 0.0)).astype(BF16)
    o_ref[...] += jnp.dot(act, wd_ref[...].astype(BF16), preferred_element_type=F32)

    @pl.when(f == pl.num_programs(1) - 1)
    def _():
        o_ref[...] = _layer_norm(alpha * h_ref[...] + o_ref[...], g_ref[...], b_ref[...])


def _mlp_ln(h, wu_stack, wd_stack, layer, g, b, tm, tf, alpha):
    M, D = h.shape
    FF = wu_stack.shape[2]
    return pl.pallas_call(
        functools.partial(_mlp_ln_body, alpha=alpha),
        grid=(M // tm, FF // tf),
        in_specs=[pl.BlockSpec((tm, D), lambda i, f: (i, 0), pipeline_mode=pl.Buffered(1)),
                  pl.BlockSpec((None, D, tf), lambda i, f: (layer, 0, f)),
                  pl.BlockSpec((None, tf, D), lambda i, f: (layer, f, 0)),
                  pl.BlockSpec((1, D), lambda i, f: (0, 0)),
                  pl.BlockSpec((1, D), lambda i, f: (0, 0))],
        out_specs=pl.BlockSpec((tm, D), lambda i, f: (i, 0), pipeline_mode=pl.Buffered(1)),
        out_shape=jax.ShapeDtypeStruct((M, D), F32),
        scratch_shapes=[pltpu.VMEM((tm, D), BF16)],
        compiler_params=_params(("parallel", "arbitrary")),
        name="mlp_ln",
    )(h, wu_stack, wd_stack, g, b)


def _ple_body(h_ref, p_ref, wg_ref, wp_ref, o_ref):
    h = h_ref[...]
    gate = _sigmoid(jnp.dot(h.astype(BF16), wg_ref[...], preferred_element_type=F32))
    emb = jnp.dot(p_ref[...].astype(BF16), wp_ref[...], preferred_element_type=F32)
    o_ref[...] = h + gate * emb


def _ple(h, p_stack, layer, wg_stack, wp_stack, tm):
    M, D = h.shape
    P = p_stack.shape[2]
    row = lambda i: (i, 0)
    return pl.pallas_call(
        _ple_body,
        grid=(M // tm,),
        in_specs=[pl.BlockSpec((tm, D), row),
                  pl.BlockSpec((None, tm, P), lambda i: (layer, i, 0)),
                  pl.BlockSpec((None, D, D), lambda i: (layer, 0, 0), pipeline_mode=pl.Buffered(1)),
                  pl.BlockSpec((None, P, D), lambda i: (layer, 0, 0), pipeline_mode=pl.Buffered(1))],
        out_specs=pl.BlockSpec((tm, D), row),
        out_shape=jax.ShapeDtypeStruct((M, D), F32),
        compiler_params=_params(("parallel",)),
        name="ple_gate",
    )(h, p_stack, wg_stack, wp_stack)


def _mixer_e_body(proj_ref, gate_ref, conv0_ref, s0_ref, h0_ref, wconv_ref, bconv_ref, alog_ref,
                  dtb_ref, nw_ref, wri_ref, br_ref, bi_ref, lam_ref,
                  y_ref, convn_ref, s_ref, hl_ref,
                  cbuf, a_scr, bx_scr, hseq, *, L, seg, t_valid, conv_win):
    c = pl.program_id(1)
    heads = range(HEADS)
    nseg = L // seg
    segs = range(nseg)
    stride = CONV_TAIL + seg

    @pl.when(c == 0)
    def _():
        for s in segs:
            cbuf[s * stride:s * stride + CONV_TAIL, :] = conv0_ref[s * CONV_TAIL:(s + 1) * CONV_TAIL, :]
        s_ref[...] = s0_ref[...]
        hl_ref[...] = h0_ref[...]

    same, tril, strict, eye = _seg_masks(L, seg)
    tril_f = jnp.where(tril, 1.0, 0.0).astype(F32)
    same_f = jnp.where(same, 1.0, 0.0).astype(F32)
    row_ok = (lax.broadcasted_iota(jnp.int32, (L, 1), 0) % seg) < t_valid
    s_old = [[s_ref[s, h] for h in heads] for s in segs]
    h_old = [hl_ref[s] for s in segs]

    for s in segs:
        cbuf[s * stride + CONV_TAIL:(s + 1) * stride, :] = proj_ref[s * seg:(s + 1) * seg, 0:CONV_CH]
    full = cbuf[...]
    prev = pltpu.roll(full, 1, axis=0)
    pair_new = full * wconv_ref[3:4, :] + prev * wconv_ref[2:3, :]
    pair_old = full * wconv_ref[1:2, :] + prev * wconv_ref[0:1, :]
    conv_full = bconv_ref[...] + pair_new + pltpu.roll(pair_old, 2, axis=0)
    acc = jnp.concatenate([conv_full[s * stride + CONV_TAIL:(s + 1) * stride, :] for s in segs], axis=0)

    gp = gate_ref[...]
    g_all = -jnp.exp(alog_ref[...]) * _softplus(gp + dtb_ref[...])
    beta_all = _sigmoid(gp)
    if t_valid < seg:
        g_all = jnp.where(row_ok, g_all, 0.0)
        beta_all = jnp.where(row_ok, beta_all, 0.0)
    gc_all = _dot_f32(tril_f, g_all)
    gl_all = _dot_f32(same_f, g_all)

    q_l, k_l, kb_l, gcol_l, egc_l, y_l = [], [], [], [], [], []
    qa_l = [_silu(acc[:, h * DK:(h + 1) * DK]) for h in heads]
    ka_l = [_silu(acc[:, HEADS * DK + h * DK:HEADS * DK + (h + 1) * DK]) for h in heads]
    qn_l = [lax.rsqrt(jnp.sum(qa_l[h] * qa_l[h], axis=-1, keepdims=True) + L2_EPS) * (DK ** -0.5)
            for h in heads]
    kn_l = [lax.rsqrt(jnp.sum(ka_l[h] * ka_l[h], axis=-1, keepdims=True) + L2_EPS) for h in heads]
    for h in heads:
        q = qa_l[h] * qn_l[h]
        k = ka_l[h] * kn_l[h]
        v = _silu(acc[:, 2 * HEADS * DK + h * A_DV:2 * HEADS * DK + (h + 1) * A_DV])
        gcol = gc_all[:, h:h + 1]
        bcol = beta_all[:, HEADS + h:HEADS + h + 1]
        egc = jnp.exp(gcol)
        kb = k * bcol
        q_l.append(q)
        k_l.append(k)
        kb_l.append(kb)
        gcol_l.append(gcol)
        egc_l.append(egc)
        y_l.append(jnp.concatenate([v * bcol, kb * egc], axis=1))
    kq_k = [_mm_nt(jnp.concatenate([kb_l[h], q_l[h]], axis=0), k_l[h]) for h in heads]
    p_l, qk_l = [], []
    for h in heads:
        grow = _col_to_row(gcol_l[h], eye)
        decay = jnp.where(tril, jnp.exp(jnp.where(tril, gcol_l[h] - grow, 0.0)), 0.0)
        p_l.append(jnp.where(strict, -(kq_k[h][0:L] * decay), 0.0))
        qk_l.append(kq_k[h][L:2 * L] * decay)

    n_levels = max(1, math.ceil(math.log2(t_valid)))
    r_l = p_l
    if n_levels >= 2:
        p_l = [_mm(p_l[h], p_l[h]) for h in heads]
        for _ in range(n_levels - 2):
            out = [_mm(jnp.concatenate([p_l[h], r_l[h]], axis=0), p_l[h]) for h in heads]
            r_l = [r_l[h] + p_l[h] + out[h][L:2 * L] for h in heads]
            p_l = [out[h][0:L] for h in heads]
        out = [_mm(r_l[h], p_l[h]) for h in heads]
        r_l = [r_l[h] + p_l[h] + out[h] for h in heads]
    y_l = [y_l[h] + _mm(r_l[h], y_l[h]) for h in heads]

    qe_l = [q_l[h] * egc_l[h] for h in heads]
    wq_s = [[_mm(jnp.concatenate([y_l[h][s * seg:(s + 1) * seg, A_DV:A_DV + DK],
                                  qe_l[h][s * seg:(s + 1) * seg]], axis=0), s_old[s][h])
             for h in heads] for s in segs]
    v_new = [y_l[h][:, 0:A_DV] - jnp.concatenate([wq_s[s][h][0:seg] for s in segs], axis=0)
             for h in heads]
    o_intra = [_mm(qk_l[h], v_new[h]) for h in heads]
    kd_l = [k_l[h] * jnp.exp(gl_all[:, h:h + 1] - gcol_l[h]) for h in heads]
    s_upd = [[s_old[s][h] * jnp.exp(gl_all[s * seg:s * seg + 1, h:h + 1])
              + _mm_tn(kd_l[h][s * seg:(s + 1) * seg], v_new[h][s * seg:(s + 1) * seg])
              for h in heads] for s in segs]

    bw = LRU_W // LRU_BLOCKS
    xr = acc[:, 3 * HEADS * DK:3 * HEADS * DK + LRU_W]
    sp = _softplus(-lam_ref[...])
    for n in range(LRU_BLOCKS):
        sl = slice(n * bw, (n + 1) * bw)
        xb = xr[:, sl]
        ri_pre = _mm(xb, wri_ref[n])
        r_pre = ri_pre[:, 0:bw] + br_ref[:, sl]
        i_pre = ri_pre[:, bw:2 * bw] + bi_ref[:, sl]
        log_a = -LRU_C * _sigmoid(r_pre) * sp[:, sl]
        a = jnp.exp(log_a)
        a_scr[:, sl] = a
        bx_scr[:, sl] = jnp.sqrt(-jnp.tanh(log_a) * (a * a + 1.0)) * _sigmoid(i_pre) * xb

    for s in segs:
        hc = h_old[s]
        for t in range(seg):
            row = s * seg + t
            if t < t_valid:
                hc = a_scr[row:row + 1, :] * hc + bx_scr[row:row + 1, :]
            hseq[row:row + 1, :] = hc
        hl_ref[s] = hc

    o_l = [jnp.concatenate([wq_s[s][h][seg:2 * seg] for s in segs], axis=0) + o_intra[h] for h in heads]
    rms_l = [lax.rsqrt(jnp.mean(o_l[h] * o_l[h], axis=-1, keepdims=True) + RMS_EPS) for h in heads]
    for h in heads:
        z = proj_ref[:, CONV_CH + h * A_DV:CONV_CH + (h + 1) * A_DV]
        y_ref[:, h * A_DV:(h + 1) * A_DV] = o_l[h] * rms_l[h] * nw_ref[...] * _silu(z)
    for s in segs:
        for h in heads:
            s_ref[s, h] = s_upd[s][h]
    gate = proj_ref[:, CONV_CH + HEADS * A_DV:CONV_CH + HEADS * A_DV + LRU_W]
    y_ref[:, HEADS * A_DV:HEADS * A_DV + LRU_W] = hseq[...] * _gelu_tanh(gate)

    @pl.when(c == pl.num_programs(1) - 1)
    def _():
        for s in segs:
            convn_ref[s * SUBLANES:(s + 1) * SUBLANES, :] = (
                cbuf[s * stride + conv_win:s * stride + conv_win + SUBLANES, :])

    for s in segs:
        cbuf[s * stride:s * stride + CONV_TAIL, :] = cbuf[s * stride + seg:s * stride + seg + CONV_TAIL, :]


def _mixer_e(proj, gates, conv0, s0, h0, wconv, bconv, alog, dtb, nw, wri, br, bi, lam,
             *, nblk, nc, L, seg, t_valid):
    nseg = L // seg
    nseq = nblk * nseg
    assert nc == 1 or nseg == 1
    conv_row = CONV_TAIL - (CONV_W - 1) + t_valid
    conv_win = (conv_row // SUBLANES) * SUBLANES
    assert conv_row - conv_win + CONV_W - 1 <= SUBLANES and t_valid >= CONV_W - 1
    mix = HEADS * A_DV + LRU_W
    chunk = lambda b, c: (b * nc + c, 0)
    fix2 = lambda b, c: (0, 0)
    fix3 = lambda b, c: (0, 0, 0)
    outs = pl.pallas_call(
        functools.partial(_mixer_e_body, L=L, seg=seg, t_valid=t_valid, conv_win=conv_win),
        grid=(nblk, nc),
        in_specs=[pl.BlockSpec((L, PROJ_MAIN), chunk),
                  pl.BlockSpec((L, LANES), chunk),
                  pl.BlockSpec((nseg * SUBLANES, CONV_CH), lambda b, c: (b, 0)),
                  pl.BlockSpec((nseg, HEADS, DK, A_DV), lambda b, c: (b, 0, 0, 0)),
                  pl.BlockSpec((nseg, 1, LRU_W), lambda b, c: (b, 0, 0)),
                  pl.BlockSpec((CONV_W, CONV_CH), fix2),
                  pl.BlockSpec((1, CONV_CH), fix2),
                  pl.BlockSpec((1, LANES), fix2),
                  pl.BlockSpec((1, LANES), fix2),
                  pl.BlockSpec((1, A_DV), fix2),
                  pl.BlockSpec((LRU_BLOCKS, LRU_W // LRU_BLOCKS, 2 * LRU_W // LRU_BLOCKS), fix3),
                  pl.BlockSpec((1, LRU_W), fix2),
                  pl.BlockSpec((1, LRU_W), fix2),
                  pl.BlockSpec((1, LRU_W), fix2)],
        out_specs=[pl.BlockSpec((L, mix), chunk),
                   pl.BlockSpec((nseg * SUBLANES, CONV_CH), lambda b, c: (b, 0)),
                   pl.BlockSpec((nseg, HEADS, DK, A_DV), lambda b, c: (b, 0, 0, 0)),
                   pl.BlockSpec((nseg, 1, LRU_W), lambda b, c: (b, 0, 0))],
        out_shape=[jax.ShapeDtypeStruct((nblk * nc * L, mix), F32),
                   jax.ShapeDtypeStruct((nseq * SUBLANES, CONV_CH), F32),
                   jax.ShapeDtypeStruct((nseq, HEADS, DK, A_DV), F32),
                   jax.ShapeDtypeStruct((nseq, 1, LRU_W), F32)],
        scratch_shapes=[pltpu.VMEM((nseg * (CONV_TAIL + seg), CONV_CH), F32),
                        pltpu.VMEM((L, LRU_W), F32),
                        pltpu.VMEM((L, LRU_W), F32),
                        pltpu.VMEM((L, LRU_W), F32)],
        compiler_params=_params(("parallel", "arbitrary")),
        name="delta_lru_mixer",
    )(proj, gates, conv0, s0, h0, wconv, bconv, alog, dtb, nw, wri, br, bi, lam)
    return outs, conv_row - conv_win


def _mixer_o_body(proj_ref, gate_ref, c0_ref, n0_ref, m0_ref, big_ref, bfg_ref, nw_ref,
                  y_ref, c_ref, n_ref, m_ref, *, L, seg, t_valid):
    ci = pl.program_id(1)
    heads = range(HEADS)
    nseg = L // seg
    segs = range(nseg)

    @pl.when(ci == 0)
    def _():
        c_ref[...] = c0_ref[...]
        n_ref[...] = n0_ref[...]
        m_ref[...] = m0_ref[...]

    same, tril, _, eye = _seg_masks(L, seg)
    tril_f = jnp.where(tril, 1.0, 0.0).astype(F32)
    same_f = jnp.where(same, 1.0, 0.0).astype(F32)
    row_id = lax.broadcasted_iota(jnp.int32, (L, 1), 0)
    row_ok = (row_id % seg) < t_valid
    c_old = [[c_ref[s, h] for h in heads] for s in segs]
    n_old = [n_ref[s] for s in segs]
    m_old = [m_ref[s] for s in segs]

    gp = gate_ref[...]
    ig_all = gp + big_ref[...]
    logf_all = -_softplus(-(gp + bfg_ref[...]))
    if t_valid < seg:
        ig_all = jnp.where(row_ok, ig_all, NEG_BIG)
        logf_all = jnp.where(row_ok, logf_all, 0.0)
    bcum_all = _dot_f32(tril_f, logf_all)
    blast_all = _dot_f32(same_f, logf_all)
    mrow_all = jnp.concatenate([jnp.broadcast_to(m_old[s], (seg, LANES)) for s in segs], axis=0)

    q_l = [proj_ref[:, h * DK:(h + 1) * DK] for h in heads]
    k_l = [proj_ref[:, HEADS * DK + h * DK:HEADS * DK + (h + 1) * DK] * (DK ** -0.5) for h in heads]
    v_l = [proj_ref[:, 2 * HEADS * DK + h * C_DV:2 * HEADS * DK + (h + 1) * C_DV] for h in heads]
    qk_l = [_mm_nt(q_l[h], k_l[h]) for h in heads]
    qc_l = [jnp.concatenate([_mm_nt(q_l[h][s * seg:(s + 1) * seg], c_old[s][h]) for s in segs], axis=0)
            for h in heads]

    s_l, e_l, mt_l, sc_l, ws_l, mnew_l = [], [], [], [], [], []
    for h in heads:
        bcol = bcum_all[:, HEADS + h:HEADS + h + 1]
        blast = blast_all[:, HEADS + h:HEADS + h + 1]
        icol = ig_all[:, h:h + 1]
        mcol = mrow_all[:, h:h + 1]
        brow = _col_to_row(bcol, eye)
        irow = _col_to_row(icol, eye)
        d_intra = jnp.where(tril, bcol - brow + irow, NEG_BIG)
        ge_mat = jnp.where(same, blast - brow + irow, NEG_BIG)
        ge_col = blast - bcol + icol
        inter = bcol + mcol
        m_t = jnp.maximum(inter, jnp.max(d_intra, axis=-1, keepdims=True))
        m_new = jnp.maximum(blast + mcol, jnp.max(ge_mat, axis=-1, keepdims=True))
        e_l.append(jnp.exp(inter - m_t))
        mt_l.append(m_t)
        s_l.append(qk_l[h] * jnp.exp(d_intra - m_t))
        sc_l.append(jnp.exp(blast + mcol - m_new))
        ws_l.append(jnp.exp(ge_col - m_new))
        mnew_l.append(m_new)

    sv = [_mm(s_l[h], v_l[h]) for h in heads]
    vw_l = [v_l[h] * ws_l[h] for h in heads]
    kw_l = [k_l[h] * ws_l[h] for h in heads]
    kv = [[_mm_tn(vw_l[h][s * seg:(s + 1) * seg], k_l[h][s * seg:(s + 1) * seg]) for h in heads]
          for s in segs]

    nrow_l = [jnp.concatenate([jnp.broadcast_to(n_old[s][h:h + 1, :], (seg, DK)) for s in segs], axis=0)
              for h in heads]
    den_l = [e_l[h] * jnp.sum(q_l[h] * nrow_l[h], axis=-1, keepdims=True)
             + jnp.sum(s_l[h], axis=-1, keepdims=True) for h in heads]
    inv_l = [1.0 / jnp.maximum(jnp.abs(den_l[h]), jnp.exp(-mt_l[h])) for h in heads]
    hh_l = [(e_l[h] * qc_l[h] + sv[h]) * inv_l[h] for h in heads]
    rms_l = [lax.rsqrt(jnp.mean(hh_l[h] * hh_l[h], axis=-1, keepdims=True) + RMS_EPS) for h in heads]
    for h in heads:
        o_pre = proj_ref[:, 2 * HEADS * DK + HEADS * C_DV + h * C_DV:
                         2 * HEADS * DK + HEADS * C_DV + (h + 1) * C_DV]
        y_ref[:, h * C_DV:(h + 1) * C_DV] = (hh_l[h] * rms_l[h] * nw_ref[:, h * C_DV:(h + 1) * C_DV]
                                             * _sigmoid(o_pre))
    lane = lax.broadcasted_iota(jnp.int32, (1, LANES), 1)
    for s in segs:
        r0 = s * seg
        m_row = m_old[s]
        n_rows = []
        for h in heads:
            sc = sc_l[h][r0:r0 + 1, :]
            c_ref[s, h] = sc * c_old[s][h] + kv[s][h]
            n_rows.append(sc * n_old[s][h:h + 1, :]
                          + jnp.sum(kw_l[h][r0:r0 + seg], axis=0, keepdims=True))
            m_row = jnp.where(lane == h, mnew_l[h][r0:r0 + 1, :], m_row)
        n_ref[s] = jnp.concatenate(n_rows, axis=0)
        m_ref[s] = m_row


def _mixer_o(proj, gates, c0, n0, m0, big, bfg, nw, *, nblk, nc, L, seg, t_valid):
    nseg = L // seg
    nseq = nblk * nseg
    assert nc == 1 or nseg == 1
    cv = HEADS * C_DV
    chunk = lambda b, c: (b * nc + c, 0)
    fix2 = lambda b, c: (0, 0)
    return pl.pallas_call(
        functools.partial(_mixer_o_body, L=L, seg=seg, t_valid=t_valid),
        grid=(nblk, nc),
        in_specs=[pl.BlockSpec((L, PROJ_MAIN), chunk),
                  pl.BlockSpec((L, LANES), chunk),
                  pl.BlockSpec((nseg, HEADS, C_DV, DK), lambda b, c: (b, 0, 0, 0)),
                  pl.BlockSpec((nseg, HEADS, DK), lambda b, c: (b, 0, 0)),
                  pl.BlockSpec((nseg, 1, LANES), lambda b, c: (b, 0, 0)),
                  pl.BlockSpec((1, LANES), fix2),
                  pl.BlockSpec((1, LANES), fix2),
                  pl.BlockSpec((1, cv), fix2)],
        out_specs=[pl.BlockSpec((L, cv), chunk),
                   pl.BlockSpec((nseg, HEADS, C_DV, DK), lambda b, c: (b, 0, 0, 0)),
                   pl.BlockSpec((nseg, HEADS, DK), lambda b, c: (b, 0, 0)),
                   pl.BlockSpec((nseg, 1, LANES), lambda b, c: (b, 0, 0))],
        out_shape=[jax.ShapeDtypeStruct((nblk * nc * L, cv), F32),
                   jax.ShapeDtypeStruct((nseq, HEADS, C_DV, DK), F32),
                   jax.ShapeDtypeStruct((nseq, HEADS, DK), F32),
                   jax.ShapeDtypeStruct((nseq, 1, LANES), F32)],
        compiler_params=_params(("parallel", "arbitrary")),
        name="mlstm_mixer",
    )(proj, gates, c0, n0, m0, big, bfg, nw)


def _pad_lanes(v, offset=0):
    return jnp.pad(v.astype(F32), (offset, LANES - offset - v.shape[0]))[None, :]


def _gate_weight_t(wt):
    wg = wt[PROJ_MAIN:, :]
    return jnp.pad(wg, ((0, LANES - wg.shape[0]), (0, 0)))


def _row_tile(m, cap):
    t = cap
    while t >= SUBLANES:
        if m % t == 0:
            return t
        t //= 2
    raise ValueError(f"row count {m} is not a multiple of {SUBLANES}")


def _blocking(B, T):
    t_valid = math.gcd(T, CHUNK)
    nc = T // t_valid
    seg = -(-t_valid // SUBLANES) * SUBLANES
    assert nc == 1 or seg == t_valid
    nseg = max(1, min(B, CHUNK // seg)) if nc == 1 else 1
    while B % nseg:
        nseg -= 1
    return t_valid, seg, nseg * seg, B // nseg, nc


def _trunk(x, p, conv, delta, lru, mc, mn, mm, W, depth, alpha):
    B, T, D = x.shape
    t_valid, seg, L, nblk, nc = _blocking(B, T)
    Tp = nc * seg
    if Tp != T:
        x = jnp.pad(x, ((0, 0), (0, Tp - T), (0, 0)))
        p = jnp.pad(p, ((0, 0), (0, 0), (0, Tp - T), (0, 0)))
    M = B * Tp
    tm = _row_tile(M, TM_SMALL)
    tm_proj = _row_tile(M, TM_PROJ)
    tm_mlp = _row_tile(M, TM_MLP)
    x2 = x.reshape(M, D)
    p2 = p.reshape(depth, M, -1)
    blk = dict(nblk=nblk, nc=nc, L=L, seg=seg, t_valid=t_valid)
    conv_o, delta_o, lru_o, mc_o, mn_o, mm_o = [], [], [], [], [], []
    for layer in range(depth):
        j = layer // 2
        if layer % 2 == 0:
            proj, gates = _proj(x2, W['w_in_e'], j, W['wg_e'][j], tm_proj, PROJ_TN)
            conv0 = jnp.pad(conv[j], ((0, 0), (CONV_TAIL - (CONV_W - 1), 0), (0, 0)))
            (y, convn, s_new, h_new), conv_off = _mixer_e(
                proj, gates, conv0.reshape(B * CONV_TAIL, CONV_CH), delta[j], lru[j][:, None, :],
                W['w_conv_e'][j], W['b_conv_e'][j][None, :],
                _pad_lanes(W['a_log_e'][j]), _pad_lanes(W['dt_bias_e'][j]),
                W['delta_norm_e'][j][None, :], W['lru_wri_e'][j], W['lru_br_e'][j][None, :],
                W['lru_bi_e'][j][None, :], W['lru_lambda_e'][j][None, :], **blk)
            conv_o.append(convn.reshape(B, SUBLANES, CONV_CH)[:, conv_off:conv_off + CONV_W - 1])
            delta_o.append(s_new)
            lru_o.append(h_new[:, 0])
            w_out = W['w_out_e']
        else:
            proj, gates = _proj(x2, W['w_in_o'], j, W['wg_o'][j], tm_proj, PROJ_TN)
            m0 = jnp.pad(mm[j], ((0, 0), (0, LANES - HEADS)))[:, None, :]
            y, c_new, n_new, m_new = _mixer_o(
                proj, gates, mc[j], mn[j], m0,
                _pad_lanes(W['b_ig_o'][j]), _pad_lanes(W['b_fg_o'][j], HEADS),
                W['mlstm_norm_o'][j][None, :], **blk)
            mc_o.append(c_new)
            mn_o.append(n_new)
            mm_o.append(m_new[:, 0, :HEADS])
            w_out = W['w_out_o']
        h = _outproj_ln(y, x2, w_out, j, W['ln1_g'][layer][None, :], W['ln1_b'][layer][None, :],
                        tm, alpha)
        h = _mlp_ln(h, W['w_up'], W['w_down'], layer, W['ln2_g'][layer][None, :],
                    W['ln2_b'][layer][None, :], tm_mlp, MLP_TF, alpha)
        x2 = _ple(h, p2, layer, W['w_ple_gate'], W['w_ple'], tm)
    return (x2.reshape(B, Tp, D)[:, :T], jnp.stack(conv_o), jnp.stack(delta_o), jnp.stack(lru_o),
            jnp.stack(mc_o), jnp.stack(mn_o), jnp.stack(mm_o))


def kernel(x_prompt, x_sample, p_prompt, p_sample, state_conv, state_delta, state_lru,
           state_mlstm_c, state_mlstm_n, state_mlstm_m, w_in_e, w_conv_e, b_conv_e, a_log_e,
           dt_bias_e, delta_norm_e, lru_wr_e, lru_br_e, lru_wi_e, lru_bi_e, lru_lambda_e, w_out_e,
           w_in_o, b_ig_o, b_fg_o, mlstm_norm_o, w_out_o, ln1_g, ln1_b, ln2_g, ln2_b, w_up, w_down,
           w_ple, w_ple_gate):
    depth = ln1_g.shape[0]
    n_even, n_odd = w_in_e.shape[0], w_in_o.shape[0]
    alpha = (2 * depth) ** 0.25
    wt_e = jnp.swapaxes(w_in_e, 1, 2)
    wt_o = jnp.swapaxes(w_in_o, 1, 2)
    W = dict(
        w_in_e=wt_e, w_in_o=wt_o,
        wg_e=[_gate_weight_t(wt_e[j]) for j in range(n_even)],
        wg_o=[_gate_weight_t(wt_o[j]) for j in range(n_odd)],
        w_out_e=w_out_e.astype(BF16), w_out_o=w_out_o.astype(BF16),
        w_up=w_up, w_down=w_down,
        w_ple=w_ple.astype(BF16), w_ple_gate=w_ple_gate.astype(BF16),
        w_conv_e=w_conv_e, b_conv_e=b_conv_e, a_log_e=a_log_e, dt_bias_e=dt_bias_e,
        delta_norm_e=delta_norm_e, lru_br_e=lru_br_e,
        lru_wri_e=jnp.concatenate([lru_wr_e, lru_wi_e], axis=-1).astype(BF16),
        lru_bi_e=lru_bi_e, lru_lambda_e=lru_lambda_e, b_ig_o=b_ig_o, b_fg_o=b_fg_o,
        mlstm_norm_o=mlstm_norm_o, ln1_g=ln1_g, ln1_b=ln1_b, ln2_g=ln2_g, ln2_b=ln2_b)
    bp = x_prompt.shape[0]
    zeros = lambda a: jnp.zeros((a.shape[0], bp) + a.shape[2:], F32)
    out_p = _trunk(x_prompt, p_prompt, zeros(state_conv), zeros(state_delta), zeros(state_lru),
                   zeros(state_mlstm_c), zeros(state_mlstm_n), zeros(state_mlstm_m), W, depth, alpha)
    out_s = _trunk(x_sample, p_sample, state_conv, state_delta, state_lru,
                   state_mlstm_c, state_mlstm_n, state_mlstm_m, W, depth, alpha)
    return (out_p[0], out_s[0]) + tuple(out_p[1:]) + tuple(out_s[1:])
```

```python
import functools
import math

import jax
import jax.numpy as jnp
from jax import lax
from jax.experimental import pallas as pl
from jax.experimental.pallas import tpu as pltpu

F32 = jnp.float32
BF16 = jnp.bfloat16
HI = lax.Precision.HIGHEST

LANES = 128
SUBLANES = 8
VMEM_LIMIT = 56 * 1024 * 1024

HEADS = 8
DK = 128
A_DV = 128
C_DV = 256
LRU_W = 1024
LRU_BLOCKS = 8
LRU_C = 8.0
CONV_W = 4
CONV_CH = 4096
CHUNK = 64
LN_EPS = 1e-5
RMS_EPS = 1e-6
L2_EPS = 1e-6
PROJ_MAIN = 6144
PROJ_TN = 512
MLP_TF = 512
TM_PROJ = 2048
TM_MLP = 1024
TM_SMALL = 512
CONV_TAIL = SUBLANES
SEQS_PER_BLOCK = 8
NEG_BIG = -1e30


def _dot_f32(a, b):
    return lax.dot_general(a, b, (((1,), (0,)), ((), ())), precision=HI,
                           preferred_element_type=F32)


def _mm(a, b):
    return lax.dot_general(a.astype(BF16), b.astype(BF16), (((1,), (0,)), ((), ())),
                           preferred_element_type=F32)


def _mm_nt(a, b):
    return lax.dot_general(a.astype(BF16), b.astype(BF16), (((1,), (1,)), ((), ())),
                           preferred_element_type=F32)


def _mm_tn(a, b):
    return lax.dot_general(a.astype(BF16), b.astype(BF16), (((0,), (0,)), ((), ())),
                           preferred_element_type=F32)


def _sigmoid(x):
    return 1.0 / (1.0 + jnp.exp(-x))


def _silu(x):
    return x * _sigmoid(x)


def _softplus(x):
    return jnp.maximum(x, 0.0) + jnp.log1p(jnp.exp(-jnp.abs(x)))


def _gelu_tanh(x):
    return 0.5 * x * (1.0 + jnp.tanh(math.sqrt(2.0 / math.pi) * (x + 0.044715 * (x * x * x))))


def _layer_norm(x, g, b):
    mu = jnp.mean(x, axis=-1, keepdims=True)
    xc = x - mu
    var = jnp.mean(xc * xc, axis=-1, keepdims=True)
    return xc * lax.rsqrt(var + LN_EPS) * g + b


def _col_to_row(col, eye):
    return jnp.sum(jnp.where(eye, col, 0.0), axis=0, keepdims=True)


def _seg_masks(L, seg):
    ri = lax.broadcasted_iota(jnp.int32, (L, L), 0)
    ci = lax.broadcasted_iota(jnp.int32, (L, L), 1)
    same = (ri // seg) == (ci // seg)
    return same, same & (ri >= ci), same & (ri > ci), ri == ci


def _seg_tile(a, s, seg):
    if seg >= SUBLANES:
        return a[s * seg:(s + 1) * seg]
    t = (s * seg) // SUBLANES
    return a[t * SUBLANES:(t + 1) * SUBLANES]


def _seg_only(a_tile, s, seg):
    if seg >= SUBLANES:
        return a_tile
    row = lax.broadcasted_iota(jnp.int32, (SUBLANES, 1), 0)
    lo = (s * seg) % SUBLANES
    return jnp.where((row >= lo) & (row < lo + seg), a_tile, 0.0)


def _seg_join(parts, seg):
    if seg >= SUBLANES:
        return jnp.concatenate(parts, axis=0)
    per_tile = SUBLANES // seg
    row = lax.broadcasted_iota(jnp.int32, (SUBLANES, 1), 0)
    tiles = []
    for t in range(len(parts) // per_tile):
        tile = parts[t * per_tile]
        for i in range(1, per_tile):
            tile = jnp.where(row >= i * seg, parts[t * per_tile + i], tile)
        tiles.append(tile)
    return jnp.concatenate(tiles, axis=0)


def _seg_rows(vals, L, seg):
    if seg >= SUBLANES:
        return jnp.concatenate([jnp.broadcast_to(v, (seg, v.shape[1])) for v in vals], axis=0)
    seg_id = lax.broadcasted_iota(jnp.int32, (L, 1), 0) // seg
    out = jnp.broadcast_to(vals[0], (L, vals[0].shape[1]))
    for i in range(1, len(vals)):
        out = jnp.where(seg_id == i, vals[i], out)
    return out


def _params(sem):
    return pltpu.CompilerParams(dimension_semantics=sem, vmem_limit_bytes=VMEM_LIMIT)


def _proj_body(x_ref, wt_ref, wgt_ref, o_ref, og_ref, xb_ref):
    @pl.when(pl.program_id(1) == 0)
    def _():
        xb = x_ref[...].astype(BF16)
        xb_ref[...] = xb
        og_ref[...] = lax.dot_general(xb, wgt_ref[...].astype(BF16), (((1,), (1,)), ((), ())),
                                      preferred_element_type=F32)

    o_ref[...] = lax.dot_general(xb_ref[...], wt_ref[...].astype(BF16), (((1,), (1,)), ((), ())),
                                 preferred_element_type=F32)


def _proj(x, wt_stack, layer, wgt, tm, tn):
    M, K = x.shape
    return pl.pallas_call(
        _proj_body,
        grid=(M // tm, PROJ_MAIN // tn),
        in_specs=[pl.BlockSpec((tm, K), lambda i, j: (i, 0), pipeline_mode=pl.Buffered(1)),
                  pl.BlockSpec((None, tn, K), lambda i, j: (layer, j, 0)),
                  pl.BlockSpec((LANES, K), lambda i, j: (0, 0))],
        out_specs=[pl.BlockSpec((tm, tn), lambda i, j: (i, j)),
                   pl.BlockSpec((tm, LANES), lambda i, j: (i, 0))],
        out_shape=[jax.ShapeDtypeStruct((M, PROJ_MAIN), F32),
                   jax.ShapeDtypeStruct((M, LANES), F32)],
        scratch_shapes=[pltpu.VMEM((tm, K), BF16)],
        compiler_params=_params(("parallel", "arbitrary")),
        name="in_proj",
    )(x, wt_stack, wgt)


def _outproj_ln_body(y_ref, x_ref, w_ref, g_ref, b_ref, o_ref, *, alpha):
    mix = jnp.dot(y_ref[...].astype(BF16), w_ref[...], preferred_element_type=F32)
    o_ref[...] = _layer_norm(alpha * x_ref[...] + mix, g_ref[...], b_ref[...])


def _outproj_ln(y, x, w_stack, layer, g, b, tm, alpha):
    M, K = y.shape
    D = w_stack.shape[2]
    row = lambda i: (i, 0)
    fixed = lambda i: (0, 0)
    return pl.pallas_call(
        functools.partial(_outproj_ln_body, alpha=alpha),
        grid=(M // tm,),
        in_specs=[pl.BlockSpec((tm, K), row), pl.BlockSpec((tm, D), row),
                  pl.BlockSpec((None, K, D), lambda i: (layer, 0, 0), pipeline_mode=pl.Buffered(1)),
                  pl.BlockSpec((1, D), fixed), pl.BlockSpec((1, D), fixed)],
        out_specs=pl.BlockSpec((tm, D), row),
        out_shape=jax.ShapeDtypeStruct((M, D), F32),
        compiler_params=_params(("parallel",)),
        name="out_proj_ln",
    )(y, x, w_stack, g, b)


def _mlp_ln_body(h_ref, wu_ref, wd_ref, g_ref, b_ref, o_ref, hb_ref, *, alpha):
    f = pl.program_id(1)

    @pl.when(f == 0)
    def _():
        hb_ref[...] = h_ref[...].astype(BF16)
        o_ref[...] = jnp.zeros_like(o_ref)

    up = jnp.dot(hb_ref[...], wu_ref[...].astype(BF16), preferred_element_type=F32)
    act = jnp.square(jnp.maximum(up, 0.0)).astype(BF16)
    o_ref[...] += jnp.dot(act, wd_ref[...].astype(BF16), preferred_element_type=F32)

    @pl.when(f == pl.num_programs(1) - 1)
    def _():
        o_ref[...] = _layer_norm(alpha * h_ref[...] + o_ref[...], g_ref[...], b_ref[...])


def _mlp_ln(h, wu_stack, wd_stack, layer, g, b, tm, tf, alpha):
    M, D = h.shape
    FF = wu_stack.shape[2]
    return pl.pallas_call(
        functools.partial(_mlp_ln_body, alpha=alpha),
        grid=(M // tm, FF // tf),
        in_specs=[pl.BlockSpec((tm, D), lambda i, f: (i, 0), pipeline_mode=pl.Buffered(1)),
                  pl.BlockSpec((None, D, tf), lambda i, f: (layer, 0, f)),
                  pl.BlockSpec((None, tf, D), lambda i, f: (layer, f, 0)),
                  pl.BlockSpec((1, D), lambda i, f: (0, 0)),
                  pl.BlockSpec((1, D), lambda i, f: (0, 0))],
        out_specs=pl.BlockSpec((tm, D), lambda i, f: (i, 0), pipeline_mode=pl.Buffered(1)),
        out_shape=jax.ShapeDtypeStruct((M, D), F32),
        scratch_shapes=[pltpu.VMEM((tm, D), BF16)],
        compiler_params=_params(("parallel", "arbitrary")),
        name="mlp_ln",
    )(h, wu_stack, wd_stack, g, b)


def _ple_body(h_ref, p_ref, wg_ref, wp_ref, o_ref):
    h = h_ref[...]
    gate = _sigmoid(jnp.dot(h.astype(BF16), wg_ref[...], preferred_element_type=F32))
    emb = jnp.dot(p_ref[...].astype(BF16), wp_ref[...], preferred_element_type=F32)
    o_ref[...] = h + gate * emb


def _ple(h, p_stack, layer, wg_stack, wp_stack, tm):
    M, D = h.shape
    P = p_stack.shape[2]
    row = lambda i: (i, 0)
    return pl.pallas_call(
        _ple_body,
        grid=(M // tm,),
        in_specs=[pl.BlockSpec((tm, D), row),
                  pl.BlockSpec((None, tm, P), lambda i: (layer, i, 0)),
                  pl.BlockSpec((None, D, D), lambda i: (layer, 0, 0), pipeline_mode=pl.Buffered(1)),
                  pl.BlockSpec((None, P, D), lambda i: (layer, 0, 0), pipeline_mode=pl.Buffered(1))],
        out_specs=pl.BlockSpec((tm, D), row),
        out_shape=jax.ShapeDtypeStruct((M, D), F32),
        compiler_params=_params(("parallel",)),
        name="ple_gate",
    )(h, p_stack, wg_stack, wp_stack)


def _mixer_e_body(proj_ref, gate_ref, conv0_ref, s0_ref, h0_ref, wconv_ref, bconv_ref, alog_ref,
                  dtb_ref, nw_ref, wri_ref, br_ref, bi_ref, lam_ref,
                  y_ref, convn_ref, s_ref, hl_ref,
                  cbuf, a_scr, bx_scr, hseq, *, L, seg, t_valid, conv_win):
    c = pl.program_id(1)
    heads = range(HEADS)
    nseg = L // seg
    segs = range(nseg)
    stride = CONV_TAIL + seg

    packed = seg < SUBLANES

    @pl.when(c == 0)
    def _():
        if not packed:
            for s in segs:
                cbuf[s * stride:s * stride + CONV_TAIL, :] = conv0_ref[s * CONV_TAIL:(s + 1) * CONV_TAIL, :]
        s_ref[...] = s0_ref[...]
        hl_ref[...] = h0_ref[...]

    same, tril, strict, eye = _seg_masks(L, seg)
    tril_f = jnp.where(tril, 1.0, 0.0).astype(F32)
    same_f = jnp.where(same, 1.0, 0.0).astype(F32)
    row_ok = (lax.broadcasted_iota(jnp.int32, (L, 1), 0) % seg) < t_valid
    s_old = [[s_ref[s, h] for h in heads] for s in segs]
    h_old = [hl_ref[s] for s in segs]

    if packed:
        u = proj_ref[:, 0:CONV_CH]
        hist = conv0_ref[...]
        rowm = lax.broadcasted_iota(jnp.int32, (L, 1), 0) % seg
        acc = bconv_ref[...] + u * wconv_ref[CONV_W - 1:CONV_W, :]
        for j in range(1, CONV_W):
            shifted = jnp.where(rowm >= j, pltpu.roll(u, j, axis=0), pltpu.roll(hist, (j - seg) % L, axis=0))
            acc = acc + shifted * wconv_ref[CONV_W - 1 - j:CONV_W - j, :]
        convn_ref[...] = u
    else:
        for s in segs:
            cbuf[s * stride + CONV_TAIL:(s + 1) * stride, :] = proj_ref[s * seg:(s + 1) * seg, 0:CONV_CH]
        full = cbuf[...]
        prev = pltpu.roll(full, 1, axis=0)
        pair_new = full * wconv_ref[3:4, :] + prev * wconv_ref[2:3, :]
        pair_old = full * wconv_ref[1:2, :] + prev * wconv_ref[0:1, :]
        conv_full = bconv_ref[...] + pair_new + pltpu.roll(pair_old, 2, axis=0)
        acc = jnp.concatenate([conv_full[s * stride + CONV_TAIL:(s + 1) * stride, :] for s in segs], axis=0)

    gp = gate_ref[...]
    g_all = -jnp.exp(alog_ref[...]) * _softplus(gp + dtb_ref[...])
    beta_all = _sigmoid(gp)
    if t_valid < seg:
        g_all = jnp.where(row_ok, g_all, 0.0)
        beta_all = jnp.where(row_ok, beta_all, 0.0)
    gc_all = _dot_f32(tril_f, g_all)
    gl_all = _dot_f32(same_f, g_all)

    q_l, k_l, kb_l, gcol_l, egc_l, y_l = [], [], [], [], [], []
    qa_l = [_silu(acc[:, h * DK:(h + 1) * DK]) for h in heads]
    ka_l = [_silu(acc[:, HEADS * DK + h * DK:HEADS * DK + (h + 1) * DK]) for h in heads]
    qn_l = [lax.rsqrt(jnp.sum(qa_l[h] * qa_l[h], axis=-1, keepdims=True) + L2_EPS) * (DK ** -0.5)
            for h in heads]
    kn_l = [lax.rsqrt(jnp.sum(ka_l[h] * ka_l[h], axis=-1, keepdims=True) + L2_EPS) for h in heads]
    for h in heads:
        q = qa_l[h] * qn_l[h]
        k = ka_l[h] * kn_l[h]
        v = _silu(acc[:, 2 * HEADS * DK + h * A_DV:2 * HEADS * DK + (h + 1) * A_DV])
        gcol = gc_all[:, h:h + 1]
        bcol = beta_all[:, HEADS + h:HEADS + h + 1]
        egc = jnp.exp(gcol)
        kb = k * bcol
        q_l.append(q)
        k_l.append(k)
        kb_l.append(kb)
        gcol_l.append(gcol)
        egc_l.append(egc)
        y_l.append(jnp.concatenate([v * bcol, kb * egc], axis=1))
    kq_k = [_mm_nt(jnp.concatenate([kb_l[h], q_l[h]], axis=0), k_l[h]) for h in heads]
    p_l, qk_l = [], []
    for h in heads:
        grow = _col_to_row(gcol_l[h], eye)
        decay = jnp.where(tril, jnp.exp(jnp.where(tril, gcol_l[h] - grow, 0.0)), 0.0)
        p_l.append(jnp.where(strict, -(kq_k[h][0:L] * decay), 0.0))
        qk_l.append(kq_k[h][L:2 * L] * decay)

    n_levels = max(1, math.ceil(math.log2(t_valid)))
    r_l = p_l
    if n_levels >= 2:
        p_l = [_mm(p_l[h], p_l[h]) for h in heads]
        for _ in range(n_levels - 2):
            out = [_mm(jnp.concatenate([p_l[h], r_l[h]], axis=0), p_l[h]) for h in heads]
            r_l = [r_l[h] + p_l[h] + out[h][L:2 * L] for h in heads]
            p_l = [out[h][0:L] for h in heads]
        out = [_mm(r_l[h], p_l[h]) for h in heads]
        r_l = [r_l[h] + p_l[h] + out[h] for h in heads]
    y_l = [y_l[h] + _mm(r_l[h], y_l[h]) for h in heads]

    R = max(seg, SUBLANES)
    qe_l = [q_l[h] * egc_l[h] for h in heads]
    wq_s = [[_mm(jnp.concatenate([_seg_tile(y_l[h][:, A_DV:A_DV + DK], s, seg),
                                  _seg_tile(qe_l[h], s, seg)], axis=0), s_old[s][h])
             for h in heads] for s in segs]
    v_new = [y_l[h][:, 0:A_DV] - _seg_join([wq_s[s][h][0:R] for s in segs], seg) for h in heads]
    o_intra = [_mm(qk_l[h], v_new[h]) for h in heads]
    kd_l = [k_l[h] * jnp.exp(gl_all[:, h:h + 1] - gcol_l[h]) for h in heads]
    s_upd = [[s_old[s][h] * jnp.exp(gl_all[s * seg:s * seg + 1, h:h + 1])
              + _mm_tn(_seg_only(_seg_tile(kd_l[h], s, seg), s, seg), _seg_tile(v_new[h], s, seg))
              for h in heads] for s in segs]

    bw = LRU_W // LRU_BLOCKS
    xr = acc[:, 3 * HEADS * DK:3 * HEADS * DK + LRU_W]
    sp = _softplus(-lam_ref[...])
    for n in range(LRU_BLOCKS):
        sl = slice(n * bw, (n + 1) * bw)
        xb = xr[:, sl]
        ri_pre = _mm(xb, wri_ref[n])
        r_pre = ri_pre[:, 0:bw] + br_ref[:, sl]
        i_pre = ri_pre[:, bw:2 * bw] + bi_ref[:, sl]
        log_a = -LRU_C * _sigmoid(r_pre) * sp[:, sl]
        a = jnp.exp(log_a)
        a_scr[:, sl] = a
        bx_scr[:, sl] = jnp.sqrt(-jnp.tanh(log_a) * (a * a + 1.0)) * _sigmoid(i_pre) * xb

    for s in segs:
        hc = h_old[s]
        for t in range(seg):
            row = s * seg + t
            if t < t_valid:
                hc = a_scr[row:row + 1, :] * hc + bx_scr[row:row + 1, :]
            hseq[row:row + 1, :] = hc
        hl_ref[s] = hc

    o_l = [_seg_join([wq_s[s][h][R:2 * R] for s in segs], seg) + o_intra[h] for h in heads]
    rms_l = [lax.rsqrt(jnp.mean(o_l[h] * o_l[h], axis=-1, keepdims=True) + RMS_EPS) for h in heads]
    for h in heads:
        z = proj_ref[:, CONV_CH + h * A_DV:CONV_CH + (h + 1) * A_DV]
        y_ref[:, h * A_DV:(h + 1) * A_DV] = o_l[h] * rms_l[h] * nw_ref[...] * _silu(z)
    for s in segs:
        for h in heads:
            s_ref[s, h] = s_upd[s][h]
    gate = proj_ref[:, CONV_CH + HEADS * A_DV:CONV_CH + HEADS * A_DV + LRU_W]
    y_ref[:, HEADS * A_DV:HEADS * A_DV + LRU_W] = hseq[...] * _gelu_tanh(gate)

    if not packed:
        @pl.when(c == pl.num_programs(1) - 1)
        def _():
            for s in segs:
                convn_ref[s * SUBLANES:(s + 1) * SUBLANES, :] = (
                    cbuf[s * stride + conv_win:s * stride + conv_win + SUBLANES, :])

        for s in segs:
            cbuf[s * stride:s * stride + CONV_TAIL, :] = cbuf[s * stride + seg:s * stride + seg + CONV_TAIL, :]


def _mixer_e(proj, gates, conv0, s0, h0, wconv, bconv, alog, dtb, nw, wri, br, bi, lam,
             *, nblk, nc, L, seg, t_valid):
    nseg = L // seg
    nseq = nblk * nseg
    assert nc == 1 or nseg == 1
    assert t_valid >= CONV_W - 1
    if seg < SUBLANES:
        assert nc == 1 and t_valid == seg and L % SUBLANES == 0
        hist_rows, conv_win, conv_off = L, 0, seg - (CONV_W - 1)
    else:
        conv_row = CONV_TAIL - (CONV_W - 1) + t_valid
        conv_win = (conv_row // SUBLANES) * SUBLANES
        conv_off = conv_row - conv_win
        assert conv_off + CONV_W - 1 <= SUBLANES
        hist_rows = nseg * SUBLANES
    mix = HEADS * A_DV + LRU_W
    chunk = lambda b, c: (b * nc + c, 0)
    fix2 = lambda b, c: (0, 0)
    fix3 = lambda b, c: (0, 0, 0)
    outs = pl.pallas_call(
        functools.partial(_mixer_e_body, L=L, seg=seg, t_valid=t_valid, conv_win=conv_win),
        grid=(nblk, nc),
        in_specs=[pl.BlockSpec((L, PROJ_MAIN), chunk),
                  pl.BlockSpec((L, LANES), chunk),
                  pl.BlockSpec((hist_rows, CONV_CH), lambda b, c: (b, 0)),
                  pl.BlockSpec((nseg, HEADS, DK, A_DV), lambda b, c: (b, 0, 0, 0)),
                  pl.BlockSpec((nseg, 1, LRU_W), lambda b, c: (b, 0, 0)),
                  pl.BlockSpec((CONV_W, CONV_CH), fix2),
                  pl.BlockSpec((1, CONV_CH), fix2),
                  pl.BlockSpec((1, LANES), fix2),
                  pl.BlockSpec((1, LANES), fix2),
                  pl.BlockSpec((1, A_DV), fix2),
                  pl.BlockSpec((LRU_BLOCKS, LRU_W // LRU_BLOCKS, 2 * LRU_W // LRU_BLOCKS), fix3),
                  pl.BlockSpec((1, LRU_W), fix2),
                  pl.BlockSpec((1, LRU_W), fix2),
                  pl.BlockSpec((1, LRU_W), fix2)],
        out_specs=[pl.BlockSpec((L, mix), chunk),
                   pl.BlockSpec((hist_rows, CONV_CH), lambda b, c: (b, 0)),
                   pl.BlockSpec((nseg, HEADS, DK, A_DV), lambda b, c: (b, 0, 0, 0)),
                   pl.BlockSpec((nseg, 1, LRU_W), lambda b, c: (b, 0, 0))],
        out_shape=[jax.ShapeDtypeStruct((nblk * nc * L, mix), F32),
                   jax.ShapeDtypeStruct((nblk * hist_rows, CONV_CH), F32),
                   jax.ShapeDtypeStruct((nseq, HEADS, DK, A_DV), F32),
                   jax.ShapeDtypeStruct((nseq, 1, LRU_W), F32)],
        scratch_shapes=[pltpu.VMEM((nseg * (CONV_TAIL + seg), CONV_CH), F32),
                        pltpu.VMEM((L, LRU_W), F32),
                        pltpu.VMEM((L, LRU_W), F32),
                        pltpu.VMEM((L, LRU_W), F32)],
        compiler_params=_params(("parallel", "arbitrary")),
        name="delta_lru_mixer",
    )(proj, gates, conv0, s0, h0, wconv, bconv, alog, dtb, nw, wri, br, bi, lam)
    return outs, conv_off


def _mixer_o_body(proj_ref, gate_ref, c0_ref, n0_ref, m0_ref, big_ref, bfg_ref, nw_ref,
                  y_ref, c_ref, n_ref, m_ref, *, L, seg, t_valid):
    ci = pl.program_id(1)
    heads = range(HEADS)
    nseg = L // seg
    segs = range(nseg)

    @pl.when(ci == 0)
    def _():
        c_ref[...] = c0_ref[...]
        n_ref[...] = n0_ref[...]
        m_ref[...] = m0_ref[...]

    same, tril, _, eye = _seg_masks(L, seg)
    tril_f = jnp.where(tril, 1.0, 0.0).astype(F32)
    same_f = jnp.where(same, 1.0, 0.0).astype(F32)
    row_id = lax.broadcasted_iota(jnp.int32, (L, 1), 0)
    row_ok = (row_id % seg) < t_valid
    c_old = [[c_ref[s, h] for h in heads] for s in segs]
    n_old = [n_ref[s] for s in segs]
    m_old = [m_ref[s] for s in segs]

    gp = gate_ref[...]
    ig_all = gp + big_ref[...]
    logf_all = -_softplus(-(gp + bfg_ref[...]))
    if t_valid < seg:
        ig_all = jnp.where(row_ok, ig_all, NEG_BIG)
        logf_all = jnp.where(row_ok, logf_all, 0.0)
    bcum_all = _dot_f32(tril_f, logf_all)
    blast_all = _dot_f32(same_f, logf_all)
    mrow_all = _seg_rows(m_old, L, seg)

    q_l = [proj_ref[:, h * DK:(h + 1) * DK] for h in heads]
    k_l = [proj_ref[:, HEADS * DK + h * DK:HEADS * DK + (h + 1) * DK] * (DK ** -0.5) for h in heads]
    v_l = [proj_ref[:, 2 * HEADS * DK + h * C_DV:2 * HEADS * DK + (h + 1) * C_DV] for h in heads]
    qk_l = [_mm_nt(q_l[h], k_l[h]) for h in heads]
    qc_l = [_seg_join([_mm_nt(_seg_tile(q_l[h], s, seg), c_old[s][h]) for s in segs], seg) for h in heads]

    s_l, e_l, mt_l, sc_l, ws_l, mnew_l = [], [], [], [], [], []
    for h in heads:
        bcol = bcum_all[:, HEADS + h:HEADS + h + 1]
        blast = blast_all[:, HEADS + h:HEADS + h + 1]
        icol = ig_all[:, h:h + 1]
        mcol = mrow_all[:, h:h + 1]
        brow = _col_to_row(bcol, eye)
        irow = _col_to_row(icol, eye)
        d_intra = jnp.where(tril, bcol - brow + irow, NEG_BIG)
        ge_mat = jnp.where(same, blast - brow + irow, NEG_BIG)
        ge_col = blast - bcol + icol
        inter = bcol + mcol
        m_t = jnp.maximum(inter, jnp.max(d_intra, axis=-1, keepdims=True))
        m_new = jnp.maximum(blast + mcol, jnp.max(ge_mat, axis=-1, keepdims=True))
        e_l.append(jnp.exp(inter - m_t))
        mt_l.append(m_t)
        s_l.append(qk_l[h] * jnp.exp(d_intra - m_t))
        sc_l.append(jnp.exp(blast + mcol - m_new))
        ws_l.append(jnp.exp(ge_col - m_new))
        mnew_l.append(m_new)

    sv = [_mm(s_l[h], v_l[h]) for h in heads]
    vw_l = [v_l[h] * ws_l[h] for h in heads]
    kw_l = [k_l[h] * ws_l[h] for h in heads]
    kv = [[_mm_tn(_seg_only(_seg_tile(vw_l[h], s, seg), s, seg), _seg_tile(k_l[h], s, seg)) for h in heads]
          for s in segs]

    nrow_l = [_seg_rows([n_old[s][h:h + 1, :] for s in segs], L, seg) for h in heads]
    den_l = [e_l[h] * jnp.sum(q_l[h] * nrow_l[h], axis=-1, keepdims=True)
             + jnp.sum(s_l[h], axis=-1, keepdims=True) for h in heads]
    inv_l = [1.0 / jnp.maximum(jnp.abs(den_l[h]), jnp.exp(-mt_l[h])) for h in heads]
    hh_l = [(e_l[h] * qc_l[h] + sv[h]) * inv_l[h] for h in heads]
    rms_l = [lax.rsqrt(jnp.mean(hh_l[h] * hh_l[h], axis=-1, keepdims=True) + RMS_EPS) for h in heads]
    for h in heads:
        o_pre = proj_ref[:, 2 * HEADS * DK + HEADS * C_DV + h * C_DV:
                         2 * HEADS * DK + HEADS * C_DV + (h + 1) * C_DV]
        y_ref[:, h * C_DV:(h + 1) * C_DV] = (hh_l[h] * rms_l[h] * nw_ref[:, h * C_DV:(h + 1) * C_DV]
                                             * _sigmoid(o_pre))
    lane = lax.broadcasted_iota(jnp.int32, (1, LANES), 1)
    for s in segs:
        r0 = s * seg
        m_row = m_old[s]
        n_rows = []
        for h in heads:
            sc = sc_l[h][r0:r0 + 1, :]
            c_ref[s, h] = sc * c_old[s][h] + kv[s][h]
            n_rows.append(sc * n_old[s][h:h + 1, :]
                          + jnp.sum(_seg_only(_seg_tile(kw_l[h], s, seg), s, seg), axis=0, keepdims=True))
            m_row = jnp.where(lane == h, mnew_l[h][r0:r0 + 1, :], m_row)
        n_ref[s] = jnp.concatenate(n_rows, axis=0)
        m_ref[s] = m_row


def _mixer_o(proj, gates, c0, n0, m0, big, bfg, nw, *, nblk, nc, L, seg, t_valid):
    nseg = L // seg
    nseq = nblk * nseg
    assert nc == 1 or nseg == 1
    cv = HEADS * C_DV
    chunk = lambda b, c: (b * nc + c, 0)
    fix2 = lambda b, c: (0, 0)
    return pl.pallas_call(
        functools.partial(_mixer_o_body, L=L, seg=seg, t_valid=t_valid),
        grid=(nblk, nc),
        in_specs=[pl.BlockSpec((L, PROJ_MAIN), chunk),
                  pl.BlockSpec((L, LANES), chunk),
                  pl.BlockSpec((nseg, HEADS, C_DV, DK), lambda b, c: (b, 0, 0, 0)),
                  pl.BlockSpec((nseg, HEADS, DK), lambda b, c: (b, 0, 0)),
                  pl.BlockSpec((nseg, 1, LANES), lambda b, c: (b, 0, 0)),
                  pl.BlockSpec((1, LANES), fix2),
                  pl.BlockSpec((1, LANES), fix2),
                  pl.BlockSpec((1, cv), fix2)],
        out_specs=[pl.BlockSpec((L, cv), chunk),
                   pl.BlockSpec((nseg, HEADS, C_DV, DK), lambda b, c: (b, 0, 0, 0)),
                   pl.BlockSpec((nseg, HEADS, DK), lambda b, c: (b, 0, 0)),
                   pl.BlockSpec((nseg, 1, LANES), lambda b, c: (b, 0, 0))],
        out_shape=[jax.ShapeDtypeStruct((nblk * nc * L, cv), F32),
                   jax.ShapeDtypeStruct((nseq, HEADS, C_DV, DK), F32),
                   jax.ShapeDtypeStruct((nseq, HEADS, DK), F32),
                   jax.ShapeDtypeStruct((nseq, 1, LANES), F32)],
        compiler_params=_params(("parallel", "arbitrary")),
        name="mlstm_mixer",
    )(proj, gates, c0, n0, m0, big, bfg, nw)


def _pad_lanes(v, offset=0):
    return jnp.pad(v.astype(F32), (offset, LANES - offset - v.shape[0]))[None, :]


def _gate_weight_t(wt):
    wg = wt[PROJ_MAIN:, :]
    return jnp.pad(wg, ((0, LANES - wg.shape[0]), (0, 0)))


def _row_tile(m, cap):
    t = cap
    while t >= SUBLANES:
        if m % t == 0:
            return t
        t //= 2
    raise ValueError(f"row count {m} is not a multiple of {SUBLANES}")


def _blocking(B, T):
    t_valid = math.gcd(T, CHUNK)
    nc = T // t_valid
    if SUBLANES % t_valid == 0 and nc == 1:
        seg = t_valid
    else:
        seg = -(-t_valid // SUBLANES) * SUBLANES
    assert nc == 1 or seg == t_valid
    nseg = 1
    if nc == 1:
        nseg = max(1, min(B, SEQS_PER_BLOCK))
        while B % nseg or (nseg * seg) % SUBLANES:
            nseg -= 1
        assert nseg >= 1
    return t_valid, seg, nseg * seg, B // nseg, nc


def _trunk(x, p, conv, delta, lru, mc, mn, mm, W, depth, alpha):
    B, T, D = x.shape
    t_valid, seg, L, nblk, nc = _blocking(B, T)
    Tp = nc * seg
    if Tp != T:
        x = jnp.pad(x, ((0, 0), (0, Tp - T), (0, 0)))
        p = jnp.pad(p, ((0, 0), (0, 0), (0, Tp - T), (0, 0)))
    M = B * Tp
    tm = _row_tile(M, TM_SMALL)
    tm_proj = _row_tile(M, TM_PROJ)
    tm_mlp = _row_tile(M, TM_MLP)
    x2 = x.reshape(M, D)
    p2 = p.reshape(depth, M, -1)
    blk = dict(nblk=nblk, nc=nc, L=L, seg=seg, t_valid=t_valid)
    hist = seg if seg < SUBLANES else CONV_TAIL
    conv_o, delta_o, lru_o, mc_o, mn_o, mm_o = [], [], [], [], [], []
    for layer in range(depth):
        j = layer // 2
        if layer % 2 == 0:
            proj, gates = _proj(x2, W['w_in_e'], j, W['wg_e'][j], tm_proj, PROJ_TN)
            conv0 = jnp.pad(conv[j], ((0, 0), (hist - (CONV_W - 1), 0), (0, 0)))
            (y, convn, s_new, h_new), conv_off = _mixer_e(
                proj, gates, conv0.reshape(B * hist, CONV_CH), delta[j], lru[j][:, None, :],
                W['w_conv_e'][j], W['b_conv_e'][j][None, :],
                _pad_lanes(W['a_log_e'][j]), _pad_lanes(W['dt_bias_e'][j]),
                W['delta_norm_e'][j][None, :], W['lru_wri_e'][j], W['lru_br_e'][j][None, :],
                W['lru_bi_e'][j][None, :], W['lru_lambda_e'][j][None, :], **blk)
            conv_o.append(convn.reshape(B, hist, CONV_CH)[:, conv_off:conv_off + CONV_W - 1])
            delta_o.append(s_new)
            lru_o.append(h_new[:, 0])
            w_out = W['w_out_e']
        else:
            proj, gates = _proj(x2, W['w_in_o'], j, W['wg_o'][j], tm_proj, PROJ_TN)
            m0 = jnp.pad(mm[j], ((0, 0), (0, LANES - HEADS)))[:, None, :]
            y, c_new, n_new, m_new = _mixer_o(
                proj, gates, mc[j], mn[j], m0,
                _pad_lanes(W['b_ig_o'][j]), _pad_lanes(W['b_fg_o'][j], HEADS),
                W['mlstm_norm_o'][j][None, :], **blk)
            mc_o.append(c_new)
            mn_o.append(n_new)
            mm_o.append(m_new[:, 0, :HEADS])
            w_out = W['w_out_o']
        h = _outproj_ln(y, x2, w_out, j, W['ln1_g'][layer][None, :], W['ln1_b'][layer][None, :],
                        tm, alpha)
        h = _mlp_ln(h, W['w_up'], W['w_down'], layer, W['ln2_g'][layer][None, :],
                    W['ln2_b'][layer][None, :], tm_mlp, MLP_TF, alpha)
        x2 = _ple(h, p2, layer, W['w_ple_gate'], W['w_ple'], tm)
    return (x2.reshape(B, Tp, D)[:, :T], jnp.stack(conv_o), jnp.stack(delta_o), jnp.stack(lru_o),
            jnp.stack(mc_o), jnp.stack(mn_o), jnp.stack(mm_o))


def kernel(x_prompt, x_sample, p_prompt, p_sample, state_conv, state_delta, state_lru,
           state_mlstm_c, state_mlstm_n, state_mlstm_m, w_in_e, w_conv_e, b_conv_e, a_log_e,
           dt_bias_e, delta_norm_e, lru_wr_e, lru_br_e, lru_wi_e, lru_bi_e, lru_lambda_e, w_out_e,
           w_in_o, b_ig_o, b_fg_o, mlstm_norm_o, w_out_o, ln1_g, ln1_b, ln2_g, ln2_b, w_up, w_down,
           w_ple, w_ple_gate):
    depth = ln1_g.shape[0]
    n_even, n_odd = w_in_e.shape[0], w_in_o.shape[0]
    alpha = (2 * depth) ** 0.25
    wt_e = jnp.swapaxes(w_in_e, 1, 2)
    wt_o = jnp.swapaxes(w_in_o, 1, 2)
    W = dict(
        w_in_e=wt_e, w_in_o=wt_o,
        wg_e=[_gate_weight_t(wt_e[j]) for j in range(n_even)],
        wg_o=[_gate_weight_t(wt_o[j]) for j in range(n_odd)],
        w_out_e=w_out_e.astype(BF16), w_out_o=w_out_o.astype(BF16),
        w_up=w_up, w_down=w_down,
        w_ple=w_ple.astype(BF16), w_ple_gate=w_ple_gate.astype(BF16),
        w_conv_e=w_conv_e, b_conv_e=b_conv_e, a_log_e=a_log_e, dt_bias_e=dt_bias_e,
        delta_norm_e=delta_norm_e, lru_br_e=lru_br_e,
        lru_wri_e=jnp.concatenate([lru_wr_e, lru_wi_e], axis=-1).astype(BF16),
        lru_bi_e=lru_bi_e, lru_lambda_e=lru_lambda_e, b_ig_o=b_ig_o, b_fg_o=b_fg_o,
        mlstm_norm_o=mlstm_norm_o, ln1_g=ln1_g, ln1_b=ln1_b, ln2_g=ln2_g, ln2_b=ln2_b)
    bp = x_prompt.shape[0]
    zeros = lambda a: jnp.zeros((a.shape[0], bp) + a.shape[2:], F32)
    out_p = _trunk(x_prompt, p_prompt, zeros(state_conv), zeros(state_delta), zeros(state_lru),
                   zeros(state_mlstm_c), zeros(state_mlstm_n), zeros(state_mlstm_m), W, depth, alpha)
    out_s = _trunk(x_sample, p_sample, state_conv, state_delta, state_lru,
                   state_mlstm_c, state_mlstm_n, state_mlstm_m, W, depth, alpha)
    return (out_p[0], out_s[0]) + tuple(out_p[1:]) + tuple(out_s[1:])
```

```python
import functools
import math

import jax
import jax.numpy as jnp
from jax import lax
from jax.experimental import pallas as pl
from jax.experimental.pallas import tpu as pltpu

F32 = jnp.float32
BF16 = jnp.bfloat16
HI = lax.Precision.HIGHEST

LANES = 128
SUBLANES = 8
VMEM_LIMIT = 56 * 1024 * 1024

HEADS = 8
DK = 128
A_DV = 128
C_DV = 256
LRU_W = 1024
LRU_BLOCKS = 8
LRU_C = 8.0
CONV_W = 4
CONV_CH = 4096
CHUNK = 64
LN_EPS = 1e-5
RMS_EPS = 1e-6
L2_EPS = 1e-6
PROJ_MAIN = 6144
PROJ_TN = 512
MLP_TF = 512
TM_PROJ = 2048
TM_MLP = 1024
TM_SMALL = 512
MXU_N = 256
MLP_ROW_BLOCKS = 2
CONV_TAIL = SUBLANES
SEQS_PER_BLOCK = 8
NEG_BIG = -1e30


def _dot_f32(a, b):
    return lax.dot_general(a, b, (((1,), (0,)), ((), ())), precision=HI,
                           preferred_element_type=F32)


def _mm(a, b):
    return lax.dot_general(a.astype(BF16), b.astype(BF16), (((1,), (0,)), ((), ())),
                           preferred_element_type=F32)


def _mm_nt(a, b):
    return lax.dot_general(a.astype(BF16), b.astype(BF16), (((1,), (1,)), ((), ())),
                           preferred_element_type=F32)


def _mm_tn(a, b):
    return lax.dot_general(a.astype(BF16), b.astype(BF16), (((0,), (0,)), ((), ())),
                           preferred_element_type=F32)


def _sigmoid(x):
    return 1.0 / (1.0 + jnp.exp(-x))


def _silu(x):
    return x * _sigmoid(x)


def _softplus(x):
    return jnp.maximum(x, 0.0) + jnp.log1p(jnp.exp(-jnp.abs(x)))


def _gelu_tanh(x):
    return 0.5 * x * (1.0 + jnp.tanh(math.sqrt(2.0 / math.pi) * (x + 0.044715 * (x * x * x))))


def _layer_norm(x, g, b):
    mu = jnp.mean(x, axis=-1, keepdims=True)
    xc = x - mu
    var = jnp.mean(xc * xc, axis=-1, keepdims=True)
    return xc * lax.rsqrt(var + LN_EPS) * g + b


def _col_to_row(col, eye):
    return jnp.sum(jnp.where(eye, col, 0.0), axis=0, keepdims=True)


def _seg_masks(L, seg):
    ri = lax.broadcasted_iota(jnp.int32, (L, L), 0)
    ci = lax.broadcasted_iota(jnp.int32, (L, L), 1)
    same = (ri // seg) == (ci // seg)
    return same, same & (ri >= ci), same & (ri > ci), ri == ci


def _seg_tile(a, s, seg):
    if seg >= SUBLANES:
        return a[s * seg:(s + 1) * seg]
    t = (s * seg) // SUBLANES
    return a[t * SUBLANES:(t + 1) * SUBLANES]


def _seg_only(a_tile, s, seg):
    if seg >= SUBLANES:
        return a_tile
    row = lax.broadcasted_iota(jnp.int32, (SUBLANES, 1), 0)
    lo = (s * seg) % SUBLANES
    return jnp.where((row >= lo) & (row < lo + seg), a_tile, 0.0)


def _seg_join(parts, seg):
    if seg >= SUBLANES:
        return jnp.concatenate(parts, axis=0)
    per_tile = SUBLANES // seg
    row = lax.broadcasted_iota(jnp.int32, (SUBLANES, 1), 0)
    tiles = []
    for t in range(len(parts) // per_tile):
        tile = parts[t * per_tile]
        for i in range(1, per_tile):
            tile = jnp.where(row >= i * seg, parts[t * per_tile + i], tile)
        tiles.append(tile)
    return jnp.concatenate(tiles, axis=0)


def _seg_rows(vals, L, seg):
    if seg >= SUBLANES:
        return jnp.concatenate([jnp.broadcast_to(v, (seg, v.shape[1])) for v in vals], axis=0)
    seg_id = lax.broadcasted_iota(jnp.int32, (L, 1), 0) // seg
    out = jnp.broadcast_to(vals[0], (L, vals[0].shape[1]))
    for i in range(1, len(vals)):
        out = jnp.where(seg_id == i, vals[i], out)
    return out


def _ticker(overlap, n_ticks):
    pending = list(overlap()) if overlap is not None else []
    share = -(-len(pending) // n_ticks) if pending else 0

    def tick(count=1):
        n = len(pending) if count is None else count * share
        for _ in range(min(n, len(pending))):
            pending.pop(0)()
    return tick


def _params(sem):
    return pltpu.CompilerParams(dimension_semantics=sem, vmem_limit_bytes=VMEM_LIMIT)


def _proj_body(x_ref, wt_ref, wgt_ref, o_ref, og_ref, xb_ref):
    @pl.when(pl.program_id(1) == 0)
    def _():
        xb = x_ref[...].astype(BF16)
        xb_ref[...] = xb
        og_ref[...] = lax.dot_general(xb, wgt_ref[...].astype(BF16), (((1,), (1,)), ((), ())),
                                      preferred_element_type=F32)

    o_ref[...] = lax.dot_general(xb_ref[...], wt_ref[...].astype(BF16), (((1,), (1,)), ((), ())),
                                 preferred_element_type=F32)


def _proj(x, wt_stack, layer, wgt, tm, tn):
    M, K = x.shape
    return pl.pallas_call(
        _proj_body,
        grid=(M // tm, PROJ_MAIN // tn),
        in_specs=[pl.BlockSpec((tm, K), lambda i, j: (i, 0), pipeline_mode=pl.Buffered(1)),
                  pl.BlockSpec((None, tn, K), lambda i, j: (layer, j, 0)),
                  pl.BlockSpec((LANES, K), lambda i, j: (0, 0))],
        out_specs=[pl.BlockSpec((tm, tn), lambda i, j: (i, j)),
                   pl.BlockSpec((tm, LANES), lambda i, j: (i, 0))],
        out_shape=[jax.ShapeDtypeStruct((M, PROJ_MAIN), F32),
                   jax.ShapeDtypeStruct((M, LANES), F32)],
        scratch_shapes=[pltpu.VMEM((tm, K), BF16)],
        compiler_params=_params(("parallel", "arbitrary")),
        name="in_proj",
    )(x, wt_stack, wgt)


def _row_window(total_rows, rows, row0, tm):
    rows = total_rows if rows is None else rows
    assert rows % tm == 0 and row0 % tm == 0
    return rows // tm, row0 // tm


def _into_operand(into):
    if into is None:
        return [], []
    return [into], [pl.BlockSpec(memory_space=pl.ANY)]


def _outproj_ln_body(y_ref, x_ref, w_ref, g_ref, b_ref, *rest, alpha):
    o_ref = rest[-1]
    mix = jnp.dot(y_ref[...].astype(BF16), w_ref[...], preferred_element_type=F32)
    o_ref[...] = _layer_norm(alpha * x_ref[...] + mix, g_ref[...], b_ref[...])


def _outproj_ln(y, x, w_stack, layer, g, b, tm, alpha, rows=None, row0=0, into=None):
    M, K = y.shape
    D = w_stack.shape[2]
    n, off = _row_window(M, rows, row0, tm)
    row = lambda i: (off + i, 0)
    fixed = lambda i: (0, 0)
    extra, extra_specs = _into_operand(into)
    return pl.pallas_call(
        functools.partial(_outproj_ln_body, alpha=alpha),
        grid=(n,),
        in_specs=[pl.BlockSpec((tm, K), row), pl.BlockSpec((tm, D), row),
                  pl.BlockSpec((None, K, D), lambda i: (layer, 0, 0), pipeline_mode=pl.Buffered(1)),
                  pl.BlockSpec((1, D), fixed), pl.BlockSpec((1, D), fixed)] + extra_specs,
        out_specs=pl.BlockSpec((tm, D), row),
        out_shape=jax.ShapeDtypeStruct((M, D), F32),
        input_output_aliases={5: 0} if extra else {},
        compiler_params=_params(("parallel",)),
        name="out_proj_ln",
    )(y, x, w_stack, g, b, *extra)


def _mlp_pre(f, h_ref, o_ref, hb_ref):
    @pl.when(f == 0)
    def _():
        hb_ref[...] = h_ref[...].astype(BF16)
        o_ref[...] = jnp.zeros_like(o_ref)


def _mlp_main(wu_ref, wd_ref, o_ref, hb_ref):
    up = jnp.dot(hb_ref[...], wu_ref[...].astype(BF16), preferred_element_type=F32)
    act = jnp.square(jnp.maximum(up, 0.0)).astype(BF16)
    o_ref[...] += jnp.dot(act, wd_ref[...].astype(BF16), preferred_element_type=F32)


def _mlp_pieces(wu_ref, wd_ref, o_ref, hb_ref):
    tm, D = hb_ref.shape
    tf = wu_ref.shape[1]
    rb = tm // MLP_ROW_BLOCKS
    acts = {}
    pieces = []
    for r in range(MLP_ROW_BLOCKS):
        rows = slice(r * rb, (r + 1) * rb)
        for n in range(tf // MXU_N):
            def up_piece(r=r, n=n, rows=rows):
                w = wu_ref[:, n * MXU_N:(n + 1) * MXU_N].astype(BF16)
                up = jnp.dot(hb_ref[rows, :], w, preferred_element_type=F32)
                acts[r, n] = jnp.square(jnp.maximum(up, 0.0)).astype(BF16)
            pieces.append(up_piece)
        for k in range(D // MXU_N):
            def down_piece(r=r, k=k, rows=rows):
                act = jnp.concatenate([acts[r, n] for n in range(tf // MXU_N)], axis=1)
                w = wd_ref[:, k * MXU_N:(k + 1) * MXU_N].astype(BF16)
                o_ref[rows, k * MXU_N:(k + 1) * MXU_N] += jnp.dot(act, w, preferred_element_type=F32)
            pieces.append(down_piece)
    return pieces


def _mlp_post(f, nf, h_ref, g_ref, b_ref, o_ref, alpha):
    @pl.when(f == nf - 1)
    def _():
        o_ref[...] = _layer_norm(alpha * h_ref[...] + o_ref[...], g_ref[...], b_ref[...])


def _mlp_ln_body(h_ref, wu_ref, wd_ref, g_ref, b_ref, *rest, alpha):
    o_ref, hb_ref = rest[-2:]
    f = pl.program_id(1)
    _mlp_pre(f, h_ref, o_ref, hb_ref)
    _mlp_main(wu_ref, wd_ref, o_ref, hb_ref)
    _mlp_post(f, pl.num_programs(1), h_ref, g_ref, b_ref, o_ref, alpha)


def _mlp_specs(D, tm, tf, layer, off):
    in_specs = [pl.BlockSpec((tm, D), lambda i, f: (off + i, 0), pipeline_mode=pl.Buffered(1)),
                pl.BlockSpec((None, D, tf), lambda i, f: (layer, 0, f)),
                pl.BlockSpec((None, tf, D), lambda i, f: (layer, f, 0)),
                pl.BlockSpec((1, D), lambda i, f: (0, 0)),
                pl.BlockSpec((1, D), lambda i, f: (0, 0))]
    out_spec = pl.BlockSpec((tm, D), lambda i, f: (off + i, 0), pipeline_mode=pl.Buffered(1))
    return in_specs, out_spec, [pltpu.VMEM((tm, D), BF16)]


def _mlp_ln(h, wu_stack, wd_stack, layer, g, b, tm, tf, alpha, rows=None, row0=0, into=None):
    M, D = h.shape
    FF = wu_stack.shape[2]
    n, off = _row_window(M, rows, row0, tm)
    in_specs, out_spec, scratch = _mlp_specs(D, tm, tf, layer, off)
    extra, extra_specs = _into_operand(into)
    return pl.pallas_call(
        functools.partial(_mlp_ln_body, alpha=alpha),
        grid=(n, FF // tf),
        in_specs=in_specs + extra_specs,
        out_specs=out_spec,
        out_shape=jax.ShapeDtypeStruct((M, D), F32),
        scratch_shapes=scratch,
        input_output_aliases={5: 0} if extra else {},
        compiler_params=_params(("parallel", "arbitrary")),
        name="mlp_ln",
    )(h, wu_stack, wd_stack, g, b, *extra)


def _ple_body(h_ref, p_ref, wg_ref, wp_ref, o_ref):
    h = h_ref[...]
    gate = _sigmoid(jnp.dot(h.astype(BF16), wg_ref[...], preferred_element_type=F32))
    emb = jnp.dot(p_ref[...].astype(BF16), wp_ref[...], preferred_element_type=F32)
    o_ref[...] = h + gate * emb


def _ple(h, p_stack, layer, wg_stack, wp_stack, tm):
    M, D = h.shape
    P = p_stack.shape[2]
    row = lambda i: (i, 0)
    return pl.pallas_call(
        _ple_body,
        grid=(M // tm,),
        in_specs=[pl.BlockSpec((tm, D), row),
                  pl.BlockSpec((None, tm, P), lambda i: (layer, i, 0)),
                  pl.BlockSpec((None, D, D), lambda i: (layer, 0, 0), pipeline_mode=pl.Buffered(1)),
                  pl.BlockSpec((None, P, D), lambda i: (layer, 0, 0), pipeline_mode=pl.Buffered(1))],
        out_specs=pl.BlockSpec((tm, D), row),
        out_shape=jax.ShapeDtypeStruct((M, D), F32),
        compiler_params=_params(("parallel",)),
        name="ple_gate",
    )(h, p_stack, wg_stack, wp_stack)


def _mixer_e_body(proj_ref, gate_ref, conv0_ref, s0_ref, h0_ref, wconv_ref, bconv_ref, alog_ref,
                  dtb_ref, nw_ref, wri_ref, br_ref, bi_ref, lam_ref,
                  y_ref, convn_ref, s_ref, hl_ref,
                  cbuf, a_scr, bx_scr, hseq, *, L, seg, t_valid, conv_win,
                  chunk=None, n_chunks=None, overlap=None):
    c = pl.program_id(1) if chunk is None else chunk
    last = (pl.num_programs(1) if n_chunks is None else n_chunks) - 1
    heads = range(HEADS)
    nseg = L // seg
    segs = range(nseg)
    stride = CONV_TAIL + seg

    packed = seg < SUBLANES

    @pl.when(c == 0)
    def _():
        if not packed:
            for s in segs:
                cbuf[s * stride:s * stride + CONV_TAIL, :] = conv0_ref[s * CONV_TAIL:(s + 1) * CONV_TAIL, :]
        s_ref[...] = s0_ref[...]
        hl_ref[...] = h0_ref[...]

    tick = _ticker(overlap, 28)
    same, tril, strict, eye = _seg_masks(L, seg)
    tril_f = jnp.where(tril, 1.0, 0.0).astype(F32)
    same_f = jnp.where(same, 1.0, 0.0).astype(F32)
    row_ok = (lax.broadcasted_iota(jnp.int32, (L, 1), 0) % seg) < t_valid
    s_old = [[s_ref[s, h] for h in heads] for s in segs]
    h_old = [hl_ref[s] for s in segs]

    if packed:
        u = proj_ref[:, 0:CONV_CH]
        hist = conv0_ref[...]
        rowm = lax.broadcasted_iota(jnp.int32, (L, 1), 0) % seg
        acc = bconv_ref[...] + u * wconv_ref[CONV_W - 1:CONV_W, :]
        for j in range(1, CONV_W):
            shifted = jnp.where(rowm >= j, pltpu.roll(u, j, axis=0), pltpu.roll(hist, (j - seg) % L, axis=0))
            acc = acc + shifted * wconv_ref[CONV_W - 1 - j:CONV_W - j, :]
        convn_ref[...] = u
    else:
        for s in segs:
            cbuf[s * stride + CONV_TAIL:(s + 1) * stride, :] = proj_ref[s * seg:(s + 1) * seg, 0:CONV_CH]
        full = cbuf[...]
        prev = pltpu.roll(full, 1, axis=0)
        pair_new = full * wconv_ref[3:4, :] + prev * wconv_ref[2:3, :]
        pair_old = full * wconv_ref[1:2, :] + prev * wconv_ref[0:1, :]
        conv_full = bconv_ref[...] + pair_new + pltpu.roll(pair_old, 2, axis=0)
        acc = jnp.concatenate([conv_full[s * stride + CONV_TAIL:(s + 1) * stride, :] for s in segs], axis=0)

    tick()
    gp = gate_ref[...]
    g_all = -jnp.exp(alog_ref[...]) * _softplus(gp + dtb_ref[...])
    beta_all = _sigmoid(gp)
    if t_valid < seg:
        g_all = jnp.where(row_ok, g_all, 0.0)
        beta_all = jnp.where(row_ok, beta_all, 0.0)
    gc_all = _dot_f32(tril_f, g_all)
    gl_all = _dot_f32(same_f, g_all)
    tick()

    q_l, k_l, kb_l, gcol_l, egc_l, y_l = [], [], [], [], [], []
    qa_l = [_silu(acc[:, h * DK:(h + 1) * DK]) for h in heads]
    tick()
    ka_l = [_silu(acc[:, HEADS * DK + h * DK:HEADS * DK + (h + 1) * DK]) for h in heads]
    tick()
    qn_l = [lax.rsqrt(jnp.sum(qa_l[h] * qa_l[h], axis=-1, keepdims=True) + L2_EPS) * (DK ** -0.5)
            for h in heads]
    kn_l = [lax.rsqrt(jnp.sum(ka_l[h] * ka_l[h], axis=-1, keepdims=True) + L2_EPS) for h in heads]
    tick()
    for h in heads:
        q = qa_l[h] * qn_l[h]
        k = ka_l[h] * kn_l[h]
        v = _silu(acc[:, 2 * HEADS * DK + h * A_DV:2 * HEADS * DK + (h + 1) * A_DV])
        gcol = gc_all[:, h:h + 1]
        bcol = beta_all[:, HEADS + h:HEADS + h + 1]
        egc = jnp.exp(gcol)
        kb = k * bcol
        q_l.append(q)
        k_l.append(k)
        kb_l.append(kb)
        gcol_l.append(gcol)
        egc_l.append(egc)
        y_l.append(jnp.concatenate([v * bcol, kb * egc], axis=1))
        if h % 2:
            tick()
    kq_k = [_mm_nt(jnp.concatenate([kb_l[h], q_l[h]], axis=0), k_l[h]) for h in heads]
    tick()
    p_l, qk_l = [], []
    for h in heads:
        grow = _col_to_row(gcol_l[h], eye)
        decay = jnp.where(tril, jnp.exp(jnp.where(tril, gcol_l[h] - grow, 0.0)), 0.0)
        p_l.append(jnp.where(strict, -(kq_k[h][0:L] * decay), 0.0))
        qk_l.append(kq_k[h][L:2 * L] * decay)
        if h % 4 == 3:
            tick()

    n_levels = max(1, math.ceil(math.log2(t_valid)))
    r_l = p_l
    if n_levels >= 2:
        p_l = [_mm(p_l[h], p_l[h]) for h in heads]
        tick()
        for _ in range(n_levels - 2):
            out = [_mm(jnp.concatenate([p_l[h], r_l[h]], axis=0), p_l[h]) for h in heads]
            r_l = [r_l[h] + p_l[h] + out[h][L:2 * L] for h in heads]
            p_l = [out[h][0:L] for h in heads]
            tick()
        out = [_mm(r_l[h], p_l[h]) for h in heads]
        r_l = [r_l[h] + p_l[h] + out[h] for h in heads]
        tick()
    y_l = [y_l[h] + _mm(r_l[h], y_l[h]) for h in heads]
    tick()

    R = max(seg, SUBLANES)
    qe_l = [q_l[h] * egc_l[h] for h in heads]
    wq_s = [[_mm(jnp.concatenate([_seg_tile(y_l[h][:, A_DV:A_DV + DK], s, seg),
                                  _seg_tile(qe_l[h], s, seg)], axis=0), s_old[s][h])
             for h in heads] for s in segs]
    v_new = [y_l[h][:, 0:A_DV] - _seg_join([wq_s[s][h][0:R] for s in segs], seg) for h in heads]
    tick()
    o_intra = [_mm(qk_l[h], v_new[h]) for h in heads]
    tick()
    kd_l = [k_l[h] * jnp.exp(gl_all[:, h:h + 1] - gcol_l[h]) for h in heads]
    s_upd = [[s_old[s][h] * jnp.exp(gl_all[s * seg:s * seg + 1, h:h + 1])
              + _mm_tn(_seg_only(_seg_tile(kd_l[h], s, seg), s, seg), _seg_tile(v_new[h], s, seg))
              for h in heads] for s in segs]

    bw = LRU_W // LRU_BLOCKS
    xr = acc[:, 3 * HEADS * DK:3 * HEADS * DK + LRU_W]
    sp = _softplus(-lam_ref[...])
    for n in range(LRU_BLOCKS):
        sl = slice(n * bw, (n + 1) * bw)
        xb = xr[:, sl]
        ri_pre = _mm(xb, wri_ref[n])
        r_pre = ri_pre[:, 0:bw] + br_ref[:, sl]
        i_pre = ri_pre[:, bw:2 * bw] + bi_ref[:, sl]
        log_a = -LRU_C * _sigmoid(r_pre) * sp[:, sl]
        a = jnp.exp(log_a)
        a_scr[:, sl] = a
        bx_scr[:, sl] = jnp.sqrt(-jnp.tanh(log_a) * (a * a + 1.0)) * _sigmoid(i_pre) * xb
        tick()

    for s in segs:
        hc = h_old[s]
        for t in range(seg):
            row = s * seg + t
            if t < t_valid:
                hc = a_scr[row:row + 1, :] * hc + bx_scr[row:row + 1, :]
            hseq[row:row + 1, :] = hc
        hl_ref[s] = hc
    tick()

    o_l = [_seg_join([wq_s[s][h][R:2 * R] for s in segs], seg) + o_intra[h] for h in heads]
    rms_l = [lax.rsqrt(jnp.mean(o_l[h] * o_l[h], axis=-1, keepdims=True) + RMS_EPS) for h in heads]
    tick()
    for h in heads:
        z = proj_ref[:, CONV_CH + h * A_DV:CONV_CH + (h + 1) * A_DV]
        y_ref[:, h * A_DV:(h + 1) * A_DV] = o_l[h] * rms_l[h] * nw_ref[...] * _silu(z)
    for s in segs:
        for h in heads:
            s_ref[s, h] = s_upd[s][h]
    gate = proj_ref[:, CONV_CH + HEADS * A_DV:CONV_CH + HEADS * A_DV + LRU_W]
    y_ref[:, HEADS * A_DV:HEADS * A_DV + LRU_W] = hseq[...] * _gelu_tanh(gate)
    tick(None)

    if not packed:
        @pl.when(c == last)
        def _():
            for s in segs:
                convn_ref[s * SUBLANES:(s + 1) * SUBLANES, :] = (
                    cbuf[s * stride + conv_win:s * stride + conv_win + SUBLANES, :])

        for s in segs:
            cbuf[s * stride:s * stride + CONV_TAIL, :] = cbuf[s * stride + seg:s * stride + seg + CONV_TAIL, :]


def _mixer_e_parts(proj, gates, conv0, s0, h0, wconv, bconv, alog, dtb, nw, wri, br, bi, lam,
                   *, seq0, nblk, nc, L, seg, t_valid, bc):
    nseg = L // seg
    assert nc == 1 or nseg == 1
    assert t_valid >= CONV_W - 1
    if seg < SUBLANES:
        assert nc == 1 and t_valid == seg and L % SUBLANES == 0
        hist_rows, conv_win, conv_off = L, 0, seg - (CONV_W - 1)
    else:
        conv_row = CONV_TAIL - (CONV_W - 1) + t_valid
        conv_win = (conv_row // SUBLANES) * SUBLANES
        conv_off = conv_row - conv_win
        assert conv_off + CONV_W - 1 <= SUBLANES
        hist_rows = nseg * SUBLANES
    mix = HEADS * A_DV + LRU_W

    def chunk(*g):
        b, c = bc(*g)
        return ((seq0 + b) * nc + c, 0)

    blk_in = lambda *g: seq0 + bc(*g)[0]
    blk_out = lambda *g: bc(*g)[0]
    fix2 = lambda *g: (0, 0)
    fix3 = lambda *g: (0, 0, 0)
    return dict(
        body=functools.partial(_mixer_e_body, L=L, seg=seg, t_valid=t_valid, conv_win=conv_win),
        operands=[proj, gates, conv0, s0, h0, wconv, bconv, alog, dtb, nw, wri, br, bi, lam],
        in_specs=[pl.BlockSpec((L, PROJ_MAIN), chunk),
                  pl.BlockSpec((L, LANES), chunk),
                  pl.BlockSpec((hist_rows, CONV_CH), lambda *g: (blk_in(*g), 0)),
                  pl.BlockSpec((nseg, HEADS, DK, A_DV), lambda *g: (blk_in(*g), 0, 0, 0)),
                  pl.BlockSpec((nseg, 1, LRU_W), lambda *g: (blk_in(*g), 0, 0)),
                  pl.BlockSpec((CONV_W, CONV_CH), fix2),
                  pl.BlockSpec((1, CONV_CH), fix2),
                  pl.BlockSpec((1, LANES), fix2),
                  pl.BlockSpec((1, LANES), fix2),
                  pl.BlockSpec((1, A_DV), fix2),
                  pl.BlockSpec((LRU_BLOCKS, LRU_W // LRU_BLOCKS, 2 * LRU_W // LRU_BLOCKS), fix3),
                  pl.BlockSpec((1, LRU_W), fix2),
                  pl.BlockSpec((1, LRU_W), fix2),
                  pl.BlockSpec((1, LRU_W), fix2)],
        out_specs=[pl.BlockSpec((L, mix), chunk),
                   pl.BlockSpec((hist_rows, CONV_CH), lambda *g: (blk_out(*g), 0)),
                   pl.BlockSpec((nseg, HEADS, DK, A_DV), lambda *g: (blk_out(*g), 0, 0, 0)),
                   pl.BlockSpec((nseg, 1, LRU_W), lambda *g: (blk_out(*g), 0, 0))],
        out_shape=[jax.ShapeDtypeStruct((proj.shape[0], mix), F32),
                   jax.ShapeDtypeStruct((nblk * hist_rows, CONV_CH), F32),
                   jax.ShapeDtypeStruct((nblk * nseg, HEADS, DK, A_DV), F32),
                   jax.ShapeDtypeStruct((nblk * nseg, 1, LRU_W), F32)],
        scratch_shapes=[pltpu.VMEM((nseg * (CONV_TAIL + seg), CONV_CH), F32),
                        pltpu.VMEM((L, LRU_W), F32),
                        pltpu.VMEM((L, LRU_W), F32),
                        pltpu.VMEM((L, LRU_W), F32)],
        name="delta_lru_mixer", conv_off=conv_off)


def _mixer_o_body(proj_ref, gate_ref, c0_ref, n0_ref, m0_ref, big_ref, bfg_ref, nw_ref,
                  y_ref, c_ref, n_ref, m_ref, *, L, seg, t_valid, chunk=None, n_chunks=None, overlap=None):
    ci = pl.program_id(1) if chunk is None else chunk
    heads = range(HEADS)
    nseg = L // seg
    segs = range(nseg)

    @pl.when(ci == 0)
    def _():
        c_ref[...] = c0_ref[...]
        n_ref[...] = n0_ref[...]
        m_ref[...] = m0_ref[...]

    tick = _ticker(overlap, 16)
    same, tril, _, eye = _seg_masks(L, seg)
    tril_f = jnp.where(tril, 1.0, 0.0).astype(F32)
    same_f = jnp.where(same, 1.0, 0.0).astype(F32)
    row_id = lax.broadcasted_iota(jnp.int32, (L, 1), 0)
    row_ok = (row_id % seg) < t_valid
    c_old = [[c_ref[s, h] for h in heads] for s in segs]
    n_old = [n_ref[s] for s in segs]
    m_old = [m_ref[s] for s in segs]

    gp = gate_ref[...]
    ig_all = gp + big_ref[...]
    logf_all = -_softplus(-(gp + bfg_ref[...]))
    if t_valid < seg:
        ig_all = jnp.where(row_ok, ig_all, NEG_BIG)
        logf_all = jnp.where(row_ok, logf_all, 0.0)
    bcum_all = _dot_f32(tril_f, logf_all)
    blast_all = _dot_f32(same_f, logf_all)
    mrow_all = _seg_rows(m_old, L, seg)
    tick()

    q_l = [proj_ref[:, h * DK:(h + 1) * DK] for h in heads]
    k_l = [proj_ref[:, HEADS * DK + h * DK:HEADS * DK + (h + 1) * DK] * (DK ** -0.5) for h in heads]
    v_l = [proj_ref[:, 2 * HEADS * DK + h * C_DV:2 * HEADS * DK + (h + 1) * C_DV] for h in heads]
    qk_l = [_mm_nt(q_l[h], k_l[h]) for h in heads]
    tick()
    qc_l = [_seg_join([_mm_nt(_seg_tile(q_l[h], s, seg), c_old[s][h]) for s in segs], seg) for h in heads]
    tick()

    s_l, e_l, mt_l, sc_l, ws_l, mnew_l = [], [], [], [], [], []
    for h in heads:
        bcol = bcum_all[:, HEADS + h:HEADS + h + 1]
        blast = blast_all[:, HEADS + h:HEADS + h + 1]
        icol = ig_all[:, h:h + 1]
        mcol = mrow_all[:, h:h + 1]
        brow = _col_to_row(bcol, eye)
        irow = _col_to_row(icol, eye)
        d_intra = jnp.where(tril, bcol - brow + irow, NEG_BIG)
        ge_mat = jnp.where(same, blast - brow + irow, NEG_BIG)
        ge_col = blast - bcol + icol
        inter = bcol + mcol
        m_t = jnp.maximum(inter, jnp.max(d_intra, axis=-1, keepdims=True))
        m_new = jnp.maximum(blast + mcol, jnp.max(ge_mat, axis=-1, keepdims=True))
        e_l.append(jnp.exp(inter - m_t))
        mt_l.append(m_t)
        s_l.append(qk_l[h] * jnp.exp(d_intra - m_t))
        sc_l.append(jnp.exp(blast + mcol - m_new))
        ws_l.append(jnp.exp(ge_col - m_new))
        mnew_l.append(m_new)
        tick()

    sv = [_mm(s_l[h], v_l[h]) for h in heads]
    tick()
    vw_l = [v_l[h] * ws_l[h] for h in heads]
    kw_l = [k_l[h] * ws_l[h] for h in heads]
    kv = [[_mm_tn(_seg_only(_seg_tile(vw_l[h], s, seg), s, seg), _seg_tile(k_l[h], s, seg)) for h in heads]
          for s in segs]

    nrow_l = [_seg_rows([n_old[s][h:h + 1, :] for s in segs], L, seg) for h in heads]
    den_l = [e_l[h] * jnp.sum(q_l[h] * nrow_l[h], axis=-1, keepdims=True)
             + jnp.sum(s_l[h], axis=-1, keepdims=True) for h in heads]
    inv_l = [1.0 / jnp.maximum(jnp.abs(den_l[h]), jnp.exp(-mt_l[h])) for h in heads]
    tick()
    hh_l = [(e_l[h] * qc_l[h] + sv[h]) * inv_l[h] for h in heads]
    rms_l = [lax.rsqrt(jnp.mean(hh_l[h] * hh_l[h], axis=-1, keepdims=True) + RMS_EPS) for h in heads]
    tick()
    for h in heads:
        o_pre = proj_ref[:, 2 * HEADS * DK + HEADS * C_DV + h * C_DV:
                         2 * HEADS * DK + HEADS * C_DV + (h + 1) * C_DV]
        y_ref[:, h * C_DV:(h + 1) * C_DV] = (hh_l[h] * rms_l[h] * nw_ref[:, h * C_DV:(h + 1) * C_DV]
                                             * _sigmoid(o_pre))
        if h % 4 == 3:
            tick()
    lane = lax.broadcasted_iota(jnp.int32, (1, LANES), 1)
    for s in segs:
        r0 = s * seg
        m_row = m_old[s]
        n_rows = []
        for h in heads:
            sc = sc_l[h][r0:r0 + 1, :]
            c_ref[s, h] = sc * c_old[s][h] + kv[s][h]
            n_rows.append(sc * n_old[s][h:h + 1, :]
                          + jnp.sum(_seg_only(_seg_tile(kw_l[h], s, seg), s, seg), axis=0, keepdims=True))
            m_row = jnp.where(lane == h, mnew_l[h][r0:r0 + 1, :], m_row)
        n_ref[s] = jnp.concatenate(n_rows, axis=0)
        m_ref[s] = m_row
    tick(None)


def _mixer_o_parts(proj, gates, c0, n0, m0, big, bfg, nw, *, seq0, nblk, nc, L, seg, t_valid, bc):
    nseg = L // seg
    assert nc == 1 or nseg == 1
    cv = HEADS * C_DV

    def chunk(*g):
        b, c = bc(*g)
        return ((seq0 + b) * nc + c, 0)

    blk_in = lambda *g: seq0 + bc(*g)[0]
    blk_out = lambda *g: bc(*g)[0]
    fix2 = lambda *g: (0, 0)
    return dict(
        body=functools.partial(_mixer_o_body, L=L, seg=seg, t_valid=t_valid),
        operands=[proj, gates, c0, n0, m0, big, bfg, nw],
        in_specs=[pl.BlockSpec((L, PROJ_MAIN), chunk),
                  pl.BlockSpec((L, LANES), chunk),
                  pl.BlockSpec((nseg, HEADS, C_DV, DK), lambda *g: (blk_in(*g), 0, 0, 0)),
                  pl.BlockSpec((nseg, HEADS, DK), lambda *g: (blk_in(*g), 0, 0)),
                  pl.BlockSpec((nseg, 1, LANES), lambda *g: (blk_in(*g), 0, 0)),
                  pl.BlockSpec((1, LANES), fix2),
                  pl.BlockSpec((1, LANES), fix2),
                  pl.BlockSpec((1, cv), fix2)],
        out_specs=[pl.BlockSpec((L, cv), chunk),
                   pl.BlockSpec((nseg, HEADS, C_DV, DK), lambda *g: (blk_out(*g), 0, 0, 0)),
                   pl.BlockSpec((nseg, HEADS, DK), lambda *g: (blk_out(*g), 0, 0)),
                   pl.BlockSpec((nseg, 1, LANES), lambda *g: (blk_out(*g), 0, 0))],
        out_shape=[jax.ShapeDtypeStruct((proj.shape[0], cv), F32),
                   jax.ShapeDtypeStruct((nblk * nseg, HEADS, C_DV, DK), F32),
                   jax.ShapeDtypeStruct((nblk * nseg, HEADS, DK), F32),
                   jax.ShapeDtypeStruct((nblk * nseg, 1, LANES), F32)],
        scratch_shapes=[],
        name="mlstm_mixer")


def _run_mixer(parts, nblk, nc, y_into=None):
    extra, extra_specs = _into_operand(y_into)
    n_in = len(parts['operands'])
    return pl.pallas_call(
        _drop_operand(parts['body'], n_in) if extra else parts['body'],
        grid=(nblk, nc),
        in_specs=parts['in_specs'] + extra_specs,
        out_specs=parts['out_specs'],
        out_shape=parts['out_shape'],
        scratch_shapes=parts['scratch_shapes'],
        input_output_aliases={n_in: 0} if extra else {},
        compiler_params=_params(("parallel", "arbitrary")),
        name=parts['name'],
    )(*parts['operands'], *extra)


def _drop_operand(body, index):
    def wrapped(*refs):
        return body(*refs[:index], *refs[index + 1:])
    return wrapped


def _fused_body(*refs, mixer_body, n_mi, n_mo, n_ms, n_into, alpha):
    mi = refs[:n_mi]
    h_ref, wu_ref, wd_ref, g_ref, b_ref = refs[n_mi:n_mi + 5]
    outs = refs[n_mi + 5 + n_into:]
    mo = outs[:n_mo]
    o_ref = outs[n_mo]
    ms = outs[n_mo + 1:n_mo + 1 + n_ms]
    hb_ref = outs[n_mo + 1 + n_ms]
    i, f = pl.program_id(0), pl.program_id(1)
    nf = pl.num_programs(1)
    _mlp_pre(f, h_ref, o_ref, hb_ref)
    mixer_body(*mi, *mo, *ms, chunk=i * nf + f, n_chunks=pl.num_programs(0) * nf,
               overlap=lambda: _mlp_pieces(wu_ref, wd_ref, o_ref, hb_ref))
    _mlp_post(f, nf, h_ref, g_ref, b_ref, o_ref, alpha)


def _run_mixer_mlp(parts, tiles, nf, h, wu_stack, wd_stack, layer, g, b, tm, tf, alpha,
                   mlp_row0, y_into, h2_into):
    M, D = h.shape
    off = mlp_row0 // tm
    mlp_in, mlp_out, mlp_scr = _mlp_specs(D, tm, tf, layer, off)
    n_mi, n_mo, n_ms = len(parts['operands']), len(parts['out_specs']), len(parts['scratch_shapes'])
    intos = [y_into] + ([h2_into] if h2_into is not None else [])
    aliases = {n_mi + 5: 0}
    if h2_into is not None:
        aliases[n_mi + 6] = n_mo
    outs = pl.pallas_call(
        functools.partial(_fused_body, mixer_body=parts['body'], n_mi=n_mi, n_mo=n_mo, n_ms=n_ms,
                          n_into=len(intos), alpha=alpha),
        grid=(tiles, nf),
        in_specs=parts['in_specs'] + mlp_in + [pl.BlockSpec(memory_space=pl.ANY)] * len(intos),
        out_specs=parts['out_specs'] + [mlp_out],
        out_shape=parts['out_shape'] + [jax.ShapeDtypeStruct((M, D), F32)],
        scratch_shapes=parts['scratch_shapes'] + mlp_scr,
        input_output_aliases=aliases,
        compiler_params=_params(("arbitrary", "arbitrary")),
        name=parts['name'] + "_mlp",
    )(*parts['operands'], h, wu_stack, wd_stack, g, b, *intos)
    return outs[:n_mo], outs[n_mo]


def _pad_lanes(v, offset=0):
    return jnp.pad(v.astype(F32), (offset, LANES - offset - v.shape[0]))[None, :]


def _gate_weight_t(wt):
    wg = wt[PROJ_MAIN:, :]
    return jnp.pad(wg, ((0, LANES - wg.shape[0]), (0, 0)))


def _row_tile(m, cap):
    t = cap
    while t >= SUBLANES:
        if m % t == 0:
            return t
        t //= 2
    raise ValueError(f"row count {m} is not a multiple of {SUBLANES}")


def _blocking(B, T):
    t_valid = math.gcd(T, CHUNK)
    nc = T // t_valid
    if SUBLANES % t_valid == 0 and nc == 1:
        seg = t_valid
    else:
        seg = -(-t_valid // SUBLANES) * SUBLANES
    assert nc == 1 or seg == t_valid
    nseg = 1
    if nc == 1:
        nseg = max(1, min(B, SEQS_PER_BLOCK))
        while B % nseg or (nseg * seg) % SUBLANES:
            nseg -= 1
        assert nseg >= 1
    return t_valid, seg, nseg * seg, B // nseg, nc


def _trunk(x, p, conv, delta, lru, mc, mn, mm, W, depth, alpha):
    B, T, D = x.shape
    t_valid, seg, L, nblk, nc = _blocking(B, T)
    Tp = nc * seg
    if Tp != T:
        x = jnp.pad(x, ((0, 0), (0, Tp - T), (0, 0)))
        p = jnp.pad(p, ((0, 0), (0, 0), (0, Tp - T), (0, 0)))
    M = B * Tp
    FF = W['w_up'].shape[2]
    tm_proj = _row_tile(M, TM_PROJ)
    tiles = max(1, Tp // TM_MLP)
    pipelined = B > 1 and nc > 1 and nc % tiles == 0 and Tp % tiles == 0 and FF % (nc // tiles) == 0
    if pipelined:
        nf = nc // tiles
        tm_mlp, tf = Tp // tiles, FF // nf
        pipelined = tf % LANES == 0 and tm_mlp % SUBLANES == 0
    if not pipelined:
        tm_mlp, tf = _row_tile(M, TM_MLP), MLP_TF
    tm = _row_tile(Tp if pipelined else M, TM_SMALL)
    x2 = x.reshape(M, D)
    p2 = p.reshape(depth, M, -1)
    blk = dict(nc=nc, L=L, seg=seg, t_valid=t_valid)
    hist = seg if seg < SUBLANES else CONV_TAIL
    conv_o, delta_o, lru_o, mc_o, mn_o, mm_o = [], [], [], [], [], []
    for layer in range(depth):
        j = layer // 2
        g1, b1 = W['ln1_g'][layer][None, :], W['ln1_b'][layer][None, :]
        g2, b2 = W['ln2_g'][layer][None, :], W['ln2_b'][layer][None, :]
        if layer % 2 == 0:
            proj, gates = _proj(x2, W['w_in_e'], j, W['wg_e'][j], tm_proj, PROJ_TN)
            conv0 = jnp.pad(conv[j], ((0, 0), (hist - (CONV_W - 1), 0), (0, 0)))
            make_parts = functools.partial(
                _mixer_e_parts, proj, gates, conv0.reshape(B * hist, CONV_CH), delta[j], lru[j][:, None, :],
                W['w_conv_e'][j], W['b_conv_e'][j][None, :],
                _pad_lanes(W['a_log_e'][j]), _pad_lanes(W['dt_bias_e'][j]),
                W['delta_norm_e'][j][None, :], W['lru_wri_e'][j], W['lru_br_e'][j][None, :],
                W['lru_bi_e'][j][None, :], W['lru_lambda_e'][j][None, :], **blk)
            w_out = W['w_out_e']
        else:
            proj, gates = _proj(x2, W['w_in_o'], j, W['wg_o'][j], tm_proj, PROJ_TN)
            m0 = jnp.pad(mm[j], ((0, 0), (0, LANES - HEADS)))[:, None, :]
            make_parts = functools.partial(
                _mixer_o_parts, proj, gates, mc[j], mn[j], m0,
                _pad_lanes(W['b_ig_o'][j]), _pad_lanes(W['b_fg_o'][j], HEADS),
                W['mlstm_norm_o'][j][None, :], **blk)
            w_out = W['w_out_o']
        if not pipelined:
            parts = make_parts(seq0=0, nblk=nblk, bc=lambda b, c: (b, c))
            outs = _run_mixer(parts, nblk, nc)
            states = outs[1:]
            h = _outproj_ln(outs[0], x2, w_out, j, g1, b1, tm, alpha)
            h = _mlp_ln(h, W['w_up'], W['w_down'], layer, g2, b2, tm_mlp, tf, alpha)
        else:
            y = h1 = h = None
            per_seq = []
            for b in range(B):
                if b == 0:
                    parts = make_parts(seq0=b, nblk=1, bc=lambda blk_id, c: (0, c))
                    outs = _run_mixer(parts, 1, nc)
                else:
                    parts = make_parts(seq0=b, nblk=1, bc=lambda i, f: (0, i * nf + f))
                    outs, h = _run_mixer_mlp(parts, tiles, nf, h1, W['w_up'], W['w_down'], layer, g2, b2,
                                             tm_mlp, tf, alpha, (b - 1) * Tp, y, h)
                y = outs[0]
                per_seq.append(outs[1:])
                h1 = _outproj_ln(y, x2, w_out, j, g1, b1, tm, alpha, rows=Tp, row0=b * Tp, into=h1)
            h = _mlp_ln(h1, W['w_up'], W['w_down'], layer, g2, b2, tm_mlp, tf, alpha,
                        rows=Tp, row0=(B - 1) * Tp, into=h)
            states = [jnp.concatenate([st[k] for st in per_seq], axis=0) for k in range(len(per_seq[0]))]
        if layer % 2 == 0:
            conv_off = parts['conv_off']
            conv_o.append(states[0].reshape(B, hist, CONV_CH)[:, conv_off:conv_off + CONV_W - 1])
            delta_o.append(states[1])
            lru_o.append(states[2][:, 0])
        else:
            mc_o.append(states[0])
            mn_o.append(states[1])
            mm_o.append(states[2][:, 0, :HEADS])
        x2 = _ple(h, p2, layer, W['w_ple_gate'], W['w_ple'], tm)
    return (x2.reshape(B, Tp, D)[:, :T], jnp.stack(conv_o), jnp.stack(delta_o), jnp.stack(lru_o),
            jnp.stack(mc_o), jnp.stack(mn_o), jnp.stack(mm_o))


def kernel(x_prompt, x_sample, p_prompt, p_sample, state_conv, state_delta, state_lru,
           state_mlstm_c, state_mlstm_n, state_mlstm_m, w_in_e, w_conv_e, b_conv_e, a_log_e,
           dt_bias_e, delta_norm_e, lru_wr_e, lru_br_e, lru_wi_e, lru_bi_e, lru_lambda_e, w_out_e,
           w_in_o, b_ig_o, b_fg_o, mlstm_norm_o, w_out_o, ln1_g, ln1_b, ln2_g, ln2_b, w_up, w_down,
           w_ple, w_ple_gate):
    depth = ln1_g.shape[0]
    n_even, n_odd = w_in_e.shape[0], w_in_o.shape[0]
    alpha = (2 * depth) ** 0.25
    wt_e = jnp.swapaxes(w_in_e, 1, 2)
    wt_o = jnp.swapaxes(w_in_o, 1, 2)
    W = dict(
        w_in_e=wt_e, w_in_o=wt_o,
        wg_e=[_gate_weight_t(wt_e[j]) for j in range(n_even)],
        wg_o=[_gate_weight_t(wt_o[j]) for j in range(n_odd)],
        w_out_e=w_out_e.astype(BF16), w_out_o=w_out_o.astype(BF16),
        w_up=w_up, w_down=w_down,
        w_ple=w_ple.astype(BF16), w_ple_gate=w_ple_gate.astype(BF16),
        w_conv_e=w_conv_e, b_conv_e=b_conv_e, a_log_e=a_log_e, dt_bias_e=dt_bias_e,
        delta_norm_e=delta_norm_e, lru_br_e=lru_br_e,
        lru_wri_e=jnp.concatenate([lru_wr_e, lru_wi_e], axis=-1).astype(BF16),
        lru_bi_e=lru_bi_e, lru_lambda_e=lru_lambda_e, b_ig_o=b_ig_o, b_fg_o=b_fg_o,
        mlstm_norm_o=mlstm_norm_o, ln1_g=ln1_g, ln1_b=ln1_b, ln2_g=ln2_g, ln2_b=ln2_b)
    bp = x_prompt.shape[0]
    zeros = lambda a: jnp.zeros((a.shape[0], bp) + a.shape[2:], F32)
    out_p = _trunk(x_prompt, p_prompt, zeros(state_conv), zeros(state_delta), zeros(state_lru),
                   zeros(state_mlstm_c), zeros(state_mlstm_n), zeros(state_mlstm_m), W, depth, alpha)
    out_s = _trunk(x_sample, p_sample, state_conv, state_delta, state_lru,
                   state_mlstm_c, state_mlstm_n, state_mlstm_m, W, depth, alpha)
    return (out_p[0], out_s[0]) + tuple(out_p[1:]) + tuple(out_s[1:])
```

```python
import functools
import math

import jax
import jax.numpy as jnp
from jax import lax
from jax.experimental import pallas as pl
from jax.experimental.pallas import tpu as pltpu

F32 = jnp.float32
BF16 = jnp.bfloat16
HI = lax.Precision.HIGHEST

LANES = 128
SUBLANES = 8
VMEM_LIMIT = 60 * 1024 * 1024

HEADS = 8
DK = 128
A_DV = 128
C_DV = 256
LRU_W = 1024
LRU_BLOCKS = 8
LRU_C = 8.0
CONV_W = 4
CONV_CH = 4096
CHUNK = 64
LN_EPS = 1e-5
RMS_EPS = 1e-6
L2_EPS = 1e-6
PROJ_MAIN = 6144
PROJ_TN = 512
MLP_TF = 512
TM_PROJ = 2048
TM_MLP = 1024
TM_SMALL = 512
MXU_N = 256
MLP_ROW_BLOCKS = 2
CONV_TAIL = SUBLANES
SEQS_PER_BLOCK = 8
NEG_BIG = -1e30


def _dot_f32(a, b):
    return lax.dot_general(a, b, (((1,), (0,)), ((), ())), precision=HI,
                           preferred_element_type=F32)


def _mm(a, b):
    return lax.dot_general(a.astype(BF16), b.astype(BF16), (((1,), (0,)), ((), ())),
                           preferred_element_type=F32)


def _mm_nt(a, b):
    return lax.dot_general(a.astype(BF16), b.astype(BF16), (((1,), (1,)), ((), ())),
                           preferred_element_type=F32)


def _mm_tn(a, b):
    return lax.dot_general(a.astype(BF16), b.astype(BF16), (((0,), (0,)), ((), ())),
                           preferred_element_type=F32)


def _sigmoid(x):
    return 1.0 / (1.0 + jnp.exp(-x))


def _silu(x):
    return x * _sigmoid(x)


def _softplus(x):
    return jnp.maximum(x, 0.0) + jnp.log1p(jnp.exp(-jnp.abs(x)))


def _gelu_tanh(x):
    return 0.5 * x * (1.0 + jnp.tanh(math.sqrt(2.0 / math.pi) * (x + 0.044715 * (x * x * x))))


def _layer_norm(x, g, b):
    mu = jnp.mean(x, axis=-1, keepdims=True)
    xc = x - mu
    var = jnp.mean(xc * xc, axis=-1, keepdims=True)
    return xc * lax.rsqrt(var + LN_EPS) * g + b


def _col_to_row(col, eye):
    return jnp.sum(jnp.where(eye, col, 0.0), axis=0, keepdims=True)


def _seg_masks(L, seg):
    ri = lax.broadcasted_iota(jnp.int32, (L, L), 0)
    ci = lax.broadcasted_iota(jnp.int32, (L, L), 1)
    same = (ri // seg) == (ci // seg)
    return same, same & (ri >= ci), same & (ri > ci), ri == ci


def _seg_tile(a, s, seg):
    if seg >= SUBLANES:
        return a[s * seg:(s + 1) * seg]
    t = (s * seg) // SUBLANES
    return a[t * SUBLANES:(t + 1) * SUBLANES]


def _seg_only(a_tile, s, seg):
    if seg >= SUBLANES:
        return a_tile
    row = lax.broadcasted_iota(jnp.int32, (SUBLANES, 1), 0)
    lo = (s * seg) % SUBLANES
    return jnp.where((row >= lo) & (row < lo + seg), a_tile, 0.0)


def _seg_join(parts, seg):
    if seg >= SUBLANES:
        return jnp.concatenate(parts, axis=0)
    per_tile = SUBLANES // seg
    row = lax.broadcasted_iota(jnp.int32, (SUBLANES, 1), 0)
    tiles = []
    for t in range(len(parts) // per_tile):
        tile = parts[t * per_tile]
        for i in range(1, per_tile):
            tile = jnp.where(row >= i * seg, parts[t * per_tile + i], tile)
        tiles.append(tile)
    return jnp.concatenate(tiles, axis=0)


def _seg_rows(vals, L, seg):
    if seg >= SUBLANES:
        return jnp.concatenate([jnp.broadcast_to(v, (seg, v.shape[1])) for v in vals], axis=0)
    seg_id = lax.broadcasted_iota(jnp.int32, (L, 1), 0) // seg
    out = jnp.broadcast_to(vals[0], (L, vals[0].shape[1]))
    for i in range(1, len(vals)):
        out = jnp.where(seg_id == i, vals[i], out)
    return out


def _ticker(overlap, n_ticks):
    pending = list(overlap()) if overlap is not None else []
    total = len(pending)
    state = [0, 0]

    def tick(count=1):
        state[0] += 0 if count is None else count
        target = total if count is None else min(total, (state[0] * total) // n_ticks)
        while state[1] < target:
            pending[state[1]]()
            state[1] += 1
    return tick


def _params(sem):
    return pltpu.CompilerParams(dimension_semantics=sem, vmem_limit_bytes=VMEM_LIMIT)


def _proj_body(x_ref, wt_ref, wgt_ref, o_ref, og_ref, xb_ref):
    @pl.when(pl.program_id(1) == 0)
    def _():
        xb = x_ref[...].astype(BF16)
        xb_ref[...] = xb
        og_ref[...] = lax.dot_general(xb, wgt_ref[...].astype(BF16), (((1,), (1,)), ((), ())),
                                      preferred_element_type=F32)

    o_ref[...] = lax.dot_general(xb_ref[...], wt_ref[...].astype(BF16), (((1,), (1,)), ((), ())),
                                 preferred_element_type=F32)


def _proj(x, wt_stack, layer, wgt, tm, tn):
    M, K = x.shape
    return pl.pallas_call(
        _proj_body,
        grid=(M // tm, PROJ_MAIN // tn),
        in_specs=[pl.BlockSpec((tm, K), lambda i, j: (i, 0), pipeline_mode=pl.Buffered(1)),
                  pl.BlockSpec((None, tn, K), lambda i, j: (layer, j, 0)),
                  pl.BlockSpec((LANES, K), lambda i, j: (0, 0))],
        out_specs=[pl.BlockSpec((tm, tn), lambda i, j: (i, j)),
                   pl.BlockSpec((tm, LANES), lambda i, j: (i, 0))],
        out_shape=[jax.ShapeDtypeStruct((M, PROJ_MAIN), F32),
                   jax.ShapeDtypeStruct((M, LANES), F32)],
        scratch_shapes=[pltpu.VMEM((tm, K), BF16)],
        compiler_params=_params(("parallel", "arbitrary")),
        name="in_proj",
    )(x, wt_stack, wgt)


def _row_window(total_rows, rows, row0, tm):
    assert rows % tm == 0 and row0 % tm == 0
    return rows // tm, row0 // tm


def _into_operand(into):
    if into is None:
        return [], []
    return [into], [pl.BlockSpec(memory_space=pl.ANY)]


def _outproj_ln_body(y_ref, x_ref, w_ref, g_ref, b_ref, o_ref, *, alpha):
    mix = jnp.dot(y_ref[...].astype(BF16), w_ref[...], preferred_element_type=F32)
    o_ref[...] = _layer_norm(alpha * x_ref[...] + mix, g_ref[...], b_ref[...])


def _outproj_ln(y, x, w_stack, layer, g, b, tm, alpha, x_row0=0):
    M, K = y.shape
    D = w_stack.shape[2]
    n, off = _row_window(M, M, x_row0, tm)
    row = lambda i: (i, 0)
    fixed = lambda i: (0, 0)
    return pl.pallas_call(
        functools.partial(_outproj_ln_body, alpha=alpha),
        grid=(n,),
        in_specs=[pl.BlockSpec((tm, K), row), pl.BlockSpec((tm, D), lambda i: (off + i, 0)),
                  pl.BlockSpec((None, K, D), lambda i: (layer, 0, 0), pipeline_mode=pl.Buffered(1)),
                  pl.BlockSpec((1, D), fixed), pl.BlockSpec((1, D), fixed)],
        out_specs=pl.BlockSpec((tm, D), row),
        out_shape=jax.ShapeDtypeStruct((M, D), F32),
        compiler_params=_params(("parallel",)),
        name="out_proj_ln",
    )(y, x, w_stack, g, b)


def _mlp_pre(f, h_ref, o_ref, hb_ref):
    @pl.when(f == 0)
    def _():
        hb_ref[...] = h_ref[...].astype(BF16)
        o_ref[...] = jnp.zeros_like(o_ref)


def _mlp_main(wu_ref, wd_ref, o_ref, hb_ref):
    up = jnp.dot(hb_ref[...], wu_ref[...].astype(BF16), preferred_element_type=F32)
    act = jnp.square(jnp.maximum(up, 0.0)).astype(BF16)
    o_ref[...] += jnp.dot(act, wd_ref[...].astype(BF16), preferred_element_type=F32)


def _mlp_pieces(wu_ref, wd_ref, o_ref, hb_ref):
    tm, D = hb_ref.shape
    tf = wu_ref.shape[1]
    rb = tm // MLP_ROW_BLOCKS
    acts = {}
    pieces = []
    for r in range(MLP_ROW_BLOCKS):
        rows = slice(r * rb, (r + 1) * rb)
        for n in range(tf // MXU_N):
            def up_piece(r=r, n=n, rows=rows):
                w = wu_ref[:, n * MXU_N:(n + 1) * MXU_N].astype(BF16)
                up = jnp.dot(hb_ref[rows, :], w, preferred_element_type=F32)
                acts[r, n] = jnp.square(jnp.maximum(up, 0.0)).astype(BF16)
            pieces.append(up_piece)
        for k in range(D // MXU_N):
            def down_piece(r=r, k=k, rows=rows):
                act = jnp.concatenate([acts[r, n] for n in range(tf // MXU_N)], axis=1)
                w = wd_ref[:, k * MXU_N:(k + 1) * MXU_N].astype(BF16)
                o_ref[rows, k * MXU_N:(k + 1) * MXU_N] += jnp.dot(act, w, preferred_element_type=F32)
            pieces.append(down_piece)
    return pieces


def _mlp_post(f, nf, h_ref, g_ref, b_ref, o_ref, alpha):
    @pl.when(f == nf - 1)
    def _():
        o_ref[...] = _layer_norm(alpha * h_ref[...] + o_ref[...], g_ref[...], b_ref[...])


def _mlp_ln_body(h_ref, wu_ref, wd_ref, g_ref, b_ref, *rest, alpha):
    o_ref, hb_ref = rest[-2:]
    f = pl.program_id(1)
    _mlp_pre(f, h_ref, o_ref, hb_ref)
    _mlp_main(wu_ref, wd_ref, o_ref, hb_ref)
    _mlp_post(f, pl.num_programs(1), h_ref, g_ref, b_ref, o_ref, alpha)


def _mlp_specs(D, tm, tf, layer, out_off):
    in_specs = [pl.BlockSpec((tm, D), lambda i, f: (i, 0), pipeline_mode=pl.Buffered(1)),
                pl.BlockSpec((None, D, tf), lambda i, f: (layer, 0, f)),
                pl.BlockSpec((None, tf, D), lambda i, f: (layer, f, 0)),
                pl.BlockSpec((1, D), lambda i, f: (0, 0)),
                pl.BlockSpec((1, D), lambda i, f: (0, 0))]
    out_spec = pl.BlockSpec((tm, D), lambda i, f: (out_off + i, 0), pipeline_mode=pl.Buffered(1))
    return in_specs, out_spec, [pltpu.VMEM((tm, D), BF16)]


def _mlp_ln(h, wu_stack, wd_stack, layer, g, b, tm, tf, alpha, out_row0=0, into=None):
    M, D = h.shape
    FF = wu_stack.shape[2]
    n, off = _row_window(M, M, out_row0, tm)
    in_specs, out_spec, scratch = _mlp_specs(D, tm, tf, layer, off)
    extra, extra_specs = _into_operand(into)
    return pl.pallas_call(
        functools.partial(_mlp_ln_body, alpha=alpha),
        grid=(n, FF // tf),
        in_specs=in_specs + extra_specs,
        out_specs=out_spec,
        out_shape=jax.ShapeDtypeStruct(((M if into is None else into.shape[0]), D), F32),
        scratch_shapes=scratch,
        input_output_aliases={5: 0} if extra else {},
        compiler_params=_params(("parallel", "arbitrary")),
        name="mlp_ln",
    )(h, wu_stack, wd_stack, g, b, *extra)


def _ple_body(h_ref, p_ref, wg_ref, wp_ref, o_ref):
    h = h_ref[...]
    gate = _sigmoid(jnp.dot(h.astype(BF16), wg_ref[...], preferred_element_type=F32))
    emb = jnp.dot(p_ref[...].astype(BF16), wp_ref[...], preferred_element_type=F32)
    o_ref[...] = h + gate * emb


def _ple(h, p_stack, layer, wg_stack, wp_stack, tm):
    M, D = h.shape
    P = p_stack.shape[2]
    row = lambda i: (i, 0)
    return pl.pallas_call(
        _ple_body,
        grid=(M // tm,),
        in_specs=[pl.BlockSpec((tm, D), row),
                  pl.BlockSpec((None, tm, P), lambda i: (layer, i, 0)),
                  pl.BlockSpec((None, D, D), lambda i: (layer, 0, 0), pipeline_mode=pl.Buffered(1)),
                  pl.BlockSpec((None, P, D), lambda i: (layer, 0, 0), pipeline_mode=pl.Buffered(1))],
        out_specs=pl.BlockSpec((tm, D), row),
        out_shape=jax.ShapeDtypeStruct((M, D), F32),
        compiler_params=_params(("parallel",)),
        name="ple_gate",
    )(h, p_stack, wg_stack, wp_stack)


def _mixer_e_body(proj_ref, gate_ref, conv0_ref, s0_ref, h0_ref, wconv_ref, bconv_ref, alog_ref,
                  dtb_ref, nw_ref, wri_ref, br_ref, bi_ref, lam_ref,
                  y_ref, convn_ref, s_ref, hl_ref,
                  cbuf, a_scr, bx_scr, hseq, *, L, seg, t_valid, conv_win,
                  chunk=None, n_chunks=None, overlap=None):
    c = pl.program_id(1) if chunk is None else chunk
    last = (pl.num_programs(1) if n_chunks is None else n_chunks) - 1
    heads = range(HEADS)
    nseg = L // seg
    segs = range(nseg)
    stride = CONV_TAIL + seg

    packed = seg < SUBLANES

    @pl.when(c == 0)
    def _():
        if not packed:
            for s in segs:
                cbuf[s * stride:s * stride + CONV_TAIL, :] = conv0_ref[s * CONV_TAIL:(s + 1) * CONV_TAIL, :]
        s_ref[...] = s0_ref[...]
        hl_ref[...] = h0_ref[...]

    tick = _ticker(overlap, 28)
    same, tril, strict, eye = _seg_masks(L, seg)
    tril_f = jnp.where(tril, 1.0, 0.0).astype(F32)
    same_f = jnp.where(same, 1.0, 0.0).astype(F32)
    row_ok = (lax.broadcasted_iota(jnp.int32, (L, 1), 0) % seg) < t_valid
    s_old = [[s_ref[s, h] for h in heads] for s in segs]
    h_old = [hl_ref[s] for s in segs]

    if packed:
        u = proj_ref[:, 0:CONV_CH]
        hist = conv0_ref[...]
        rowm = lax.broadcasted_iota(jnp.int32, (L, 1), 0) % seg
        acc = bconv_ref[...] + u * wconv_ref[CONV_W - 1:CONV_W, :]
        for j in range(1, CONV_W):
            shifted = jnp.where(rowm >= j, pltpu.roll(u, j, axis=0), pltpu.roll(hist, (j - seg) % L, axis=0))
            acc = acc + shifted * wconv_ref[CONV_W - 1 - j:CONV_W - j, :]
        convn_ref[...] = u
    else:
        for s in segs:
            cbuf[s * stride + CONV_TAIL:(s + 1) * stride, :] = proj_ref[s * seg:(s + 1) * seg, 0:CONV_CH]
        full = cbuf[...]
        prev = pltpu.roll(full, 1, axis=0)
        pair_new = full * wconv_ref[3:4, :] + prev * wconv_ref[2:3, :]
        pair_old = full * wconv_ref[1:2, :] + prev * wconv_ref[0:1, :]
        conv_full = bconv_ref[...] + pair_new + pltpu.roll(pair_old, 2, axis=0)
        acc = jnp.concatenate([conv_full[s * stride + CONV_TAIL:(s + 1) * stride, :] for s in segs], axis=0)

    tick()
    gp = gate_ref[...]
    g_all = -jnp.exp(alog_ref[...]) * _softplus(gp + dtb_ref[...])
    beta_all = _sigmoid(gp)
    if t_valid < seg:
        g_all = jnp.where(row_ok, g_all, 0.0)
        beta_all = jnp.where(row_ok, beta_all, 0.0)
    gc_all = _dot_f32(tril_f, g_all)
    gl_all = _dot_f32(same_f, g_all)
    tick()

    q_l, k_l, kb_l, gcol_l, egc_l, y_l = [], [], [], [], [], []
    qa_l = [_silu(acc[:, h * DK:(h + 1) * DK]) for h in heads]
    tick()
    ka_l = [_silu(acc[:, HEADS * DK + h * DK:HEADS * DK + (h + 1) * DK]) for h in heads]
    tick()
    qn_l = [lax.rsqrt(jnp.sum(qa_l[h] * qa_l[h], axis=-1, keepdims=True) + L2_EPS) * (DK ** -0.5)
            for h in heads]
    kn_l = [lax.rsqrt(jnp.sum(ka_l[h] * ka_l[h], axis=-1, keepdims=True) + L2_EPS) for h in heads]
    tick()
    for h in heads:
        q = qa_l[h] * qn_l[h]
        k = ka_l[h] * kn_l[h]
        v = _silu(acc[:, 2 * HEADS * DK + h * A_DV:2 * HEADS * DK + (h + 1) * A_DV])
        gcol = gc_all[:, h:h + 1]
        bcol = beta_all[:, HEADS + h:HEADS + h + 1]
        egc = jnp.exp(gcol)
        kb = k * bcol
        q_l.append(q)
        k_l.append(k)
        kb_l.append(kb)
        gcol_l.append(gcol)
        egc_l.append(egc)
        y_l.append(jnp.concatenate([v * bcol, kb * egc], axis=1))
        if h % 2:
            tick()
    kq_k = [_mm_nt(jnp.concatenate([kb_l[h], q_l[h]], axis=0), k_l[h]) for h in heads]
    tick()
    p_l, qk_l = [], []
    for h in heads:
        grow = _col_to_row(gcol_l[h], eye)
        decay = jnp.where(tril, jnp.exp(jnp.where(tril, gcol_l[h] - grow, 0.0)), 0.0)
        p_l.append(jnp.where(strict, -(kq_k[h][0:L] * decay), 0.0))
        qk_l.append(kq_k[h][L:2 * L] * decay)
        if h % 4 == 3:
            tick()

    n_levels = max(1, math.ceil(math.log2(t_valid)))
    r_l = p_l
    if n_levels >= 2:
        p_l = [_mm(p_l[h], p_l[h]) for h in heads]
        tick()
        for _ in range(n_levels - 2):
            out = [_mm(jnp.concatenate([p_l[h], r_l[h]], axis=0), p_l[h]) for h in heads]
            r_l = [r_l[h] + p_l[h] + out[h][L:2 * L] for h in heads]
            p_l = [out[h][0:L] for h in heads]
            tick()
        out = [_mm(r_l[h], p_l[h]) for h in heads]
        r_l = [r_l[h] + p_l[h] + out[h] for h in heads]
        tick()
    y_l = [y_l[h] + _mm(r_l[h], y_l[h]) for h in heads]
    tick()

    R = max(seg, SUBLANES)
    qe_l = [q_l[h] * egc_l[h] for h in heads]
    wq_s = [[_mm(jnp.concatenate([_seg_tile(y_l[h][:, A_DV:A_DV + DK], s, seg),
                                  _seg_tile(qe_l[h], s, seg)], axis=0), s_old[s][h])
             for h in heads] for s in segs]
    v_new = [y_l[h][:, 0:A_DV] - _seg_join([wq_s[s][h][0:R] for s in segs], seg) for h in heads]
    tick()
    o_intra = [_mm(qk_l[h], v_new[h]) for h in heads]
    tick()
    kd_l = [k_l[h] * jnp.exp(gl_all[:, h:h + 1] - gcol_l[h]) for h in heads]
    s_upd = [[s_old[s][h] * jnp.exp(gl_all[s * seg:s * seg + 1, h:h + 1])
              + _mm_tn(_seg_only(_seg_tile(kd_l[h], s, seg), s, seg), _seg_tile(v_new[h], s, seg))
              for h in heads] for s in segs]

    bw = LRU_W // LRU_BLOCKS
    xr = acc[:, 3 * HEADS * DK:3 * HEADS * DK + LRU_W]
    sp = _softplus(-lam_ref[...])
    for n in range(LRU_BLOCKS):
        sl = slice(n * bw, (n + 1) * bw)
        xb = xr[:, sl]
        ri_pre = _mm(xb, wri_ref[n])
        r_pre = ri_pre[:, 0:bw] + br_ref[:, sl]
        i_pre = ri_pre[:, bw:2 * bw] + bi_ref[:, sl]
        log_a = -LRU_C * _sigmoid(r_pre) * sp[:, sl]
        a = jnp.exp(log_a)
        a_scr[:, sl] = a
        bx_scr[:, sl] = jnp.sqrt(-jnp.tanh(log_a) * (a * a + 1.0)) * _sigmoid(i_pre) * xb
        tick()

    for s in segs:
        hc = h_old[s]
        for t in range(seg):
            row = s * seg + t
            if t < t_valid:
                hc = a_scr[row:row + 1, :] * hc + bx_scr[row:row + 1, :]
            hseq[row:row + 1, :] = hc
        hl_ref[s] = hc
    tick()

    o_l = [_seg_join([wq_s[s][h][R:2 * R] for s in segs], seg) + o_intra[h] for h in heads]
    rms_l = [lax.rsqrt(jnp.mean(o_l[h] * o_l[h], axis=-1, keepdims=True) + RMS_EPS) for h in heads]
    tick()
    for h in heads:
        z = proj_ref[:, CONV_CH + h * A_DV:CONV_CH + (h + 1) * A_DV]
        y_ref[:, h * A_DV:(h + 1) * A_DV] = o_l[h] * rms_l[h] * nw_ref[...] * _silu(z)
    for s in segs:
        for h in heads:
            s_ref[s, h] = s_upd[s][h]
    gate = proj_ref[:, CONV_CH + HEADS * A_DV:CONV_CH + HEADS * A_DV + LRU_W]
    y_ref[:, HEADS * A_DV:HEADS * A_DV + LRU_W] = hseq[...] * _gelu_tanh(gate)
    tick(None)

    if not packed:
        @pl.when(c == last)
        def _():
            for s in segs:
                convn_ref[s * SUBLANES:(s + 1) * SUBLANES, :] = (
                    cbuf[s * stride + conv_win:s * stride + conv_win + SUBLANES, :])

        for s in segs:
            cbuf[s * stride:s * stride + CONV_TAIL, :] = cbuf[s * stride + seg:s * stride + seg + CONV_TAIL, :]


def _mixer_e_parts(proj, gates, conv0, s0, h0, wconv, bconv, alog, dtb, nw, wri, br, bi, lam,
                   *, seq0, nblk, nc, L, seg, t_valid, bc):
    nseg = L // seg
    assert nc == 1 or nseg == 1
    assert t_valid >= CONV_W - 1
    if seg < SUBLANES:
        assert nc == 1 and t_valid == seg and L % SUBLANES == 0
        hist_rows, conv_win, conv_off = L, 0, seg - (CONV_W - 1)
    else:
        conv_row = CONV_TAIL - (CONV_W - 1) + t_valid
        conv_win = (conv_row // SUBLANES) * SUBLANES
        conv_off = conv_row - conv_win
        assert conv_off + CONV_W - 1 <= SUBLANES
        hist_rows = nseg * SUBLANES
    mix = HEADS * A_DV + LRU_W

    def chunk(*g):
        b, c = bc(*g)
        return ((seq0 + b) * nc + c, 0)

    def chunk_out(*g):
        b, c = bc(*g)
        return (b * nc + c, 0)

    blk_in = lambda *g: seq0 + bc(*g)[0]
    blk_out = lambda *g: bc(*g)[0]
    fix2 = lambda *g: (0, 0)
    fix3 = lambda *g: (0, 0, 0)
    return dict(
        body=functools.partial(_mixer_e_body, L=L, seg=seg, t_valid=t_valid, conv_win=conv_win),
        operands=[proj, gates, conv0, s0, h0, wconv, bconv, alog, dtb, nw, wri, br, bi, lam],
        in_specs=[pl.BlockSpec((L, PROJ_MAIN), chunk),
                  pl.BlockSpec((L, LANES), chunk),
                  pl.BlockSpec((hist_rows, CONV_CH), lambda *g: (blk_in(*g), 0)),
                  pl.BlockSpec((nseg, HEADS, DK, A_DV), lambda *g: (blk_in(*g), 0, 0, 0)),
                  pl.BlockSpec((nseg, 1, LRU_W), lambda *g: (blk_in(*g), 0, 0)),
                  pl.BlockSpec((CONV_W, CONV_CH), fix2),
                  pl.BlockSpec((1, CONV_CH), fix2),
                  pl.BlockSpec((1, LANES), fix2),
                  pl.BlockSpec((1, LANES), fix2),
                  pl.BlockSpec((1, A_DV), fix2),
                  pl.BlockSpec((LRU_BLOCKS, LRU_W // LRU_BLOCKS, 2 * LRU_W // LRU_BLOCKS), fix3),
                  pl.BlockSpec((1, LRU_W), fix2),
                  pl.BlockSpec((1, LRU_W), fix2),
                  pl.BlockSpec((1, LRU_W), fix2)],
        out_specs=[pl.BlockSpec((L, mix), chunk_out),
                   pl.BlockSpec((hist_rows, CONV_CH), lambda *g: (blk_out(*g), 0)),
                   pl.BlockSpec((nseg, HEADS, DK, A_DV), lambda *g: (blk_out(*g), 0, 0, 0)),
                   pl.BlockSpec((nseg, 1, LRU_W), lambda *g: (blk_out(*g), 0, 0))],
        out_shape=[jax.ShapeDtypeStruct((nblk * nc * L, mix), F32),
                   jax.ShapeDtypeStruct((nblk * hist_rows, CONV_CH), F32),
                   jax.ShapeDtypeStruct((nblk * nseg, HEADS, DK, A_DV), F32),
                   jax.ShapeDtypeStruct((nblk * nseg, 1, LRU_W), F32)],
        scratch_shapes=[pltpu.VMEM((nseg * (CONV_TAIL + seg), CONV_CH), F32),
                        pltpu.VMEM((L, LRU_W), F32),
                        pltpu.VMEM((L, LRU_W), F32),
                        pltpu.VMEM((L, LRU_W), F32)],
        name="delta_lru_mixer", conv_off=conv_off)


def _mixer_o_body(proj_ref, gate_ref, c0_ref, n0_ref, m0_ref, big_ref, bfg_ref, nw_ref,
                  y_ref, c_ref, n_ref, m_ref, *, L, seg, t_valid, chunk=None, n_chunks=None, overlap=None):
    ci = pl.program_id(1) if chunk is None else chunk
    heads = range(HEADS)
    nseg = L // seg
    segs = range(nseg)

    @pl.when(ci == 0)
    def _():
        c_ref[...] = c0_ref[...]
        n_ref[...] = n0_ref[...]
        m_ref[...] = m0_ref[...]

    tick = _ticker(overlap, 16)
    same, tril, _, eye = _seg_masks(L, seg)
    tril_f = jnp.where(tril, 1.0, 0.0).astype(F32)
    same_f = jnp.where(same, 1.0, 0.0).astype(F32)
    row_id = lax.broadcasted_iota(jnp.int32, (L, 1), 0)
    row_ok = (row_id % seg) < t_valid
    c_old = [[c_ref[s, h] for h in heads] for s in segs]
    n_old = [n_ref[s] for s in segs]
    m_old = [m_ref[s] for s in segs]

    gp = gate_ref[...]
    ig_all = gp + big_ref[...]
    logf_all = -_softplus(-(gp + bfg_ref[...]))
    if t_valid < seg:
        ig_all = jnp.where(row_ok, ig_all, NEG_BIG)
        logf_all = jnp.where(row_ok, logf_all, 0.0)
    bcum_all = _dot_f32(tril_f, logf_all)
    blast_all = _dot_f32(same_f, logf_all)
    mrow_all = _seg_rows(m_old, L, seg)
    tick()

    q_l = [proj_ref[:, h * DK:(h + 1) * DK] for h in heads]
    k_l = [proj_ref[:, HEADS * DK + h * DK:HEADS * DK + (h + 1) * DK] * (DK ** -0.5) for h in heads]
    v_l = [proj_ref[:, 2 * HEADS * DK + h * C_DV:2 * HEADS * DK + (h + 1) * C_DV] for h in heads]
    qk_l = [_mm_nt(q_l[h], k_l[h]) for h in heads]
    tick()
    qc_l = [_seg_join([_mm_nt(_seg_tile(q_l[h], s, seg), c_old[s][h]) for s in segs], seg) for h in heads]
    tick()

    s_l, e_l, mt_l, sc_l, ws_l, mnew_l = [], [], [], [], [], []
    for h in heads:
        bcol = bcum_all[:, HEADS + h:HEADS + h + 1]
        blast = blast_all[:, HEADS + h:HEADS + h + 1]
        icol = ig_all[:, h:h + 1]
        mcol = mrow_all[:, h:h + 1]
        brow = _col_to_row(bcol, eye)
        irow = _col_to_row(icol, eye)
        d_intra = jnp.where(tril, bcol - brow + irow, NEG_BIG)
        ge_mat = jnp.where(same, blast - brow + irow, NEG_BIG)
        ge_col = blast - bcol + icol
        inter = bcol + mcol
        m_t = jnp.maximum(inter, jnp.max(d_intra, axis=-1, keepdims=True))
        m_new = jnp.maximum(blast + mcol, jnp.max(ge_mat, axis=-1, keepdims=True))
        e_l.append(jnp.exp(inter - m_t))
        mt_l.append(m_t)
        s_l.append(qk_l[h] * jnp.exp(d_intra - m_t))
        sc_l.append(jnp.exp(blast + mcol - m_new))
        ws_l.append(jnp.exp(ge_col - m_new))
        mnew_l.append(m_new)
        tick()

    sv = [_mm(s_l[h], v_l[h]) for h in heads]
    tick()
    vw_l = [v_l[h] * ws_l[h] for h in heads]
    kw_l = [k_l[h] * ws_l[h] for h in heads]
    kv = [[_mm_tn(_seg_only(_seg_tile(vw_l[h], s, seg), s, seg), _seg_tile(k_l[h], s, seg)) for h in heads]
          for s in segs]

    nrow_l = [_seg_rows([n_old[s][h:h + 1, :] for s in segs], L, seg) for h in heads]
    den_l = [e_l[h] * jnp.sum(q_l[h] * nrow_l[h], axis=-1, keepdims=True)
             + jnp.sum(s_l[h], axis=-1, keepdims=True) for h in heads]
    inv_l = [1.0 / jnp.maximum(jnp.abs(den_l[h]), jnp.exp(-mt_l[h])) for h in heads]
    tick()
    hh_l = [(e_l[h] * qc_l[h] + sv[h]) * inv_l[h] for h in heads]
    rms_l = [lax.rsqrt(jnp.mean(hh_l[h] * hh_l[h], axis=-1, keepdims=True) + RMS_EPS) for h in heads]
    tick()
    for h in heads:
        o_pre = proj_ref[:, 2 * HEADS * DK + HEADS * C_DV + h * C_DV:
                         2 * HEADS * DK + HEADS * C_DV + (h + 1) * C_DV]
        y_ref[:, h * C_DV:(h + 1) * C_DV] = (hh_l[h] * rms_l[h] * nw_ref[:, h * C_DV:(h + 1) * C_DV]
                                             * _sigmoid(o_pre))
        if h % 4 == 3:
            tick()
    lane = lax.broadcasted_iota(jnp.int32, (1, LANES), 1)
    for s in segs:
        r0 = s * seg
        m_row = m_old[s]
        n_rows = []
        for h in heads:
            sc = sc_l[h][r0:r0 + 1, :]
            c_ref[s, h] = sc * c_old[s][h] + kv[s][h]
            n_rows.append(sc * n_old[s][h:h + 1, :]
                          + jnp.sum(_seg_only(_seg_tile(kw_l[h], s, seg), s, seg), axis=0, keepdims=True))
            m_row = jnp.where(lane == h, mnew_l[h][r0:r0 + 1, :], m_row)
        n_ref[s] = jnp.concatenate(n_rows, axis=0)
        m_ref[s] = m_row
    tick(None)


def _mixer_o_parts(proj, gates, c0, n0, m0, big, bfg, nw, *, seq0, nblk, nc, L, seg, t_valid, bc):
    nseg = L // seg
    assert nc == 1 or nseg == 1
    cv = HEADS * C_DV

    def chunk(*g):
        b, c = bc(*g)
        return ((seq0 + b) * nc + c, 0)

    def chunk_out(*g):
        b, c = bc(*g)
        return (b * nc + c, 0)

    blk_in = lambda *g: seq0 + bc(*g)[0]
    blk_out = lambda *g: bc(*g)[0]
    fix2 = lambda *g: (0, 0)
    return dict(
        body=functools.partial(_mixer_o_body, L=L, seg=seg, t_valid=t_valid),
        operands=[proj, gates, c0, n0, m0, big, bfg, nw],
        in_specs=[pl.BlockSpec((L, PROJ_MAIN), chunk),
                  pl.BlockSpec((L, LANES), chunk),
                  pl.BlockSpec((nseg, HEADS, C_DV, DK), lambda *g: (blk_in(*g), 0, 0, 0)),
                  pl.BlockSpec((nseg, HEADS, DK), lambda *g: (blk_in(*g), 0, 0)),
                  pl.BlockSpec((nseg, 1, LANES), lambda *g: (blk_in(*g), 0, 0)),
                  pl.BlockSpec((1, LANES), fix2),
                  pl.BlockSpec((1, LANES), fix2),
                  pl.BlockSpec((1, cv), fix2)],
        out_specs=[pl.BlockSpec((L, cv), chunk_out),
                   pl.BlockSpec((nseg, HEADS, C_DV, DK), lambda *g: (blk_out(*g), 0, 0, 0)),
                   pl.BlockSpec((nseg, HEADS, DK), lambda *g: (blk_out(*g), 0, 0)),
                   pl.BlockSpec((nseg, 1, LANES), lambda *g: (blk_out(*g), 0, 0))],
        out_shape=[jax.ShapeDtypeStruct((nblk * nc * L, cv), F32),
                   jax.ShapeDtypeStruct((nblk * nseg, HEADS, C_DV, DK), F32),
                   jax.ShapeDtypeStruct((nblk * nseg, HEADS, DK), F32),
                   jax.ShapeDtypeStruct((nblk * nseg, 1, LANES), F32)],
        scratch_shapes=[],
        name="mlstm_mixer")


def _run_mixer(parts, nblk, nc):
    return pl.pallas_call(
        parts['body'],
        grid=(nblk, nc),
        in_specs=parts['in_specs'],
        out_specs=parts['out_specs'],
        out_shape=parts['out_shape'],
        scratch_shapes=parts['scratch_shapes'],
        compiler_params=_params(("parallel", "arbitrary")),
        name=parts['name'],
    )(*parts['operands'])


def _fused_body(*refs, mixer_body, n_mi, n_mo, n_ms, alpha):
    mi = refs[:n_mi]
    h_ref, wu_ref, wd_ref, g_ref, b_ref = refs[n_mi:n_mi + 5]
    outs = refs[n_mi + 6:]
    mo = outs[:n_mo]
    o_ref = outs[n_mo]
    ms = outs[n_mo + 1:n_mo + 1 + n_ms]
    hb_ref = outs[n_mo + 1 + n_ms]
    i, f = pl.program_id(0), pl.program_id(1)
    nf = pl.num_programs(1)
    _mlp_pre(f, h_ref, o_ref, hb_ref)
    mixer_body(*mi, *mo, *ms, chunk=i * nf + f, n_chunks=pl.num_programs(0) * nf,
               overlap=lambda: _mlp_pieces(wu_ref, wd_ref, o_ref, hb_ref))
    _mlp_post(f, nf, h_ref, g_ref, b_ref, o_ref, alpha)


def _run_mixer_mlp(parts, tiles, nf, h, wu_stack, wd_stack, layer, g, b, tm, tf, alpha, out_row0, h2_into):
    M, D = h.shape
    assert M == tiles * tm
    mlp_in, mlp_out, mlp_scr = _mlp_specs(D, tm, tf, layer, out_row0 // tm)
    n_mi, n_mo, n_ms = len(parts['operands']), len(parts['out_specs']), len(parts['scratch_shapes'])
    outs = pl.pallas_call(
        functools.partial(_fused_body, mixer_body=parts['body'], n_mi=n_mi, n_mo=n_mo, n_ms=n_ms, alpha=alpha),
        grid=(tiles, nf),
        in_specs=parts['in_specs'] + mlp_in + [pl.BlockSpec(memory_space=pl.ANY)],
        out_specs=parts['out_specs'] + [mlp_out],
        out_shape=parts['out_shape'] + [jax.ShapeDtypeStruct(h2_into.shape, F32)],
        scratch_shapes=parts['scratch_shapes'] + mlp_scr,
        input_output_aliases={n_mi + 5: n_mo},
        compiler_params=_params(("arbitrary", "arbitrary")),
        name=parts['name'] + "_mlp",
    )(*parts['operands'], h, wu_stack, wd_stack, g, b, h2_into)
    return outs[:n_mo], outs[n_mo]


def _pad_lanes(v, offset=0):
    return jnp.pad(v.astype(F32), (offset, LANES - offset - v.shape[0]))[None, :]


def _gate_weight_t(wt):
    wg = wt[PROJ_MAIN:, :]
    return jnp.pad(wg, ((0, LANES - wg.shape[0]), (0, 0)))


def _row_tile(m, cap):
    t = cap
    while t >= SUBLANES:
        if m % t == 0:
            return t
        t //= 2
    raise ValueError(f"row count {m} is not a multiple of {SUBLANES}")


def _blocking(B, T):
    t_valid = math.gcd(T, CHUNK)
    nc = T // t_valid
    if SUBLANES % t_valid == 0 and nc == 1:
        seg = t_valid
    else:
        seg = -(-t_valid // SUBLANES) * SUBLANES
    assert nc == 1 or seg == t_valid
    nseg = 1
    if nc == 1:
        nseg = max(1, min(B, SEQS_PER_BLOCK))
        while B % nseg or (nseg * seg) % SUBLANES:
            nseg -= 1
        assert nseg >= 1
    return t_valid, seg, nseg * seg, B // nseg, nc


def _trunk(x, p, conv, delta, lru, mc, mn, mm, W, depth, alpha):
    B, T, D = x.shape
    t_valid, seg, L, nblk, nc = _blocking(B, T)
    Tp = nc * seg
    if Tp != T:
        x = jnp.pad(x, ((0, 0), (0, Tp - T), (0, 0)))
        p = jnp.pad(p, ((0, 0), (0, 0), (0, Tp - T), (0, 0)))
    M = B * Tp
    FF = W['w_up'].shape[2]
    tm_proj = _row_tile(M, TM_PROJ)
    tiles = max(1, Tp // TM_MLP)
    pipelined = B > 1 and nc > 1 and nc % tiles == 0 and Tp % tiles == 0 and FF % (nc // tiles) == 0
    if pipelined:
        nf = nc // tiles
        tm_mlp, tf = Tp // tiles, FF // nf
        pipelined = tf % LANES == 0 and tm_mlp % SUBLANES == 0
    if not pipelined:
        tm_mlp, tf = _row_tile(M, TM_MLP), MLP_TF
    tm = _row_tile(Tp if pipelined else M, TM_SMALL)
    x2 = x.reshape(M, D)
    p2 = p.reshape(depth, M, -1)
    blk = dict(nc=nc, L=L, seg=seg, t_valid=t_valid)
    hist = seg if seg < SUBLANES else CONV_TAIL
    conv_o, delta_o, lru_o, mc_o, mn_o, mm_o = [], [], [], [], [], []
    spare = None
    for layer in range(depth):
        j = layer // 2
        g1, b1 = W['ln1_g'][layer][None, :], W['ln1_b'][layer][None, :]
        g2, b2 = W['ln2_g'][layer][None, :], W['ln2_b'][layer][None, :]
        if layer % 2 == 0:
            proj, gates = _proj(x2, W['w_in_e'], j, W['wg_e'][j], tm_proj, PROJ_TN)
            conv0 = jnp.pad(conv[j], ((0, 0), (hist - (CONV_W - 1), 0), (0, 0)))
            make_parts = functools.partial(
                _mixer_e_parts, proj, gates, conv0.reshape(B * hist, CONV_CH), delta[j], lru[j][:, None, :],
                W['w_conv_e'][j], W['b_conv_e'][j][None, :],
                _pad_lanes(W['a_log_e'][j]), _pad_lanes(W['dt_bias_e'][j]),
                W['delta_norm_e'][j][None, :], W['lru_wri_e'][j], W['lru_br_e'][j][None, :],
                W['lru_bi_e'][j][None, :], W['lru_lambda_e'][j][None, :], **blk)
            w_out = W['w_out_e']
        else:
            proj, gates = _proj(x2, W['w_in_o'], j, W['wg_o'][j], tm_proj, PROJ_TN)
            m0 = jnp.pad(mm[j], ((0, 0), (0, LANES - HEADS)))[:, None, :]
            make_parts = functools.partial(
                _mixer_o_parts, proj, gates, mc[j], mn[j], m0,
                _pad_lanes(W['b_ig_o'][j]), _pad_lanes(W['b_fg_o'][j], HEADS),
                W['mlstm_norm_o'][j][None, :], **blk)
            w_out = W['w_out_o']
        if not pipelined:
            parts = make_parts(seq0=0, nblk=nblk, bc=lambda b, c: (b, c))
            outs = _run_mixer(parts, nblk, nc)
            states = outs[1:]
            h = _outproj_ln(outs[0], x2, w_out, j, g1, b1, tm, alpha)
            h = _mlp_ln(h, W['w_up'], W['w_down'], layer, g2, b2, tm_mlp, tf, alpha)
        else:
            h = jnp.zeros((M, D), F32) if spare is None else spare
            h1 = None
            per_seq = []
            for b in range(B):
                if b == 0:
                    parts = make_parts(seq0=b, nblk=1, bc=lambda blk_id, c: (0, c))
                    outs = _run_mixer(parts, 1, nc)
                else:
                    parts = make_parts(seq0=b, nblk=1, bc=lambda i, f: (0, i * nf + f))
                    outs, h = _run_mixer_mlp(parts, tiles, nf, h1, W['w_up'], W['w_down'], layer, g2, b2,
                                             tm_mlp, tf, alpha, (b - 1) * Tp, h)
                per_seq.append(outs[1:])
                h1 = _outproj_ln(outs[0], x2, w_out, j, g1, b1, tm, alpha, x_row0=b * Tp)
            h = _mlp_ln(h1, W['w_up'], W['w_down'], layer, g2, b2, tm_mlp, tf, alpha,
                        out_row0=(B - 1) * Tp, into=h)
            states = [jnp.concatenate([st[k] for st in per_seq], axis=0) for k in range(len(per_seq[0]))]
        if layer % 2 == 0:
            conv_off = parts['conv_off']
            conv_o.append(states[0].reshape(B, hist, CONV_CH)[:, conv_off:conv_off + CONV_W - 1])
            delta_o.append(states[1])
            lru_o.append(states[2][:, 0])
        else:
            mc_o.append(states[0])
            mn_o.append(states[1])
            mm_o.append(states[2][:, 0, :HEADS])
        x2 = _ple(h, p2, layer, W['w_ple_gate'], W['w_ple'], tm)
        spare = h
    return (x2.reshape(B, Tp, D)[:, :T], jnp.stack(conv_o), jnp.stack(delta_o), jnp.stack(lru_o),
            jnp.stack(mc_o), jnp.stack(mn_o), jnp.stack(mm_o))


def kernel(x_prompt, x_sample, p_prompt, p_sample, state_conv, state_delta, state_lru,
           state_mlstm_c, state_mlstm_n, state_mlstm_m, w_in_e, w_conv_e, b_conv_e, a_log_e,
           dt_bias_e, delta_norm_e, lru_wr_e, lru_br_e, lru_wi_e, lru_bi_e, lru_lambda_e, w_out_e,
           w_in_o, b_ig_o, b_fg_o, mlstm_norm_o, w_out_o, ln1_g, ln1_b, ln2_g, ln2_b, w_up, w_down,
           w_ple, w_ple_gate):
    depth = ln1_g.shape[0]
    n_even, n_odd = w_in_e.shape[0], w_in_o.shape[0]
    alpha = (2 * depth) ** 0.25
    wt_e = jnp.swapaxes(w_in_e, 1, 2)
    wt_o = jnp.swapaxes(w_in_o, 1, 2)
    W = dict(
        w_in_e=wt_e, w_in_o=wt_o,
        wg_e=[_gate_weight_t(wt_e[j]) for j in range(n_even)],
        wg_o=[_gate_weight_t(wt_o[j]) for j in range(n_odd)],
        w_out_e=w_out_e.astype(BF16), w_out_o=w_out_o.astype(BF16),
        w_up=w_up, w_down=w_down,
        w_ple=w_ple.astype(BF16), w_ple_gate=w_ple_gate.astype(BF16),
        w_conv_e=w_conv_e, b_conv_e=b_conv_e, a_log_e=a_log_e, dt_bias_e=dt_bias_e,
        delta_norm_e=delta_norm_e, lru_br_e=lru_br_e,
        lru_wri_e=jnp.concatenate([lru_wr_e, lru_wi_e], axis=-1).astype(BF16),
        lru_bi_e=lru_bi_e, lru_lambda_e=lru_lambda_e, b_ig_o=b_ig_o, b_fg_o=b_fg_o,
        mlstm_norm_o=mlstm_norm_o, ln1_g=ln1_g, ln1_b=ln1_b, ln2_g=ln2_g, ln2_b=ln2_b)
    bp = x_prompt.shape[0]
    zeros = lambda a: jnp.zeros((a.shape[0], bp) + a.shape[2:], F32)
    out_p = _trunk(x_prompt, p_prompt, zeros(state_conv), zeros(state_delta), zeros(state_lru),
                   zeros(state_mlstm_c), zeros(state_mlstm_n), zeros(state_mlstm_m), W, depth, alpha)
    out_s = _trunk(x_sample, p_sample, state_conv, state_delta, state_lru,
                   state_mlstm_c, state_mlstm_n, state_mlstm_m, W, depth, alpha)
    return (out_p[0], out_s[0]) + tuple(out_p[1:]) + tuple(out_s[1:])
```

```python
import functools
import math

import jax
import jax.numpy as jnp
from jax import lax
from jax.experimental import pallas as pl
from jax.experimental.pallas import tpu as pltpu

F32 = jnp.float32
BF16 = jnp.bfloat16
HI = lax.Precision.HIGHEST

LANES = 128
SUBLANES = 8
VMEM_LIMIT = 60 * 1024 * 1024

HEADS = 8
DK = 128
A_DV = 128
C_DV = 256
LRU_W = 1024
LRU_BLOCKS = 8
LRU_C = 8.0
CONV_W = 4
CONV_CH = 4096
CHUNK = 64
LN_EPS = 1e-5
RMS_EPS = 1e-6
L2_EPS = 1e-6
PROJ_MAIN = 6144
PROJ_TN = 512
MLP_TF = 512
TM_PROJ = 2048
TM_MLP = 1024
TM_SMALL = 512
MXU_N = 256
MLP_ROW_BLOCKS = 2
CONV_TAIL = SUBLANES
SEQS_PER_BLOCK = 8
NEG_BIG = -1e30


def _dot_f32(a, b):
    return lax.dot_general(a, b, (((1,), (0,)), ((), ())), precision=HI,
                           preferred_element_type=F32)


def _mm(a, b):
    return lax.dot_general(a.astype(BF16), b.astype(BF16), (((1,), (0,)), ((), ())),
                           preferred_element_type=F32)


def _mm_nt(a, b):
    return lax.dot_general(a.astype(BF16), b.astype(BF16), (((1,), (1,)), ((), ())),
                           preferred_element_type=F32)


def _mm_tn(a, b):
    return lax.dot_general(a.astype(BF16), b.astype(BF16), (((0,), (0,)), ((), ())),
                           preferred_element_type=F32)


def _sigmoid(x):
    return 1.0 / (1.0 + jnp.exp(-x))


def _silu(x):
    return x * _sigmoid(x)


def _softplus(x):
    return jnp.maximum(x, 0.0) + jnp.log1p(jnp.exp(-jnp.abs(x)))


def _gelu_tanh(x):
    return 0.5 * x * (1.0 + jnp.tanh(math.sqrt(2.0 / math.pi) * (x + 0.044715 * (x * x * x))))


def _layer_norm(x, g, b):
    mu = jnp.mean(x, axis=-1, keepdims=True)
    xc = x - mu
    var = jnp.mean(xc * xc, axis=-1, keepdims=True)
    return xc * lax.rsqrt(var + LN_EPS) * g + b


def _col_to_row(col, eye):
    return jnp.sum(jnp.where(eye, col, 0.0), axis=0, keepdims=True)


def _seg_masks(L, seg):
    ri = lax.broadcasted_iota(jnp.int32, (L, L), 0)
    ci = lax.broadcasted_iota(jnp.int32, (L, L), 1)
    same = (ri // seg) == (ci // seg)
    return same, same & (ri >= ci), same & (ri > ci), ri == ci


def _seg_tile(a, s, seg):
    if seg >= SUBLANES:
        return a[s * seg:(s + 1) * seg]
    t = (s * seg) // SUBLANES
    return a[t * SUBLANES:(t + 1) * SUBLANES]


def _seg_only(a_tile, s, seg):
    if seg >= SUBLANES:
        return a_tile
    row = lax.broadcasted_iota(jnp.int32, (SUBLANES, 1), 0)
    lo = (s * seg) % SUBLANES
    return jnp.where((row >= lo) & (row < lo + seg), a_tile, 0.0)


def _seg_join(parts, seg):
    if seg >= SUBLANES:
        return jnp.concatenate(parts, axis=0)
    per_tile = SUBLANES // seg
    row = lax.broadcasted_iota(jnp.int32, (SUBLANES, 1), 0)
    tiles = []
    for t in range(len(parts) // per_tile):
        tile = parts[t * per_tile]
        for i in range(1, per_tile):
            tile = jnp.where(row >= i * seg, parts[t * per_tile + i], tile)
        tiles.append(tile)
    return jnp.concatenate(tiles, axis=0)


def _seg_rows(vals, L, seg):
    if seg >= SUBLANES:
        return jnp.concatenate([jnp.broadcast_to(v, (seg, v.shape[1])) for v in vals], axis=0)
    seg_id = lax.broadcasted_iota(jnp.int32, (L, 1), 0) // seg
    out = jnp.broadcast_to(vals[0], (L, vals[0].shape[1]))
    for i in range(1, len(vals)):
        out = jnp.where(seg_id == i, vals[i], out)
    return out


def _ticker(overlap, n_ticks):
    pending = list(overlap()) if overlap is not None else []
    total = len(pending)
    state = [0, 0]

    def tick(count=1):
        state[0] += 0 if count is None else count
        target = total if count is None else min(total, (state[0] * total) // n_ticks)
        while state[1] < target:
            pending[state[1]]()
            state[1] += 1
    return tick


def _params(sem):
    return pltpu.CompilerParams(dimension_semantics=sem, vmem_limit_bytes=VMEM_LIMIT)


def _proj_body(x_ref, wt_ref, wgt_ref, o_ref, og_ref, xb_ref):
    @pl.when(pl.program_id(1) == 0)
    def _():
        xb = x_ref[...].astype(BF16)
        xb_ref[...] = xb
        og_ref[...] = lax.dot_general(xb, wgt_ref[...].astype(BF16), (((1,), (1,)), ((), ())),
                                      preferred_element_type=F32)

    o_ref[...] = lax.dot_general(xb_ref[...], wt_ref[...].astype(BF16), (((1,), (1,)), ((), ())),
                                 preferred_element_type=F32)


def _proj(x, wt_stack, layer, wgt, tm, tn):
    M, K = x.shape
    return pl.pallas_call(
        _proj_body,
        grid=(M // tm, PROJ_MAIN // tn),
        in_specs=[pl.BlockSpec((tm, K), lambda i, j: (i, 0), pipeline_mode=pl.Buffered(1)),
                  pl.BlockSpec((None, tn, K), lambda i, j: (layer, j, 0)),
                  pl.BlockSpec((LANES, K), lambda i, j: (0, 0))],
        out_specs=[pl.BlockSpec((tm, tn), lambda i, j: (i, j)),
                   pl.BlockSpec((tm, LANES), lambda i, j: (i, 0))],
        out_shape=[jax.ShapeDtypeStruct((M, PROJ_MAIN), F32),
                   jax.ShapeDtypeStruct((M, LANES), F32)],
        scratch_shapes=[pltpu.VMEM((tm, K), BF16)],
        compiler_params=_params(("parallel", "arbitrary")),
        name="in_proj",
    )(x, wt_stack, wgt)


def _row_window(total_rows, rows, row0, tm):
    assert rows % tm == 0 and row0 % tm == 0
    return rows // tm, row0 // tm


def _into_operand(into):
    if into is None:
        return [], []
    return [into], [pl.BlockSpec(memory_space=pl.ANY)]


def _outproj_ln_body(y_ref, x_ref, w_ref, g_ref, b_ref, o_ref, *, alpha):
    mix = jnp.dot(y_ref[...].astype(BF16), w_ref[...], preferred_element_type=F32)
    o_ref[...] = _layer_norm(alpha * x_ref[...] + mix, g_ref[...], b_ref[...])


def _outproj_ln(y, x, w_stack, layer, g, b, tm, alpha, x_row0=0):
    M, K = y.shape
    D = w_stack.shape[2]
    n, off = _row_window(M, M, x_row0, tm)
    row = lambda i: (i, 0)
    fixed = lambda i: (0, 0)
    return pl.pallas_call(
        functools.partial(_outproj_ln_body, alpha=alpha),
        grid=(n,),
        in_specs=[pl.BlockSpec((tm, K), row), pl.BlockSpec((tm, D), lambda i: (off + i, 0)),
                  pl.BlockSpec((None, K, D), lambda i: (layer, 0, 0), pipeline_mode=pl.Buffered(1)),
                  pl.BlockSpec((1, D), fixed), pl.BlockSpec((1, D), fixed)],
        out_specs=pl.BlockSpec((tm, D), row),
        out_shape=jax.ShapeDtypeStruct((M, D), F32),
        compiler_params=_params(("parallel",)),
        name="out_proj_ln",
    )(y, x, w_stack, g, b)


def _mlp_pre(f, h_ref, o_ref, hb_ref):
    @pl.when(f == 0)
    def _():
        hb_ref[...] = h_ref[...].astype(BF16)
        o_ref[...] = jnp.zeros_like(o_ref)


def _mlp_main(wu_ref, wd_ref, o_ref, hb_ref):
    up = jnp.dot(hb_ref[...], wu_ref[...].astype(BF16), preferred_element_type=F32)
    act = jnp.square(jnp.maximum(up, 0.0)).astype(BF16)
    o_ref[...] += jnp.dot(act, wd_ref[...].astype(BF16), preferred_element_type=F32)


def _mlp_pieces(wu_ref, wd_ref, o_ref, hb_ref):
    tm, D = hb_ref.shape
    tf = wu_ref.shape[1]
    rb = tm // MLP_ROW_BLOCKS
    acts = {}
    pieces = []
    for r in range(MLP_ROW_BLOCKS):
        rows = slice(r * rb, (r + 1) * rb)
        for n in range(tf // MXU_N):
            def up_piece(r=r, n=n, rows=rows):
                w = wu_ref[:, n * MXU_N:(n + 1) * MXU_N].astype(BF16)
                up = jnp.dot(hb_ref[rows, :], w, preferred_element_type=F32)
                acts[r, n] = jnp.square(jnp.maximum(up, 0.0)).astype(BF16)
            pieces.append(up_piece)
        for k in range(D // MXU_N):
            def down_piece(r=r, k=k, rows=rows):
                act = jnp.concatenate([acts[r, n] for n in range(tf // MXU_N)], axis=1)
                w = wd_ref[:, k * MXU_N:(k + 1) * MXU_N].astype(BF16)
                o_ref[rows, k * MXU_N:(k + 1) * MXU_N] += jnp.dot(act, w, preferred_element_type=F32)
            pieces.append(down_piece)
    return pieces


def _mlp_post(f, nf, h_ref, g_ref, b_ref, o_ref, alpha):
    @pl.when(f == nf - 1)
    def _():
        o_ref[...] = _layer_norm(alpha * h_ref[...] + o_ref[...], g_ref[...], b_ref[...])


def _mlp_ln_body(h_ref, wu_ref, wd_ref, g_ref, b_ref, *rest, alpha):
    o_ref, hb_ref = rest[-2:]
    f = pl.program_id(1)
    _mlp_pre(f, h_ref, o_ref, hb_ref)
    _mlp_main(wu_ref, wd_ref, o_ref, hb_ref)
    _mlp_post(f, pl.num_programs(1), h_ref, g_ref, b_ref, o_ref, alpha)


def _mlp_specs(D, tm, tf, layer, out_off):
    in_specs = [pl.BlockSpec((tm, D), lambda i, f: (i, 0), pipeline_mode=pl.Buffered(1)),
                pl.BlockSpec((None, D, tf), lambda i, f: (layer, 0, f)),
                pl.BlockSpec((None, tf, D), lambda i, f: (layer, f, 0)),
                pl.BlockSpec((1, D), lambda i, f: (0, 0)),
                pl.BlockSpec((1, D), lambda i, f: (0, 0))]
    out_spec = pl.BlockSpec((tm, D), lambda i, f: (out_off + i, 0), pipeline_mode=pl.Buffered(1))
    return in_specs, out_spec, [pltpu.VMEM((tm, D), BF16)]


def _mlp_ln(h, wu_stack, wd_stack, layer, g, b, tm, tf, alpha, out_row0=0, into=None):
    M, D = h.shape
    FF = wu_stack.shape[2]
    n, off = _row_window(M, M, out_row0, tm)
    in_specs, out_spec, scratch = _mlp_specs(D, tm, tf, layer, off)
    extra, extra_specs = _into_operand(into)
    return pl.pallas_call(
        functools.partial(_mlp_ln_body, alpha=alpha),
        grid=(n, FF // tf),
        in_specs=in_specs + extra_specs,
        out_specs=out_spec,
        out_shape=jax.ShapeDtypeStruct(((M if into is None else into.shape[0]), D), F32),
        scratch_shapes=scratch,
        input_output_aliases={5: 0} if extra else {},
        compiler_params=_params(("parallel", "arbitrary")),
        name="mlp_ln",
    )(h, wu_stack, wd_stack, g, b, *extra)


def _ple_body(h_ref, p_ref, wg_ref, wp_ref, o_ref):
    h = h_ref[...]
    gate = _sigmoid(jnp.dot(h.astype(BF16), wg_ref[...], preferred_element_type=F32))
    emb = jnp.dot(p_ref[...].astype(BF16), wp_ref[...], preferred_element_type=F32)
    o_ref[...] = h + gate * emb


def _ple(h, p_stack, layer, wg_stack, wp_stack, tm):
    M, D = h.shape
    P = p_stack.shape[2]
    row = lambda i: (i, 0)
    return pl.pallas_call(
        _ple_body,
        grid=(M // tm,),
        in_specs=[pl.BlockSpec((tm, D), row),
                  pl.BlockSpec((None, tm, P), lambda i: (layer, i, 0)),
                  pl.BlockSpec((None, D, D), lambda i: (layer, 0, 0), pipeline_mode=pl.Buffered(1)),
                  pl.BlockSpec((None, P, D), lambda i: (layer, 0, 0), pipeline_mode=pl.Buffered(1))],
        out_specs=pl.BlockSpec((tm, D), row),
        out_shape=jax.ShapeDtypeStruct((M, D), F32),
        compiler_params=_params(("parallel",)),
        name="ple_gate",
    )(h, p_stack, wg_stack, wp_stack)


def _mixer_e_body(proj_ref, gate_ref, conv0_ref, s0_ref, h0_ref, wconv_ref, bconv_ref, alog_ref,
                  dtb_ref, nw_ref, wri_ref, br_ref, bi_ref, lam_ref,
                  y_ref, convn_ref, s_ref, hl_ref,
                  cbuf, a_scr, bx_scr, hseq, *, L, seg, t_valid, conv_win,
                  chunk=None, n_chunks=None, overlap=None):
    c = pl.program_id(1) if chunk is None else chunk
    last = (pl.num_programs(1) if n_chunks is None else n_chunks) - 1
    heads = range(HEADS)
    nseg = L // seg
    segs = range(nseg)
    stride = CONV_TAIL + seg

    packed = seg < SUBLANES

    @pl.when(c == 0)
    def _():
        if not packed:
            for s in segs:
                cbuf[s * stride:s * stride + CONV_TAIL, :] = conv0_ref[s * CONV_TAIL:(s + 1) * CONV_TAIL, :]
        s_ref[...] = s0_ref[...]
        hl_ref[...] = h0_ref[...]

    tick = _ticker(overlap, 28)
    same, tril, strict, eye = _seg_masks(L, seg)
    tril_f = jnp.where(tril, 1.0, 0.0).astype(F32)
    same_f = jnp.where(same, 1.0, 0.0).astype(F32)
    row_ok = (lax.broadcasted_iota(jnp.int32, (L, 1), 0) % seg) < t_valid
    s_old = [[s_ref[s, h] for h in heads] for s in segs]
    h_old = [hl_ref[s] for s in segs]

    if packed:
        u = proj_ref[:, 0:CONV_CH]
        hist = conv0_ref[...]
        rowm = lax.broadcasted_iota(jnp.int32, (L, 1), 0) % seg
        acc = bconv_ref[...] + u * wconv_ref[CONV_W - 1:CONV_W, :]
        for j in range(1, CONV_W):
            shifted = jnp.where(rowm >= j, pltpu.roll(u, j, axis=0), pltpu.roll(hist, (j - seg) % L, axis=0))
            acc = acc + shifted * wconv_ref[CONV_W - 1 - j:CONV_W - j, :]
        convn_ref[...] = u
    else:
        for s in segs:
            cbuf[s * stride + CONV_TAIL:(s + 1) * stride, :] = proj_ref[s * seg:(s + 1) * seg, 0:CONV_CH]
        full = cbuf[...]
        prev = pltpu.roll(full, 1, axis=0)
        pair_new = full * wconv_ref[3:4, :] + prev * wconv_ref[2:3, :]
        pair_old = full * wconv_ref[1:2, :] + prev * wconv_ref[0:1, :]
        conv_full = bconv_ref[...] + pair_new + pltpu.roll(pair_old, 2, axis=0)
        acc = jnp.concatenate([conv_full[s * stride + CONV_TAIL:(s + 1) * stride, :] for s in segs], axis=0)

    tick()
    gp = gate_ref[...]
    g_all = -jnp.exp(alog_ref[...]) * _softplus(gp + dtb_ref[...])
    beta_all = _sigmoid(gp)
    if t_valid < seg:
        g_all = jnp.where(row_ok, g_all, 0.0)
        beta_all = jnp.where(row_ok, beta_all, 0.0)
    gc_all = _dot_f32(tril_f, g_all)
    gl_all = _dot_f32(same_f, g_all)
    tick()

    q_l, k_l, kb_l, gcol_l, egc_l, y_l = [], [], [], [], [], []
    qa_l = [_silu(acc[:, h * DK:(h + 1) * DK]) for h in heads]
    tick()
    ka_l = [_silu(acc[:, HEADS * DK + h * DK:HEADS * DK + (h + 1) * DK]) for h in heads]
    tick()
    qn_l = [lax.rsqrt(jnp.sum(qa_l[h] * qa_l[h], axis=-1, keepdims=True) + L2_EPS) * (DK ** -0.5)
            for h in heads]
    kn_l = [lax.rsqrt(jnp.sum(ka_l[h] * ka_l[h], axis=-1, keepdims=True) + L2_EPS) for h in heads]
    tick()
    for h in heads:
        q = qa_l[h] * qn_l[h]
        k = ka_l[h] * kn_l[h]
        v = _silu(acc[:, 2 * HEADS * DK + h * A_DV:2 * HEADS * DK + (h + 1) * A_DV])
        gcol = gc_all[:, h:h + 1]
        bcol = beta_all[:, HEADS + h:HEADS + h + 1]
        egc = jnp.exp(gcol)
        kb = k * bcol
        q_l.append(q)
        k_l.append(k)
        kb_l.append(kb)
        gcol_l.append(gcol)
        egc_l.append(egc)
        y_l.append(jnp.concatenate([v * bcol, kb * egc], axis=1))
        if h % 2:
            tick()
    kq_k = [_mm_nt(jnp.concatenate([kb_l[h], q_l[h]], axis=0), k_l[h]) for h in heads]
    tick()
    p_l, qk_l = [], []
    for h in heads:
        grow = _col_to_row(gcol_l[h], eye)
        decay = jnp.where(tril, jnp.exp(jnp.where(tril, gcol_l[h] - grow, 0.0)), 0.0)
        p_l.append(jnp.where(strict, -(kq_k[h][0:L] * decay), 0.0))
        qk_l.append(kq_k[h][L:2 * L] * decay)
        if h % 4 == 3:
            tick()

    n_levels = max(1, math.ceil(math.log2(t_valid)))
    r_l = p_l
    if n_levels >= 2:
        p_l = [_mm(p_l[h], p_l[h]) for h in heads]
        tick()
        for _ in range(n_levels - 2):
            out = [_mm(jnp.concatenate([p_l[h], r_l[h]], axis=0), p_l[h]) for h in heads]
            r_l = [r_l[h] + p_l[h] + out[h][L:2 * L] for h in heads]
            p_l = [out[h][0:L] for h in heads]
            tick()
        out = [_mm(r_l[h], p_l[h]) for h in heads]
        r_l = [r_l[h] + p_l[h] + out[h] for h in heads]
        tick()
    y_l = [y_l[h] + _mm(r_l[h], y_l[h]) for h in heads]
    tick()

    R = max(seg, SUBLANES)
    qe_l = [q_l[h] * egc_l[h] for h in heads]
    wq_s = [[_mm(jnp.concatenate([_seg_tile(y_l[h][:, A_DV:A_DV + DK], s, seg),
                                  _seg_tile(qe_l[h], s, seg)], axis=0), s_old[s][h])
             for h in heads] for s in segs]
    v_new = [y_l[h][:, 0:A_DV] - _seg_join([wq_s[s][h][0:R] for s in segs], seg) for h in heads]
    tick()
    o_intra = [_mm(qk_l[h], v_new[h]) for h in heads]
    tick()
    kd_l = [k_l[h] * jnp.exp(gl_all[:, h:h + 1] - gcol_l[h]) for h in heads]
    s_upd = [[s_old[s][h] * jnp.exp(gl_all[s * seg:s * seg + 1, h:h + 1])
              + _mm_tn(_seg_only(_seg_tile(kd_l[h], s, seg), s, seg), _seg_tile(v_new[h], s, seg))
              for h in heads] for s in segs]

    bw = LRU_W // LRU_BLOCKS
    xr = acc[:, 3 * HEADS * DK:3 * HEADS * DK + LRU_W]
    sp = _softplus(-lam_ref[...])
    for n in range(LRU_BLOCKS):
        sl = slice(n * bw, (n + 1) * bw)
        xb = xr[:, sl]
        ri_pre = _mm(xb, wri_ref[n])
        r_pre = ri_pre[:, 0:bw] + br_ref[:, sl]
        i_pre = ri_pre[:, bw:2 * bw] + bi_ref[:, sl]
        log_a = -LRU_C * _sigmoid(r_pre) * sp[:, sl]
        a = jnp.exp(log_a)
        a_scr[:, sl] = a
        bx_scr[:, sl] = jnp.sqrt(-jnp.tanh(log_a) * (a * a + 1.0)) * _sigmoid(i_pre) * xb
        tick()

    for s in segs:
        hc = h_old[s]
        for t in range(seg):
            row = s * seg + t
            if t < t_valid:
                hc = a_scr[row:row + 1, :] * hc + bx_scr[row:row + 1, :]
            hseq[row:row + 1, :] = hc
        hl_ref[s] = hc
    tick()

    o_l = [_seg_join([wq_s[s][h][R:2 * R] for s in segs], seg) + o_intra[h] for h in heads]
    rms_l = [lax.rsqrt(jnp.mean(o_l[h] * o_l[h], axis=-1, keepdims=True) + RMS_EPS) for h in heads]
    tick()
    for h in heads:
        z = proj_ref[:, CONV_CH + h * A_DV:CONV_CH + (h + 1) * A_DV]
        y_ref[:, h * A_DV:(h + 1) * A_DV] = o_l[h] * rms_l[h] * nw_ref[...] * _silu(z)
    for s in segs:
        for h in heads:
            s_ref[s, h] = s_upd[s][h]
    gate = proj_ref[:, CONV_CH + HEADS * A_DV:CONV_CH + HEADS * A_DV + LRU_W]
    y_ref[:, HEADS * A_DV:HEADS * A_DV + LRU_W] = hseq[...] * _gelu_tanh(gate)
    tick(None)

    if not packed:
        @pl.when(c == last)
        def _():
            for s in segs:
                convn_ref[s * SUBLANES:(s + 1) * SUBLANES, :] = (
                    cbuf[s * stride + conv_win:s * stride + conv_win + SUBLANES, :])

        for s in segs:
            cbuf[s * stride:s * stride + CONV_TAIL, :] = cbuf[s * stride + seg:s * stride + seg + CONV_TAIL, :]


def _mixer_e_parts(proj, gates, conv0, s0, h0, wconv, bconv, alog, dtb, nw, wri, br, bi, lam,
                   *, seq0, nblk, nc, L, seg, t_valid, bc):
    nseg = L // seg
    assert nc == 1 or nseg == 1
    assert t_valid >= CONV_W - 1
    if seg < SUBLANES:
        assert nc == 1 and t_valid == seg and L % SUBLANES == 0
        hist_rows, conv_win, conv_off = L, 0, seg - (CONV_W - 1)
    else:
        conv_row = CONV_TAIL - (CONV_W - 1) + t_valid
        conv_win = (conv_row // SUBLANES) * SUBLANES
        conv_off = conv_row - conv_win
        assert conv_off + CONV_W - 1 <= SUBLANES
        hist_rows = nseg * SUBLANES
    mix = HEADS * A_DV + LRU_W

    def chunk(*g):
        b, c = bc(*g)
        return ((seq0 + b) * nc + c, 0)

    def chunk_out(*g):
        b, c = bc(*g)
        return (b * nc + c, 0)

    blk_in = lambda *g: seq0 + bc(*g)[0]
    blk_out = lambda *g: bc(*g)[0]
    fix2 = lambda *g: (0, 0)
    fix3 = lambda *g: (0, 0, 0)
    return dict(
        body=functools.partial(_mixer_e_body, L=L, seg=seg, t_valid=t_valid, conv_win=conv_win),
        operands=[proj, gates, conv0, s0, h0, wconv, bconv, alog, dtb, nw, wri, br, bi, lam],
        in_specs=[pl.BlockSpec((L, PROJ_MAIN), chunk),
                  pl.BlockSpec((L, LANES), chunk),
                  pl.BlockSpec((hist_rows, CONV_CH), lambda *g: (blk_in(*g), 0)),
                  pl.BlockSpec((nseg, HEADS, DK, A_DV), lambda *g: (blk_in(*g), 0, 0, 0)),
                  pl.BlockSpec((nseg, 1, LRU_W), lambda *g: (blk_in(*g), 0, 0)),
                  pl.BlockSpec((CONV_W, CONV_CH), fix2),
                  pl.BlockSpec((1, CONV_CH), fix2),
                  pl.BlockSpec((1, LANES), fix2),
                  pl.BlockSpec((1, LANES), fix2),
                  pl.BlockSpec((1, A_DV), fix2),
                  pl.BlockSpec((LRU_BLOCKS, LRU_W // LRU_BLOCKS, 2 * LRU_W // LRU_BLOCKS), fix3),
                  pl.BlockSpec((1, LRU_W), fix2),
                  pl.BlockSpec((1, LRU_W), fix2),
                  pl.BlockSpec((1, LRU_W), fix2)],
        out_specs=[pl.BlockSpec((L, mix), chunk_out),
                   pl.BlockSpec((hist_rows, CONV_CH), lambda *g: (blk_out(*g), 0)),
                   pl.BlockSpec((nseg, HEADS, DK, A_DV), lambda *g: (blk_out(*g), 0, 0, 0)),
                   pl.BlockSpec((nseg, 1, LRU_W), lambda *g: (blk_out(*g), 0, 0))],
        out_shape=[jax.ShapeDtypeStruct((nblk * nc * L, mix), F32),
                   jax.ShapeDtypeStruct((nblk * hist_rows, CONV_CH), F32),
                   jax.ShapeDtypeStruct((nblk * nseg, HEADS, DK, A_DV), F32),
                   jax.ShapeDtypeStruct((nblk * nseg, 1, LRU_W), F32)],
        scratch_shapes=[pltpu.VMEM((nseg * (CONV_TAIL + seg), CONV_CH), F32),
                        pltpu.VMEM((L, LRU_W), F32),
                        pltpu.VMEM((L, LRU_W), F32),
                        pltpu.VMEM((L, LRU_W), F32)],
        name="delta_lru_mixer", conv_off=conv_off)


def _mixer_o_body(proj_ref, gate_ref, c0_ref, n0_ref, m0_ref, big_ref, bfg_ref, nw_ref,
                  y_ref, c_ref, n_ref, m_ref, *, L, seg, t_valid, chunk=None, n_chunks=None, overlap=None):
    ci = pl.program_id(1) if chunk is None else chunk
    heads = range(HEADS)
    nseg = L // seg
    segs = range(nseg)

    @pl.when(ci == 0)
    def _():
        c_ref[...] = c0_ref[...]
        n_ref[...] = n0_ref[...]
        m_ref[...] = m0_ref[...]

    tick = _ticker(overlap, 16)
    same, tril, _, eye = _seg_masks(L, seg)
    tril_f = jnp.where(tril, 1.0, 0.0).astype(F32)
    same_f = jnp.where(same, 1.0, 0.0).astype(F32)
    row_id = lax.broadcasted_iota(jnp.int32, (L, 1), 0)
    row_ok = (row_id % seg) < t_valid
    c_old = [[c_ref[s, h] for h in heads] for s in segs]
    n_old = [n_ref[s] for s in segs]
    m_old = [m_ref[s] for s in segs]

    gp = gate_ref[...]
    ig_all = gp + big_ref[...]
    logf_all = -_softplus(-(gp + bfg_ref[...]))
    if t_valid < seg:
        ig_all = jnp.where(row_ok, ig_all, NEG_BIG)
        logf_all = jnp.where(row_ok, logf_all, 0.0)
    bcum_all = _dot_f32(tril_f, logf_all)
    blast_all = _dot_f32(same_f, logf_all)
    mrow_all = _seg_rows(m_old, L, seg)
    tick()

    q_l = [proj_ref[:, h * DK:(h + 1) * DK] for h in heads]
    k_l = [proj_ref[:, HEADS * DK + h * DK:HEADS * DK + (h + 1) * DK] * (DK ** -0.5) for h in heads]
    v_l = [proj_ref[:, 2 * HEADS * DK + h * C_DV:2 * HEADS * DK + (h + 1) * C_DV] for h in heads]
    qk_l = [_mm_nt(q_l[h], k_l[h]) for h in heads]
    tick()
    qc_l = [_seg_join([_mm_nt(_seg_tile(q_l[h], s, seg), c_old[s][h]) for s in segs], seg) for h in heads]
    tick()

    s_l, e_l, mt_l, sc_l, ws_l, mnew_l = [], [], [], [], [], []
    for h in heads:
        bcol = bcum_all[:, HEADS + h:HEADS + h + 1]
        blast = blast_all[:, HEADS + h:HEADS + h + 1]
        icol = ig_all[:, h:h + 1]
        mcol = mrow_all[:, h:h + 1]
        brow = _col_to_row(bcol, eye)
        irow = _col_to_row(icol, eye)
        d_intra = jnp.where(tril, bcol - brow + irow, NEG_BIG)
        ge_mat = jnp.where(same, blast - brow + irow, NEG_BIG)
        ge_col = blast - bcol + icol
        inter = bcol + mcol
        m_t = jnp.maximum(inter, jnp.max(d_intra, axis=-1, keepdims=True))
        m_new = jnp.maximum(blast + mcol, jnp.max(ge_mat, axis=-1, keepdims=True))
        e_l.append(jnp.exp(inter - m_t))
        mt_l.append(m_t)
        s_l.append(qk_l[h] * jnp.exp(d_intra - m_t))
        sc_l.append(jnp.exp(blast + mcol - m_new))
        ws_l.append(jnp.exp(ge_col - m_new))
        mnew_l.append(m_new)
        tick()

    sv = [_mm(s_l[h], v_l[h]) for h in heads]
    tick()
    vw_l = [v_l[h] * ws_l[h] for h in heads]
    kw_l = [k_l[h] * ws_l[h] for h in heads]
    kv = [[_mm_tn(_seg_only(_seg_tile(vw_l[h], s, seg), s, seg), _seg_tile(k_l[h], s, seg)) for h in heads]
          for s in segs]

    nrow_l = [_seg_rows([n_old[s][h:h + 1, :] for s in segs], L, seg) for h in heads]
    den_l = [e_l[h] * jnp.sum(q_l[h] * nrow_l[h], axis=-1, keepdims=True)
             + jnp.sum(s_l[h], axis=-1, keepdims=True) for h in heads]
    inv_l = [1.0 / jnp.maximum(jnp.abs(den_l[h]), jnp.exp(-mt_l[h])) for h in heads]
    tick()
    hh_l = [(e_l[h] * qc_l[h] + sv[h]) * inv_l[h] for h in heads]
    rms_l = [lax.rsqrt(jnp.mean(hh_l[h] * hh_l[h], axis=-1, keepdims=True) + RMS_EPS) for h in heads]
    tick()
    for h in heads:
        o_pre = proj_ref[:, 2 * HEADS * DK + HEADS * C_DV + h * C_DV:
                         2 * HEADS * DK + HEADS * C_DV + (h + 1) * C_DV]
        y_ref[:, h * C_DV:(h + 1) * C_DV] = (hh_l[h] * rms_l[h] * nw_ref[:, h * C_DV:(h + 1) * C_DV]
                                             * _sigmoid(o_pre))
        if h % 4 == 3:
            tick()
    lane = lax.broadcasted_iota(jnp.int32, (1, LANES), 1)
    for s in segs:
        r0 = s * seg
        m_row = m_old[s]
        n_rows = []
        for h in heads:
            sc = sc_l[h][r0:r0 + 1, :]
            c_ref[s, h] = sc * c_old[s][h] + kv[s][h]
            n_rows.append(sc * n_old[s][h:h + 1, :]
                          + jnp.sum(_seg_only(_seg_tile(kw_l[h], s, seg), s, seg), axis=0, keepdims=True))
            m_row = jnp.where(lane == h, mnew_l[h][r0:r0 + 1, :], m_row)
        n_ref[s] = jnp.concatenate(n_rows, axis=0)
        m_ref[s] = m_row
    tick(None)


def _mixer_o_parts(proj, gates, c0, n0, m0, big, bfg, nw, *, seq0, nblk, nc, L, seg, t_valid, bc):
    nseg = L // seg
    assert nc == 1 or nseg == 1
    cv = HEADS * C_DV

    def chunk(*g):
        b, c = bc(*g)
        return ((seq0 + b) * nc + c, 0)

    def chunk_out(*g):
        b, c = bc(*g)
        return (b * nc + c, 0)

    blk_in = lambda *g: seq0 + bc(*g)[0]
    blk_out = lambda *g: bc(*g)[0]
    fix2 = lambda *g: (0, 0)
    return dict(
        body=functools.partial(_mixer_o_body, L=L, seg=seg, t_valid=t_valid),
        operands=[proj, gates, c0, n0, m0, big, bfg, nw],
        in_specs=[pl.BlockSpec((L, PROJ_MAIN), chunk),
                  pl.BlockSpec((L, LANES), chunk),
                  pl.BlockSpec((nseg, HEADS, C_DV, DK), lambda *g: (blk_in(*g), 0, 0, 0)),
                  pl.BlockSpec((nseg, HEADS, DK), lambda *g: (blk_in(*g), 0, 0)),
                  pl.BlockSpec((nseg, 1, LANES), lambda *g: (blk_in(*g), 0, 0)),
                  pl.BlockSpec((1, LANES), fix2),
                  pl.BlockSpec((1, LANES), fix2),
                  pl.BlockSpec((1, cv), fix2)],
        out_specs=[pl.BlockSpec((L, cv), chunk_out),
                   pl.BlockSpec((nseg, HEADS, C_DV, DK), lambda *g: (blk_out(*g), 0, 0, 0)),
                   pl.BlockSpec((nseg, HEADS, DK), lambda *g: (blk_out(*g), 0, 0)),
                   pl.BlockSpec((nseg, 1, LANES), lambda *g: (blk_out(*g), 0, 0))],
        out_shape=[jax.ShapeDtypeStruct((nblk * nc * L, cv), F32),
                   jax.ShapeDtypeStruct((nblk * nseg, HEADS, C_DV, DK), F32),
                   jax.ShapeDtypeStruct((nblk * nseg, HEADS, DK), F32),
                   jax.ShapeDtypeStruct((nblk * nseg, 1, LANES), F32)],
        scratch_shapes=[],
        name="mlstm_mixer")


def _run_mixer(parts, nblk, nc):
    return pl.pallas_call(
        parts['body'],
        grid=(nblk, nc),
        in_specs=parts['in_specs'],
        out_specs=parts['out_specs'],
        out_shape=parts['out_shape'],
        scratch_shapes=parts['scratch_shapes'],
        compiler_params=_params(("parallel", "arbitrary")),
        name=parts['name'],
    )(*parts['operands'])


def _fused_body(*refs, mixer_body, n_mi, n_mo, n_ms, alpha):
    mi = refs[:n_mi]
    h_ref, wu_ref, wd_ref, g_ref, b_ref = refs[n_mi:n_mi + 5]
    outs = refs[n_mi + 6:]
    mo = outs[:n_mo]
    o_ref = outs[n_mo]
    ms = outs[n_mo + 1:n_mo + 1 + n_ms]
    hb_ref = outs[n_mo + 1 + n_ms]
    i, f = pl.program_id(0), pl.program_id(1)
    nf = pl.num_programs(1)
    _mlp_pre(f, h_ref, o_ref, hb_ref)
    mixer_body(*mi, *mo, *ms, chunk=i * nf + f, n_chunks=pl.num_programs(0) * nf,
               overlap=lambda: _mlp_pieces(wu_ref, wd_ref, o_ref, hb_ref))
    _mlp_post(f, nf, h_ref, g_ref, b_ref, o_ref, alpha)


def _run_mixer_mlp(parts, tiles, nf, h, wu_stack, wd_stack, layer, g, b, tm, tf, alpha, out_row0, h2_into):
    M, D = h.shape
    assert M == tiles * tm
    mlp_in, mlp_out, mlp_scr = _mlp_specs(D, tm, tf, layer, out_row0 // tm)
    n_mi, n_mo, n_ms = len(parts['operands']), len(parts['out_specs']), len(parts['scratch_shapes'])
    outs = pl.pallas_call(
        functools.partial(_fused_body, mixer_body=parts['body'], n_mi=n_mi, n_mo=n_mo, n_ms=n_ms, alpha=alpha),
        grid=(tiles, nf),
        in_specs=parts['in_specs'] + mlp_in + [pl.BlockSpec(memory_space=pl.ANY)],
        out_specs=parts['out_specs'] + [mlp_out],
        out_shape=parts['out_shape'] + [jax.ShapeDtypeStruct(h2_into.shape, F32)],
        scratch_shapes=parts['scratch_shapes'] + mlp_scr,
        input_output_aliases={n_mi + 5: n_mo},
        compiler_params=_params(("arbitrary", "arbitrary")),
        name=parts['name'] + "_mlp",
    )(*parts['operands'], h, wu_stack, wd_stack, g, b, h2_into)
    return outs[:n_mo], outs[n_mo]


def _pad_lanes(v, offset=0):
    return jnp.pad(v.astype(F32), (offset, LANES - offset - v.shape[0]))[None, :]


def _gate_weight_t(wt):
    wg = wt[PROJ_MAIN:, :]
    return jnp.pad(wg, ((0, LANES - wg.shape[0]), (0, 0)))


def _row_tile(m, cap):
    t = cap
    while t >= SUBLANES:
        if m % t == 0:
            return t
        t //= 2
    raise ValueError(f"row count {m} is not a multiple of {SUBLANES}")


def _blocking(B, T):
    t_valid = math.gcd(T, CHUNK)
    nc = T // t_valid
    if SUBLANES % t_valid == 0 and nc == 1:
        seg = t_valid
    else:
        seg = -(-t_valid // SUBLANES) * SUBLANES
    assert nc == 1 or seg == t_valid
    nseg = 1
    if nc == 1:
        nseg = max(1, min(B, SEQS_PER_BLOCK))
        while B % nseg or (nseg * seg) % SUBLANES:
            nseg -= 1
        assert nseg >= 1
    return t_valid, seg, nseg * seg, B // nseg, nc


def _group(x, p, conv, delta, lru, mc, mn, mm, W, depth):
    B, T, D = x.shape
    t_valid, seg, L, nblk, nc = _blocking(B, T)
    Tp = nc * seg
    if Tp != T:
        x = jnp.pad(x, ((0, 0), (0, Tp - T), (0, 0)))
        p = jnp.pad(p, ((0, 0), (0, 0), (0, Tp - T), (0, 0)))
    M = B * Tp
    FF = W['w_up'].shape[2]
    tiles = max(1, Tp // TM_MLP)
    pipelined = B > 1 and nc > 1 and nc % tiles == 0 and Tp % tiles == 0 and FF % (nc // tiles) == 0
    nf = nc // tiles if pipelined else FF // MLP_TF
    if pipelined:
        tm_mlp, tf = Tp // tiles, FF // nf
        pipelined = tf % LANES == 0 and tm_mlp % SUBLANES == 0
    if not pipelined:
        tm_mlp, tf, nf = _row_tile(M, TM_MLP), MLP_TF, FF // MLP_TF
    return dict(
        B=B, T=T, Tp=Tp, M=M, D=D, FF=FF, depth=depth, nblk=nblk, nc=nc, tiles=tiles, nf=nf,
        pipelined=pipelined, tm=_row_tile(Tp if pipelined else M, TM_SMALL), tm_proj=_row_tile(M, TM_PROJ),
        tm_mlp=tm_mlp, tf=tf, blk=dict(nc=nc, L=L, seg=seg, t_valid=t_valid),
        hist=seg if seg < SUBLANES else CONV_TAIL,
        x2=x.reshape(M, D), p2=p.reshape(depth, M, -1), conv=conv, delta=delta, lru=lru, mc=mc, mn=mn, mm=mm,
        spare=None, out=dict(conv=[], delta=[], lru=[], mc=[], mn=[], mm=[]))


def _layer_front(G, W, layer):
    j = layer // 2
    c = dict(j=j, layer=layer,
             ln1=(W['ln1_g'][layer][None, :], W['ln1_b'][layer][None, :]),
             ln2=(W['ln2_g'][layer][None, :], W['ln2_b'][layer][None, :]))
    if layer % 2 == 0:
        proj, gates = _proj(G['x2'], W['w_in_e'], j, W['wg_e'][j], G['tm_proj'], PROJ_TN)
        hist = G['hist']
        conv0 = jnp.pad(G['conv'][j], ((0, 0), (hist - (CONV_W - 1), 0), (0, 0)))
        c['make_parts'] = functools.partial(
            _mixer_e_parts, proj, gates, conv0.reshape(G['B'] * hist, CONV_CH), G['delta'][j],
            G['lru'][j][:, None, :], W['w_conv_e'][j], W['b_conv_e'][j][None, :],
            _pad_lanes(W['a_log_e'][j]), _pad_lanes(W['dt_bias_e'][j]),
            W['delta_norm_e'][j][None, :], W['lru_wri_e'][j], W['lru_br_e'][j][None, :],
            W['lru_bi_e'][j][None, :], W['lru_lambda_e'][j][None, :], **G['blk'])
        c['w_out'] = W['w_out_e']
    else:
        proj, gates = _proj(G['x2'], W['w_in_o'], j, W['wg_o'][j], G['tm_proj'], PROJ_TN)
        m0 = jnp.pad(G['mm'][j], ((0, 0), (0, LANES - HEADS)))[:, None, :]
        c['make_parts'] = functools.partial(
            _mixer_o_parts, proj, gates, G['mc'][j], G['mn'][j], m0,
            _pad_lanes(W['b_ig_o'][j]), _pad_lanes(W['b_fg_o'][j], HEADS),
            W['mlstm_norm_o'][j][None, :], **G['blk'])
        c['w_out'] = W['w_out_o']
    return c


def _plain_mixer(G, c, alpha):
    parts = c['make_parts'](seq0=0, nblk=G['nblk'], bc=lambda b, ch: (b, ch))
    outs = _run_mixer(parts, G['nblk'], G['blk']['nc'])
    c['states'], c['conv_off'] = outs[1:], parts.get('conv_off')
    return _outproj_ln(outs[0], G['x2'], c['w_out'], c['j'], *c['ln1'], G['tm'], alpha)


def _mlp(G, c, W, h1, alpha):
    return _mlp_ln(h1, W['w_up'], W['w_down'], c['layer'], *c['ln2'], G['tm_mlp'], G['tf'], alpha)


def _pipelined_layer(G, c, W, alpha, guest=None):
    B, Tp, nc, tiles, nf = G['B'], G['Tp'], G['blk']['nc'], G['tiles'], G['nf']
    layer, (g2, b2) = c['layer'], c['ln2']
    h = jnp.zeros((G['M'], G['D']), F32) if G['spare'] is None else G['spare']
    h1 = guest_out = None
    per_seq = []
    for b in range(B):
        fused_bc = lambda i, f, n=(guest[4] if b == 0 and guest else nf): (0, i * n + f)
        if b == 0 and guest is None:
            parts = c['make_parts'](seq0=b, nblk=1, bc=lambda blk_id, ch: (0, ch))
            outs = _run_mixer(parts, 1, nc)
        elif b == 0:
            g_h1, g_tm, g_tf, g_tiles, g_nf = guest
            parts = c['make_parts'](seq0=b, nblk=1, bc=fused_bc)
            outs, guest_out = _run_mixer_mlp(parts, g_tiles, g_nf, g_h1, W['w_up'], W['w_down'], layer, g2, b2,
                                             g_tm, g_tf, alpha, 0, jnp.zeros(g_h1.shape, F32))
        else:
            parts = c['make_parts'](seq0=b, nblk=1, bc=fused_bc)
            outs, h = _run_mixer_mlp(parts, tiles, nf, h1, W['w_up'], W['w_down'], layer, g2, b2,
                                     G['tm_mlp'], G['tf'], alpha, (b - 1) * Tp, h)
        per_seq.append(outs[1:])
        h1 = _outproj_ln(outs[0], G['x2'], c['w_out'], c['j'], *c['ln1'], G['tm'], alpha, x_row0=b * Tp)
    h = _mlp_ln(h1, W['w_up'], W['w_down'], layer, g2, b2, G['tm_mlp'], G['tf'], alpha,
                out_row0=(B - 1) * Tp, into=h)
    c['states'] = [jnp.concatenate([st[k] for st in per_seq], axis=0) for k in range(len(per_seq[0]))]
    c['conv_off'] = parts.get('conv_off')
    return h, guest_out


def _layer_back(G, c, W, h):
    st, out = c['states'], G['out']
    if c['layer'] % 2 == 0:
        off = c['conv_off']
        out['conv'].append(st[0].reshape(G['B'], G['hist'], CONV_CH)[:, off:off + CONV_W - 1])
        out['delta'].append(st[1])
        out['lru'].append(st[2][:, 0])
    else:
        out['mc'].append(st[0])
        out['mn'].append(st[1])
        out['mm'].append(st[2][:, 0, :HEADS])
    G['x2'] = _ple(h, G['p2'], c['layer'], W['w_ple_gate'], W['w_ple'], G['tm'])
    G['spare'] = h


def _results(G):
    out = G['out']
    y = G['x2'].reshape(G['B'], G['Tp'], G['D'])[:, :G['T']]
    return (y,) + tuple(jnp.stack(out[k]) for k in ('conv', 'delta', 'lru', 'mc', 'mn', 'mm'))


def kernel(x_prompt, x_sample, p_prompt, p_sample, state_conv, state_delta, state_lru,
           state_mlstm_c, state_mlstm_n, state_mlstm_m, w_in_e, w_conv_e, b_conv_e, a_log_e,
           dt_bias_e, delta_norm_e, lru_wr_e, lru_br_e, lru_wi_e, lru_bi_e, lru_lambda_e, w_out_e,
           w_in_o, b_ig_o, b_fg_o, mlstm_norm_o, w_out_o, ln1_g, ln1_b, ln2_g, ln2_b, w_up, w_down,
           w_ple, w_ple_gate):
    depth = ln1_g.shape[0]
    n_even, n_odd = w_in_e.shape[0], w_in_o.shape[0]
    alpha = (2 * depth) ** 0.25
    wt_e = jnp.swapaxes(w_in_e, 1, 2)
    wt_o = jnp.swapaxes(w_in_o, 1, 2)
    W = dict(
        w_in_e=wt_e, w_in_o=wt_o,
        wg_e=[_gate_weight_t(wt_e[j]) for j in range(n_even)],
        wg_o=[_gate_weight_t(wt_o[j]) for j in range(n_odd)],
        w_out_e=w_out_e.astype(BF16), w_out_o=w_out_o.astype(BF16),
        w_up=w_up, w_down=w_down,
        w_ple=w_ple.astype(BF16), w_ple_gate=w_ple_gate.astype(BF16),
        w_conv_e=w_conv_e, b_conv_e=b_conv_e, a_log_e=a_log_e, dt_bias_e=dt_bias_e,
        delta_norm_e=delta_norm_e, lru_br_e=lru_br_e,
        lru_wri_e=jnp.concatenate([lru_wr_e, lru_wi_e], axis=-1).astype(BF16),
        lru_bi_e=lru_bi_e, lru_lambda_e=lru_lambda_e, b_ig_o=b_ig_o, b_fg_o=b_fg_o,
        mlstm_norm_o=mlstm_norm_o, ln1_g=ln1_g, ln1_b=ln1_b, ln2_g=ln2_g, ln2_b=ln2_b)
    bp = x_prompt.shape[0]
    zeros = lambda a: jnp.zeros((a.shape[0], bp) + a.shape[2:], F32)
    P = _group(x_prompt, p_prompt, zeros(state_conv), zeros(state_delta), zeros(state_lru),
               zeros(state_mlstm_c), zeros(state_mlstm_n), zeros(state_mlstm_m), W, depth)
    S = _group(x_sample, p_sample, state_conv, state_delta, state_lru,
               state_mlstm_c, state_mlstm_n, state_mlstm_m, W, depth)
    for layer in range(depth):
        cs = _layer_front(S, W, layer)
        h1_s = _plain_mixer(S, cs, alpha)
        cp = _layer_front(P, W, layer)
        if P['pipelined']:
            tiles_s = S['M'] // S['tm_mlp']
            nf_s = P['blk']['nc'] // tiles_s if P['blk']['nc'] % tiles_s == 0 else 0
            ok = nf_s > 0 and S['FF'] % nf_s == 0 and (S['FF'] // nf_s) % MXU_N == 0
            guest = (h1_s, S['tm_mlp'], S['FF'] // nf_s, tiles_s, nf_s) if ok else None
            h_p, h_s = _pipelined_layer(P, cp, W, alpha, guest)
        else:
            h_p, h_s = _mlp(P, cp, W, _plain_mixer(P, cp, alpha), alpha), None
        if h_s is None:
            h_s = _mlp(S, cs, W, h1_s, alpha)
        _layer_back(S, cs, W, h_s)
        _layer_back(P, cp, W, h_p)
    out_p, out_s = _results(P), _results(S)
    return (out_p[0], out_s[0]) + tuple(out_p[1:]) + tuple(out_s[1:])
```

```python
import functools
import math

import jax
import jax.numpy as jnp
from jax import lax
from jax.experimental import pallas as pl
from jax.experimental.pallas import tpu as pltpu

F32 = jnp.float32
BF16 = jnp.bfloat16
HI = lax.Precision.HIGHEST

LANES = 128
SUBLANES = 8
VMEM_LIMIT = 60 * 1024 * 1024

HEADS = 8
DK = 128
A_DV = 128
C_DV = 256
LRU_W = 1024
LRU_BLOCKS = 8
LRU_C = 8.0
CONV_W = 4
CONV_CH = 4096
CHUNK = 64
LN_EPS = 1e-5
RMS_EPS = 1e-6
L2_EPS = 1e-6
PROJ_MAIN = 6144
PROJ_TN = 512
MLP_TF = 512
TM_PROJ = 2048
TM_MLP = 1024
TM_SMALL = 512
MXU_N = 256
MLP_ROW_BLOCKS = 2
CONV_TAIL = SUBLANES
SEQS_PER_BLOCK = 8
NEG_BIG = -1e30


def _dot_f32(a, b):
    return lax.dot_general(a, b, (((1,), (0,)), ((), ())), precision=HI,
                           preferred_element_type=F32)


def _mm(a, b):
    return lax.dot_general(a.astype(BF16), b.astype(BF16), (((1,), (0,)), ((), ())),
                           preferred_element_type=F32)


def _mm_nt(a, b):
    return lax.dot_general(a.astype(BF16), b.astype(BF16), (((1,), (1,)), ((), ())),
                           preferred_element_type=F32)


def _mm_tn(a, b):
    return lax.dot_general(a.astype(BF16), b.astype(BF16), (((0,), (0,)), ((), ())),
                           preferred_element_type=F32)


def _sigmoid(x):
    return 1.0 / (1.0 + jnp.exp(-x))


def _silu(x):
    return x * _sigmoid(x)


def _softplus(x):
    return jnp.maximum(x, 0.0) + jnp.log1p(jnp.exp(-jnp.abs(x)))


def _gelu_tanh(x):
    return 0.5 * x * (1.0 + jnp.tanh(math.sqrt(2.0 / math.pi) * (x + 0.044715 * (x * x * x))))


def _layer_norm(x, g, b):
    mu = jnp.mean(x, axis=-1, keepdims=True)
    xc = x - mu
    var = jnp.mean(xc * xc, axis=-1, keepdims=True)
    return xc * lax.rsqrt(var + LN_EPS) * g + b


def _col_to_row(col, eye):
    return jnp.sum(jnp.where(eye, col, 0.0), axis=0, keepdims=True)


def _seg_masks(L, seg):
    ri = lax.broadcasted_iota(jnp.int32, (L, L), 0)
    ci = lax.broadcasted_iota(jnp.int32, (L, L), 1)
    same = (ri // seg) == (ci // seg)
    return same, same & (ri >= ci), same & (ri > ci), ri == ci


def _seg_tile(a, s, seg):
    if seg >= SUBLANES:
        return a[s * seg:(s + 1) * seg]
    t = (s * seg) // SUBLANES
    return a[t * SUBLANES:(t + 1) * SUBLANES]


def _seg_only(a_tile, s, seg):
    if seg >= SUBLANES:
        return a_tile
    row = lax.broadcasted_iota(jnp.int32, (SUBLANES, 1), 0)
    lo = (s * seg) % SUBLANES
    return jnp.where((row >= lo) & (row < lo + seg), a_tile, 0.0)


def _seg_join(parts, seg):
    if seg >= SUBLANES:
        return jnp.concatenate(parts, axis=0)
    per_tile = SUBLANES // seg
    row = lax.broadcasted_iota(jnp.int32, (SUBLANES, 1), 0)
    tiles = []
    for t in range(len(parts) // per_tile):
        tile = parts[t * per_tile]
        for i in range(1, per_tile):
            tile = jnp.where(row >= i * seg, parts[t * per_tile + i], tile)
        tiles.append(tile)
    return jnp.concatenate(tiles, axis=0)


def _seg_rows(vals, L, seg):
    if seg >= SUBLANES:
        return jnp.concatenate([jnp.broadcast_to(v, (seg, v.shape[1])) for v in vals], axis=0)
    seg_id = lax.broadcasted_iota(jnp.int32, (L, 1), 0) // seg
    out = jnp.broadcast_to(vals[0], (L, vals[0].shape[1]))
    for i in range(1, len(vals)):
        out = jnp.where(seg_id == i, vals[i], out)
    return out


def _ticker(overlap, n_ticks):
    pending = list(overlap()) if overlap is not None else []
    total = len(pending)
    state = [0, 0]

    def tick(count=1):
        state[0] += 0 if count is None else count
        target = total if count is None else min(total, (state[0] * total) // n_ticks)
        while state[1] < target:
            pending[state[1]]()
            state[1] += 1
    return tick


def _params(sem):
    return pltpu.CompilerParams(dimension_semantics=sem, vmem_limit_bytes=VMEM_LIMIT)


def _proj_body(x_ref, wt_ref, wgt_ref, o_ref, og_ref, xb_ref):
    @pl.when(pl.program_id(1) == 0)
    def _():
        xb = x_ref[...].astype(BF16)
        xb_ref[...] = xb
        og_ref[...] = lax.dot_general(xb, wgt_ref[...].astype(BF16), (((1,), (1,)), ((), ())),
                                      preferred_element_type=F32)

    o_ref[...] = lax.dot_general(xb_ref[...], wt_ref[...].astype(BF16), (((1,), (1,)), ((), ())),
                                 preferred_element_type=F32)


def _proj(x, wt_stack, layer, wgt, tm, tn):
    M, K = x.shape
    return pl.pallas_call(
        _proj_body,
        grid=(M // tm, PROJ_MAIN // tn),
        in_specs=[pl.BlockSpec((tm, K), lambda i, j: (i, 0), pipeline_mode=pl.Buffered(1)),
                  pl.BlockSpec((None, tn, K), lambda i, j: (layer, j, 0)),
                  pl.BlockSpec((LANES, K), lambda i, j: (0, 0))],
        out_specs=[pl.BlockSpec((tm, tn), lambda i, j: (i, j)),
                   pl.BlockSpec((tm, LANES), lambda i, j: (i, 0))],
        out_shape=[jax.ShapeDtypeStruct((M, PROJ_MAIN), F32),
                   jax.ShapeDtypeStruct((M, LANES), F32)],
        scratch_shapes=[pltpu.VMEM((tm, K), BF16)],
        compiler_params=_params(("parallel", "arbitrary")),
        name="in_proj",
    )(x, wt_stack, wgt)


def _row_window(total_rows, rows, row0, tm):
    assert rows % tm == 0 and row0 % tm == 0
    return rows // tm, row0 // tm


def _into_operand(into):
    if into is None:
        return [], []
    return [into], [pl.BlockSpec(memory_space=pl.ANY)]


def _outproj_ln_body(y_ref, x_ref, w_ref, g_ref, b_ref, o_ref, *, alpha):
    mix = jnp.dot(y_ref[...].astype(BF16), w_ref[...], preferred_element_type=F32)
    o_ref[...] = _layer_norm(alpha * x_ref[...] + mix, g_ref[...], b_ref[...])


def _outproj_ln(y, x, w_stack, layer, g, b, tm, alpha, x_row0=0):
    M, K = y.shape
    D = w_stack.shape[2]
    n, off = _row_window(M, M, x_row0, tm)
    row = lambda i: (i, 0)
    fixed = lambda i: (0, 0)
    return pl.pallas_call(
        functools.partial(_outproj_ln_body, alpha=alpha),
        grid=(n,),
        in_specs=[pl.BlockSpec((tm, K), row), pl.BlockSpec((tm, D), lambda i: (off + i, 0)),
                  pl.BlockSpec((None, K, D), lambda i: (layer, 0, 0), pipeline_mode=pl.Buffered(1)),
                  pl.BlockSpec((1, D), fixed), pl.BlockSpec((1, D), fixed)],
        out_specs=pl.BlockSpec((tm, D), row),
        out_shape=jax.ShapeDtypeStruct((M, D), F32),
        compiler_params=_params(("parallel",)),
        name="out_proj_ln",
    )(y, x, w_stack, g, b)


def _mlp_pre(f, h_ref, o_ref, hb_ref):
    @pl.when(f == 0)
    def _():
        hb_ref[...] = h_ref[...].astype(BF16)
        o_ref[...] = jnp.zeros_like(o_ref)


def _mlp_main(wu_ref, wd_ref, o_ref, hb_ref):
    up = jnp.dot(hb_ref[...], wu_ref[...].astype(BF16), preferred_element_type=F32)
    act = jnp.square(jnp.maximum(up, 0.0)).astype(BF16)
    o_ref[...] += jnp.dot(act, wd_ref[...].astype(BF16), preferred_element_type=F32)


def _mlp_pieces(wu_ref, wd_ref, o_ref, hb_ref):
    tm, D = hb_ref.shape
    tf = wu_ref.shape[1]
    rb = tm // MLP_ROW_BLOCKS
    acts = {}
    pieces = []
    for r in range(MLP_ROW_BLOCKS):
        rows = slice(r * rb, (r + 1) * rb)
        for n in range(tf // MXU_N):
            def up_piece(r=r, n=n, rows=rows):
                w = wu_ref[:, n * MXU_N:(n + 1) * MXU_N].astype(BF16)
                up = jnp.dot(hb_ref[rows, :], w, preferred_element_type=F32)
                acts[r, n] = jnp.square(jnp.maximum(up, 0.0)).astype(BF16)
            pieces.append(up_piece)
        for k in range(D // MXU_N):
            def down_piece(r=r, k=k, rows=rows):
                act = jnp.concatenate([acts[r, n] for n in range(tf // MXU_N)], axis=1)
                w = wd_ref[:, k * MXU_N:(k + 1) * MXU_N].astype(BF16)
                o_ref[rows, k * MXU_N:(k + 1) * MXU_N] += jnp.dot(act, w, preferred_element_type=F32)
            pieces.append(down_piece)
    return pieces


def _mlp_post(f, nf, h_ref, g_ref, b_ref, o_ref, alpha):
    @pl.when(f == nf - 1)
    def _():
        o_ref[...] = _layer_norm(alpha * h_ref[...] + o_ref[...], g_ref[...], b_ref[...])


def _mlp_ln_body(h_ref, wu_ref, wd_ref, g_ref, b_ref, *rest, alpha):
    o_ref, hb_ref = rest[-2:]
    f = pl.program_id(1)
    _mlp_pre(f, h_ref, o_ref, hb_ref)
    _mlp_main(wu_ref, wd_ref, o_ref, hb_ref)
    _mlp_post(f, pl.num_programs(1), h_ref, g_ref, b_ref, o_ref, alpha)


def _mlp_specs(D, tm, tf, layer, out_off):
    in_specs = [pl.BlockSpec((tm, D), lambda i, f: (i, 0), pipeline_mode=pl.Buffered(1)),
                pl.BlockSpec((None, D, tf), lambda i, f: (layer, 0, f)),
                pl.BlockSpec((None, tf, D), lambda i, f: (layer, f, 0)),
                pl.BlockSpec((1, D), lambda i, f: (0, 0)),
                pl.BlockSpec((1, D), lambda i, f: (0, 0))]
    out_spec = pl.BlockSpec((tm, D), lambda i, f: (out_off + i, 0), pipeline_mode=pl.Buffered(1))
    return in_specs, out_spec, [pltpu.VMEM((tm, D), BF16)]


def _mlp_ln(h, wu_stack, wd_stack, layer, g, b, tm, tf, alpha, out_row0=0, into=None):
    M, D = h.shape
    FF = wu_stack.shape[2]
    n, off = _row_window(M, M, out_row0, tm)
    in_specs, out_spec, scratch = _mlp_specs(D, tm, tf, layer, off)
    extra, extra_specs = _into_operand(into)
    return pl.pallas_call(
        functools.partial(_mlp_ln_body, alpha=alpha),
        grid=(n, FF // tf),
        in_specs=in_specs + extra_specs,
        out_specs=out_spec,
        out_shape=jax.ShapeDtypeStruct(((M if into is None else into.shape[0]), D), F32),
        scratch_shapes=scratch,
        input_output_aliases={5: 0} if extra else {},
        compiler_params=_params(("parallel", "arbitrary")),
        name="mlp_ln",
    )(h, wu_stack, wd_stack, g, b, *extra)


def _ple_body(h_ref, p_ref, wg_ref, wp_ref, o_ref):
    h = h_ref[...]
    gate = _sigmoid(jnp.dot(h.astype(BF16), wg_ref[...].astype(BF16), preferred_element_type=F32))
    emb = jnp.dot(p_ref[...].astype(BF16), wp_ref[...], preferred_element_type=F32)
    o_ref[...] = h + gate * emb


def _ple(h, p_stack, layer, wg_stack, wp_stack, tm):
    M, D = h.shape
    P = p_stack.shape[2]
    row = lambda i: (i, 0)
    return pl.pallas_call(
        _ple_body,
        grid=(M // tm,),
        in_specs=[pl.BlockSpec((tm, D), row),
                  pl.BlockSpec((None, tm, P), lambda i: (layer, i, 0)),
                  pl.BlockSpec((None, D, D), lambda i: (layer, 0, 0), pipeline_mode=pl.Buffered(1)),
                  pl.BlockSpec((None, P, D), lambda i: (layer, 0, 0), pipeline_mode=pl.Buffered(1))],
        out_specs=pl.BlockSpec((tm, D), row),
        out_shape=jax.ShapeDtypeStruct((M, D), F32),
        compiler_params=_params(("parallel",)),
        name="ple_gate",
    )(h, p_stack, wg_stack, wp_stack)


def _mixer_e_body(proj_ref, gate_ref, conv0_ref, s0_ref, h0_ref, wconv_ref, bconv_ref, alog_ref,
                  dtb_ref, nw_ref, wri_ref, br_ref, bi_ref, lam_ref,
                  y_ref, convn_ref, s_ref, hl_ref,
                  cbuf, a_scr, bx_scr, hseq, *, L, seg, t_valid, conv_win,
                  chunk=None, n_chunks=None, overlap=None):
    c = pl.program_id(1) if chunk is None else chunk
    last = (pl.num_programs(1) if n_chunks is None else n_chunks) - 1
    heads = range(HEADS)
    nseg = L // seg
    segs = range(nseg)
    stride = CONV_TAIL + seg

    packed = seg < SUBLANES

    @pl.when(c == 0)
    def _():
        if not packed:
            for s in segs:
                cbuf[s * stride:s * stride + CONV_TAIL, :] = conv0_ref[s * CONV_TAIL:(s + 1) * CONV_TAIL, :]
        s_ref[...] = s0_ref[...]
        hl_ref[...] = h0_ref[...]

    tick = _ticker(overlap, 28)
    same, tril, strict, eye = _seg_masks(L, seg)
    tril_f = jnp.where(tril, 1.0, 0.0).astype(F32)
    same_f = jnp.where(same, 1.0, 0.0).astype(F32)
    row_ok = (lax.broadcasted_iota(jnp.int32, (L, 1), 0) % seg) < t_valid
    s_old = [[s_ref[s, h] for h in heads] for s in segs]
    h_old = [hl_ref[s] for s in segs]

    if packed:
        u = proj_ref[:, 0:CONV_CH]
        lead = [jnp.zeros((seg - (CONV_W - 1), CONV_CH), F32)] if seg > CONV_W - 1 else []
        hist = jnp.concatenate([r for s in segs for r in
                                lead + [conv0_ref[i, s:s + 1, :] for i in range(CONV_W - 1)]], axis=0)
        rowm = lax.broadcasted_iota(jnp.int32, (L, 1), 0) % seg
        acc = bconv_ref[...] + u * wconv_ref[CONV_W - 1:CONV_W, :]
        for j in range(1, CONV_W):
            shifted = jnp.where(rowm >= j, pltpu.roll(u, j, axis=0), pltpu.roll(hist, (j - seg) % L, axis=0))
            acc = acc + shifted * wconv_ref[CONV_W - 1 - j:CONV_W - j, :]
        for i in range(CONV_W - 1):
            convn_ref[i] = jnp.concatenate([u[(s + 1) * seg - (CONV_W - 1) + i:(s + 1) * seg - (CONV_W - 1) + i + 1, :]
                                            for s in segs], axis=0)
    else:
        for s in segs:
            cbuf[s * stride + CONV_TAIL:(s + 1) * stride, :] = proj_ref[s * seg:(s + 1) * seg, 0:CONV_CH]
        full = cbuf[...]
        prev = pltpu.roll(full, 1, axis=0)
        pair_new = full * wconv_ref[3:4, :] + prev * wconv_ref[2:3, :]
        pair_old = full * wconv_ref[1:2, :] + prev * wconv_ref[0:1, :]
        conv_full = bconv_ref[...] + pair_new + pltpu.roll(pair_old, 2, axis=0)
        acc = jnp.concatenate([conv_full[s * stride + CONV_TAIL:(s + 1) * stride, :] for s in segs], axis=0)

    tick()
    gp = gate_ref[...]
    g_all = -jnp.exp(alog_ref[...]) * _softplus(gp + dtb_ref[...])
    beta_all = _sigmoid(gp)
    if t_valid < seg:
        g_all = jnp.where(row_ok, g_all, 0.0)
        beta_all = jnp.where(row_ok, beta_all, 0.0)
    gc_all = _dot_f32(tril_f, g_all)
    gl_all = _dot_f32(same_f, g_all)
    tick()

    q_l, k_l, kb_l, gcol_l, egc_l, y_l = [], [], [], [], [], []
    qa_l = [_silu(acc[:, h * DK:(h + 1) * DK]) for h in heads]
    tick()
    ka_l = [_silu(acc[:, HEADS * DK + h * DK:HEADS * DK + (h + 1) * DK]) for h in heads]
    tick()
    qn_l = [lax.rsqrt(jnp.sum(qa_l[h] * qa_l[h], axis=-1, keepdims=True) + L2_EPS) * (DK ** -0.5)
            for h in heads]
    kn_l = [lax.rsqrt(jnp.sum(ka_l[h] * ka_l[h], axis=-1, keepdims=True) + L2_EPS) for h in heads]
    tick()
    for h in heads:
        q = qa_l[h] * qn_l[h]
        k = ka_l[h] * kn_l[h]
        v = _silu(acc[:, 2 * HEADS * DK + h * A_DV:2 * HEADS * DK + (h + 1) * A_DV])
        gcol = gc_all[:, h:h + 1]
        bcol = beta_all[:, HEADS + h:HEADS + h + 1]
        egc = jnp.exp(gcol)
        kb = k * bcol
        q_l.append(q)
        k_l.append(k)
        kb_l.append(kb)
        gcol_l.append(gcol)
        egc_l.append(egc)
        y_l.append(jnp.concatenate([v * bcol, kb * egc], axis=1))
        if h % 2:
            tick()
    kq_k = [_mm_nt(jnp.concatenate([kb_l[h], q_l[h]], axis=0), k_l[h]) for h in heads]
    tick()
    p_l, qk_l = [], []
    for h in heads:
        grow = _col_to_row(gcol_l[h], eye)
        decay = jnp.where(tril, jnp.exp(jnp.where(tril, gcol_l[h] - grow, 0.0)), 0.0)
        p_l.append(jnp.where(strict, -(kq_k[h][0:L] * decay), 0.0))
        qk_l.append(kq_k[h][L:2 * L] * decay)
        if h % 4 == 3:
            tick()

    n_levels = max(1, math.ceil(math.log2(t_valid)))
    r_l = p_l
    if n_levels >= 2:
        p_l = [_mm(p_l[h], p_l[h]) for h in heads]
        tick()
        for _ in range(n_levels - 2):
            out = [_mm(jnp.concatenate([p_l[h], r_l[h]], axis=0), p_l[h]) for h in heads]
            r_l = [r_l[h] + p_l[h] + out[h][L:2 * L] for h in heads]
            p_l = [out[h][0:L] for h in heads]
            tick()
        out = [_mm(r_l[h], p_l[h]) for h in heads]
        r_l = [r_l[h] + p_l[h] + out[h] for h in heads]
        tick()
    y_l = [y_l[h] + _mm(r_l[h], y_l[h]) for h in heads]
    tick()

    R = max(seg, SUBLANES)
    qe_l = [q_l[h] * egc_l[h] for h in heads]
    wq_s = [[_mm(jnp.concatenate([_seg_tile(y_l[h][:, A_DV:A_DV + DK], s, seg),
                                  _seg_tile(qe_l[h], s, seg)], axis=0), s_old[s][h])
             for h in heads] for s in segs]
    v_new = [y_l[h][:, 0:A_DV] - _seg_join([wq_s[s][h][0:R] for s in segs], seg) for h in heads]
    tick()
    o_intra = [_mm(qk_l[h], v_new[h]) for h in heads]
    tick()
    kd_l = [k_l[h] * jnp.exp(gl_all[:, h:h + 1] - gcol_l[h]) for h in heads]
    s_upd = [[s_old[s][h] * jnp.exp(gl_all[s * seg:s * seg + 1, h:h + 1])
              + _mm_tn(_seg_only(_seg_tile(kd_l[h], s, seg), s, seg), _seg_tile(v_new[h], s, seg))
              for h in heads] for s in segs]

    bw = LRU_W // LRU_BLOCKS
    xr = acc[:, 3 * HEADS * DK:3 * HEADS * DK + LRU_W]
    sp = _softplus(-lam_ref[...])
    for n in range(LRU_BLOCKS):
        sl = slice(n * bw, (n + 1) * bw)
        xb = xr[:, sl]
        ri_pre = _mm(xb, wri_ref[n])
        r_pre = ri_pre[:, 0:bw] + br_ref[:, sl]
        i_pre = ri_pre[:, bw:2 * bw] + bi_ref[:, sl]
        log_a = -LRU_C * _sigmoid(r_pre) * sp[:, sl]
        a = jnp.exp(log_a)
        a_scr[:, sl] = a
        bx_scr[:, sl] = jnp.sqrt(-jnp.tanh(log_a) * (a * a + 1.0)) * _sigmoid(i_pre) * xb
        tick()

    for s in segs:
        hc = h_old[s]
        for t in range(seg):
            row = s * seg + t
            if t < t_valid:
                hc = a_scr[row:row + 1, :] * hc + bx_scr[row:row + 1, :]
            hseq[row:row + 1, :] = hc
        hl_ref[s] = hc
    tick()

    o_l = [_seg_join([wq_s[s][h][R:2 * R] for s in segs], seg) + o_intra[h] for h in heads]
    rms_l = [lax.rsqrt(jnp.mean(o_l[h] * o_l[h], axis=-1, keepdims=True) + RMS_EPS) for h in heads]
    tick()
    for h in heads:
        z = proj_ref[:, CONV_CH + h * A_DV:CONV_CH + (h + 1) * A_DV]
        y_ref[:, h * A_DV:(h + 1) * A_DV] = o_l[h] * rms_l[h] * nw_ref[...] * _silu(z)
    for s in segs:
        for h in heads:
            s_ref[s, h] = s_upd[s][h]
    gate = proj_ref[:, CONV_CH + HEADS * A_DV:CONV_CH + HEADS * A_DV + LRU_W]
    y_ref[:, HEADS * A_DV:HEADS * A_DV + LRU_W] = hseq[...] * _gelu_tanh(gate)
    tick(None)

    if not packed:
        @pl.when(c == last)
        def _():
            for s in segs:
                convn_ref[s * SUBLANES:(s + 1) * SUBLANES, :] = (
                    cbuf[s * stride + conv_win:s * stride + conv_win + SUBLANES, :])

        for s in segs:
            cbuf[s * stride:s * stride + CONV_TAIL, :] = cbuf[s * stride + seg:s * stride + seg + CONV_TAIL, :]


def _mixer_e_parts(proj, gates, conv0, s0, h0, wconv, bconv, alog, dtb, nw, wri, br, bi, lam,
                   *, seq0, nblk, nc, L, seg, t_valid, bc):
    nseg = L // seg
    assert nc == 1 or nseg == 1
    assert t_valid >= CONV_W - 1
    packed = seg < SUBLANES
    if packed:
        assert nc == 1 and t_valid == seg and L % SUBLANES == 0
        hist_rows, conv_win, conv_off = nseg, 0, None
    else:
        conv_row = CONV_TAIL - (CONV_W - 1) + t_valid
        conv_win = (conv_row // SUBLANES) * SUBLANES
        conv_off = conv_row - conv_win
        assert conv_off + CONV_W - 1 <= SUBLANES
        hist_rows = nseg * SUBLANES
    mix = HEADS * A_DV + LRU_W

    def chunk(*g):
        b, c = bc(*g)
        return ((seq0 + b) * nc + c, 0)

    def chunk_out(*g):
        b, c = bc(*g)
        return (b * nc + c, 0)

    blk_in = lambda *g: seq0 + bc(*g)[0]
    blk_out = lambda *g: bc(*g)[0]
    fix2 = lambda *g: (0, 0)
    fix3 = lambda *g: (0, 0, 0)
    return dict(
        body=functools.partial(_mixer_e_body, L=L, seg=seg, t_valid=t_valid, conv_win=conv_win),
        operands=[proj, gates, conv0, s0, h0, wconv, bconv, alog, dtb, nw, wri, br, bi, lam],
        in_specs=[pl.BlockSpec((L, PROJ_MAIN), chunk),
                  pl.BlockSpec((L, LANES), chunk),
                  (pl.BlockSpec((CONV_W - 1, nseg, CONV_CH), lambda *g: (0, blk_in(*g), 0)) if packed else
                   pl.BlockSpec((hist_rows, CONV_CH), lambda *g: (blk_in(*g), 0))),
                  pl.BlockSpec((nseg, HEADS, DK, A_DV), lambda *g: (blk_in(*g), 0, 0, 0)),
                  pl.BlockSpec((nseg, 1, LRU_W), lambda *g: (blk_in(*g), 0, 0)),
                  pl.BlockSpec((CONV_W, CONV_CH), fix2),
                  pl.BlockSpec((1, CONV_CH), fix2),
                  pl.BlockSpec((1, LANES), fix2),
                  pl.BlockSpec((1, LANES), fix2),
                  pl.BlockSpec((1, A_DV), fix2),
                  pl.BlockSpec((LRU_BLOCKS, LRU_W // LRU_BLOCKS, 2 * LRU_W // LRU_BLOCKS), fix3),
                  pl.BlockSpec((1, LRU_W), fix2),
                  pl.BlockSpec((1, LRU_W), fix2),
                  pl.BlockSpec((1, LRU_W), fix2)],
        out_specs=[pl.BlockSpec((L, mix), chunk_out),
                   (pl.BlockSpec((CONV_W - 1, nseg, CONV_CH), lambda *g: (0, blk_out(*g), 0)) if packed else
                    pl.BlockSpec((hist_rows, CONV_CH), lambda *g: (blk_out(*g), 0))),
                   pl.BlockSpec((nseg, HEADS, DK, A_DV), lambda *g: (blk_out(*g), 0, 0, 0)),
                   pl.BlockSpec((nseg, 1, LRU_W), lambda *g: (blk_out(*g), 0, 0))],
        out_shape=[jax.ShapeDtypeStruct((nblk * nc * L, mix), F32),
                   jax.ShapeDtypeStruct(((CONV_W - 1, nblk * nseg, CONV_CH) if packed else
                                         (nblk * hist_rows, CONV_CH)), F32),
                   jax.ShapeDtypeStruct((nblk * nseg, HEADS, DK, A_DV), F32),
                   jax.ShapeDtypeStruct((nblk * nseg, 1, LRU_W), F32)],
        scratch_shapes=[pltpu.VMEM((nseg * (CONV_TAIL + seg), CONV_CH), F32),
                        pltpu.VMEM((L, LRU_W), F32),
                        pltpu.VMEM((L, LRU_W), F32),
                        pltpu.VMEM((L, LRU_W), F32)],
        name="delta_lru_mixer", conv_off=conv_off)


def _mixer_o_body(proj_ref, gate_ref, c0_ref, n0_ref, m0_ref, big_ref, bfg_ref, nw_ref,
                  y_ref, c_ref, n_ref, m_ref, *, L, seg, t_valid, chunk=None, n_chunks=None, overlap=None):
    ci = pl.program_id(1) if chunk is None else chunk
    heads = range(HEADS)
    nseg = L // seg
    segs = range(nseg)

    @pl.when(ci == 0)
    def _():
        c_ref[...] = c0_ref[...]
        n_ref[...] = n0_ref[...]
        m_ref[...] = m0_ref[...]

    tick = _ticker(overlap, 16)
    same, tril, _, eye = _seg_masks(L, seg)
    tril_f = jnp.where(tril, 1.0, 0.0).astype(F32)
    same_f = jnp.where(same, 1.0, 0.0).astype(F32)
    row_id = lax.broadcasted_iota(jnp.int32, (L, 1), 0)
    row_ok = (row_id % seg) < t_valid
    c_old = [[c_ref[s, h] for h in heads] for s in segs]
    n_old = [n_ref[s] for s in segs]
    m_old = [m_ref[s] for s in segs]

    gp = gate_ref[...]
    ig_all = gp + big_ref[...]
    logf_all = -_softplus(-(gp + bfg_ref[...]))
    if t_valid < seg:
        ig_all = jnp.where(row_ok, ig_all, NEG_BIG)
        logf_all = jnp.where(row_ok, logf_all, 0.0)
    bcum_all = _dot_f32(tril_f, logf_all)
    blast_all = _dot_f32(same_f, logf_all)
    mrow_all = _seg_rows(m_old, L, seg)
    tick()

    q_l = [proj_ref[:, h * DK:(h + 1) * DK] for h in heads]
    k_l = [proj_ref[:, HEADS * DK + h * DK:HEADS * DK + (h + 1) * DK] * (DK ** -0.5) for h in heads]
    v_l = [proj_ref[:, 2 * HEADS * DK + h * C_DV:2 * HEADS * DK + (h + 1) * C_DV] for h in heads]
    qk_l = [_mm_nt(q_l[h], k_l[h]) for h in heads]
    tick()
    qc_l = [_seg_join([_mm_nt(_seg_tile(q_l[h], s, seg), c_old[s][h]) for s in segs], seg) for h in heads]
    tick()

    s_l, e_l, mt_l, sc_l, ws_l, mnew_l = [], [], [], [], [], []
    for h in heads:
        bcol = bcum_all[:, HEADS + h:HEADS + h + 1]
        blast = blast_all[:, HEADS + h:HEADS + h + 1]
        icol = ig_all[:, h:h + 1]
        mcol = mrow_all[:, h:h + 1]
        brow = _col_to_row(bcol, eye)
        irow = _col_to_row(icol, eye)
        d_intra = jnp.where(tril, bcol - brow + irow, NEG_BIG)
        ge_mat = jnp.where(same, blast - brow + irow, NEG_BIG)
        ge_col = blast - bcol + icol
        inter = bcol + mcol
        m_t = jnp.maximum(inter, jnp.max(d_intra, axis=-1, keepdims=True))
        m_new = jnp.maximum(blast + mcol, jnp.max(ge_mat, axis=-1, keepdims=True))
        e_l.append(jnp.exp(inter - m_t))
        mt_l.append(m_t)
        s_l.append(qk_l[h] * jnp.exp(d_intra - m_t))
        sc_l.append(jnp.exp(blast + mcol - m_new))
        ws_l.append(jnp.exp(ge_col - m_new))
        mnew_l.append(m_new)
        tick()

    sv = [_mm(s_l[h], v_l[h]) for h in heads]
    tick()
    vw_l = [v_l[h] * ws_l[h] for h in heads]
    kw_l = [k_l[h] * ws_l[h] for h in heads]
    kv = [[_mm_tn(_seg_only(_seg_tile(vw_l[h], s, seg), s, seg), _seg_tile(k_l[h], s, seg)) for h in heads]
          for s in segs]

    nrow_l = [_seg_rows([n_old[s][h:h + 1, :] for s in segs], L, seg) for h in heads]
    den_l = [e_l[h] * jnp.sum(q_l[h] * nrow_l[h], axis=-1, keepdims=True)
             + jnp.sum(s_l[h], axis=-1, keepdims=True) for h in heads]
    inv_l = [1.0 / jnp.maximum(jnp.abs(den_l[h]), jnp.exp(-mt_l[h])) for h in heads]
    tick()
    hh_l = [(e_l[h] * qc_l[h] + sv[h]) * inv_l[h] for h in heads]
    rms_l = [lax.rsqrt(jnp.mean(hh_l[h] * hh_l[h], axis=-1, keepdims=True) + RMS_EPS) for h in heads]
    tick()
    for h in heads:
        o_pre = proj_ref[:, 2 * HEADS * DK + HEADS * C_DV + h * C_DV:
                         2 * HEADS * DK + HEADS * C_DV + (h + 1) * C_DV]
        y_ref[:, h * C_DV:(h + 1) * C_DV] = (hh_l[h] * rms_l[h] * nw_ref[:, h * C_DV:(h + 1) * C_DV]
                                             * _sigmoid(o_pre))
        if h % 4 == 3:
            tick()
    lane = lax.broadcasted_iota(jnp.int32, (1, LANES), 1)
    for s in segs:
        r0 = s * seg
        m_row = m_old[s]
        n_rows = []
        for h in heads:
            sc = sc_l[h][r0:r0 + 1, :]
            c_ref[s, h] = sc * c_old[s][h] + kv[s][h]
            n_rows.append(sc * n_old[s][h:h + 1, :]
                          + jnp.sum(_seg_only(_seg_tile(kw_l[h], s, seg), s, seg), axis=0, keepdims=True))
            m_row = jnp.where(lane == h, mnew_l[h][r0:r0 + 1, :], m_row)
        n_ref[s] = jnp.concatenate(n_rows, axis=0)
        m_ref[s] = m_row
    tick(None)


def _mixer_o_parts(proj, gates, c0, n0, m0, big, bfg, nw, *, seq0, nblk, nc, L, seg, t_valid, bc):
    nseg = L // seg
    assert nc == 1 or nseg == 1
    cv = HEADS * C_DV

    def chunk(*g):
        b, c = bc(*g)
        return ((seq0 + b) * nc + c, 0)

    def chunk_out(*g):
        b, c = bc(*g)
        return (b * nc + c, 0)

    blk_in = lambda *g: seq0 + bc(*g)[0]
    blk_out = lambda *g: bc(*g)[0]
    fix2 = lambda *g: (0, 0)
    return dict(
        body=functools.partial(_mixer_o_body, L=L, seg=seg, t_valid=t_valid),
        operands=[proj, gates, c0, n0, m0, big, bfg, nw],
        in_specs=[pl.BlockSpec((L, PROJ_MAIN), chunk),
                  pl.BlockSpec((L, LANES), chunk),
                  pl.BlockSpec((nseg, HEADS, C_DV, DK), lambda *g: (blk_in(*g), 0, 0, 0)),
                  pl.BlockSpec((nseg, HEADS, DK), lambda *g: (blk_in(*g), 0, 0)),
                  pl.BlockSpec((nseg, 1, LANES), lambda *g: (blk_in(*g), 0, 0)),
                  pl.BlockSpec((1, LANES), fix2),
                  pl.BlockSpec((1, LANES), fix2),
                  pl.BlockSpec((1, cv), fix2)],
        out_specs=[pl.BlockSpec((L, cv), chunk_out),
                   pl.BlockSpec((nseg, HEADS, C_DV, DK), lambda *g: (blk_out(*g), 0, 0, 0)),
                   pl.BlockSpec((nseg, HEADS, DK), lambda *g: (blk_out(*g), 0, 0)),
                   pl.BlockSpec((nseg, 1, LANES), lambda *g: (blk_out(*g), 0, 0))],
        out_shape=[jax.ShapeDtypeStruct((nblk * nc * L, cv), F32),
                   jax.ShapeDtypeStruct((nblk * nseg, HEADS, C_DV, DK), F32),
                   jax.ShapeDtypeStruct((nblk * nseg, HEADS, DK), F32),
                   jax.ShapeDtypeStruct((nblk * nseg, 1, LANES), F32)],
        scratch_shapes=[],
        name="mlstm_mixer")


def _run_mixer(parts, nblk, nc):
    return pl.pallas_call(
        parts['body'],
        grid=(nblk, nc),
        in_specs=parts['in_specs'],
        out_specs=parts['out_specs'],
        out_shape=parts['out_shape'],
        scratch_shapes=parts['scratch_shapes'],
        compiler_params=_params(("parallel", "arbitrary")),
        name=parts['name'],
    )(*parts['operands'])


def _fused_body(*refs, mixer_body, n_mi, n_mo, n_ms, alpha):
    mi = refs[:n_mi]
    h_ref, wu_ref, wd_ref, g_ref, b_ref = refs[n_mi:n_mi + 5]
    outs = refs[n_mi + 6:]
    mo = outs[:n_mo]
    o_ref = outs[n_mo]
    ms = outs[n_mo + 1:n_mo + 1 + n_ms]
    hb_ref = outs[n_mo + 1 + n_ms]
    i, f = pl.program_id(0), pl.program_id(1)
    nf = pl.num_programs(1)
    _mlp_pre(f, h_ref, o_ref, hb_ref)
    mixer_body(*mi, *mo, *ms, chunk=i * nf + f, n_chunks=pl.num_programs(0) * nf,
               overlap=lambda: _mlp_pieces(wu_ref, wd_ref, o_ref, hb_ref))
    _mlp_post(f, nf, h_ref, g_ref, b_ref, o_ref, alpha)


def _run_mixer_mlp(parts, tiles, nf, h, wu_stack, wd_stack, layer, g, b, tm, tf, alpha, out_row0, h2_into):
    M, D = h.shape
    assert M == tiles * tm
    mlp_in, mlp_out, mlp_scr = _mlp_specs(D, tm, tf, layer, out_row0 // tm)
    n_mi, n_mo, n_ms = len(parts['operands']), len(parts['out_specs']), len(parts['scratch_shapes'])
    outs = pl.pallas_call(
        functools.partial(_fused_body, mixer_body=parts['body'], n_mi=n_mi, n_mo=n_mo, n_ms=n_ms, alpha=alpha),
        grid=(tiles, nf),
        in_specs=parts['in_specs'] + mlp_in + [pl.BlockSpec(memory_space=pl.ANY)],
        out_specs=parts['out_specs'] + [mlp_out],
        out_shape=parts['out_shape'] + [jax.ShapeDtypeStruct(h2_into.shape, F32)],
        scratch_shapes=parts['scratch_shapes'] + mlp_scr,
        input_output_aliases={n_mi + 5: n_mo},
        compiler_params=_params(("arbitrary", "arbitrary")),
        name=parts['name'] + "_mlp",
    )(*parts['operands'], h, wu_stack, wd_stack, g, b, h2_into)
    return outs[:n_mo], outs[n_mo]


def _pad_lanes(v, offset=0):
    return jnp.pad(v.astype(F32), (offset, LANES - offset - v.shape[0]))[None, :]


def _gate_weight_t(wt):
    wg = wt[PROJ_MAIN:, :]
    return jnp.pad(wg, ((0, LANES - wg.shape[0]), (0, 0)))


def _row_tile(m, cap):
    t = cap
    while t >= SUBLANES:
        if m % t == 0:
            return t
        t //= 2
    raise ValueError(f"row count {m} is not a multiple of {SUBLANES}")


def _blocking(B, T):
    t_valid = math.gcd(T, CHUNK)
    nc = T // t_valid
    if SUBLANES % t_valid == 0 and nc == 1:
        seg = t_valid
    else:
        seg = -(-t_valid // SUBLANES) * SUBLANES
    assert nc == 1 or seg == t_valid
    nseg = 1
    if nc == 1:
        nseg = max(1, min(B, SEQS_PER_BLOCK))
        while B % nseg or (nseg * seg) % SUBLANES:
            nseg -= 1
        assert nseg >= 1
    return t_valid, seg, nseg * seg, B // nseg, nc


def _group(x, p, conv, delta, lru, mc, mn, mm, W, depth):
    B, T, D = x.shape
    t_valid, seg, L, nblk, nc = _blocking(B, T)
    Tp = nc * seg
    if Tp != T:
        x = jnp.pad(x, ((0, 0), (0, Tp - T), (0, 0)))
        p = jnp.pad(p, ((0, 0), (0, 0), (0, Tp - T), (0, 0)))
    M = B * Tp
    FF = W['w_up'].shape[2]
    tiles = max(1, Tp // TM_MLP)
    pipelined = B > 1 and nc > 1 and nc % tiles == 0 and Tp % tiles == 0 and FF % (nc // tiles) == 0
    nf = nc // tiles if pipelined else FF // MLP_TF
    if pipelined:
        tm_mlp, tf = Tp // tiles, FF // nf
        pipelined = tf % LANES == 0 and tm_mlp % SUBLANES == 0
    if not pipelined:
        tm_mlp, tf, nf = _row_tile(M, TM_MLP), MLP_TF, FF // MLP_TF
    return dict(
        B=B, T=T, Tp=Tp, M=M, D=D, FF=FF, depth=depth, nblk=nblk, nc=nc, tiles=tiles, nf=nf,
        pipelined=pipelined, tm=_row_tile(Tp if pipelined else M, TM_SMALL), tm_proj=_row_tile(M, TM_PROJ),
        tm_mlp=tm_mlp, tf=tf, blk=dict(nc=nc, L=L, seg=seg, t_valid=t_valid),
        hist=seg if seg < SUBLANES else CONV_TAIL,
        x2=x.reshape(M, D), p2=p.reshape(depth, M, -1), conv=conv, delta=delta, lru=lru, mc=mc, mn=mn, mm=mm,
        spare=None, out=dict(conv=[], delta=[], lru=[], mc=[], mn=[], mm=[]))


def _layer_front(G, W, layer):
    j = layer // 2
    c = dict(j=j, layer=layer,
             ln1=(W['ln1_g'][layer][None, :], W['ln1_b'][layer][None, :]),
             ln2=(W['ln2_g'][layer][None, :], W['ln2_b'][layer][None, :]))
    if layer % 2 == 0:
        proj, gates = _proj(G['x2'], W['w_in_e'], j, W['wg_e'][j], G['tm_proj'], PROJ_TN)
        hist = G['hist']
        if G['blk']['seg'] < SUBLANES:
            conv0 = jnp.transpose(G['conv'][j], (1, 0, 2))
        else:
            conv0 = jnp.pad(G['conv'][j], ((0, 0), (hist - (CONV_W - 1), 0), (0, 0))).reshape(G['B'] * hist, CONV_CH)
        c['make_parts'] = functools.partial(
            _mixer_e_parts, proj, gates, conv0, G['delta'][j],
            G['lru'][j][:, None, :], W['w_conv_e'][j], W['b_conv_e'][j][None, :],
            _pad_lanes(W['a_log_e'][j]), _pad_lanes(W['dt_bias_e'][j]),
            W['delta_norm_e'][j][None, :], W['lru_wri_e'][j], W['lru_br_e'][j][None, :],
            W['lru_bi_e'][j][None, :], W['lru_lambda_e'][j][None, :], **G['blk'])
        c['w_out'] = W['w_out_e']
    else:
        proj, gates = _proj(G['x2'], W['w_in_o'], j, W['wg_o'][j], G['tm_proj'], PROJ_TN)
        m0 = jnp.pad(G['mm'][j], ((0, 0), (0, LANES - HEADS)))[:, None, :]
        c['make_parts'] = functools.partial(
            _mixer_o_parts, proj, gates, G['mc'][j], G['mn'][j], m0,
            _pad_lanes(W['b_ig_o'][j]), _pad_lanes(W['b_fg_o'][j], HEADS),
            W['mlstm_norm_o'][j][None, :], **G['blk'])
        c['w_out'] = W['w_out_o']
    return c


def _plain_mixer(G, c, alpha):
    parts = c['make_parts'](seq0=0, nblk=G['nblk'], bc=lambda b, ch: (b, ch))
    outs = _run_mixer(parts, G['nblk'], G['blk']['nc'])
    c['states'], c['conv_off'] = outs[1:], parts.get('conv_off')
    return _outproj_ln(outs[0], G['x2'], c['w_out'], c['j'], *c['ln1'], G['tm'], alpha)


def _mlp(G, c, W, h1, alpha):
    return _mlp_ln(h1, W['w_up'], W['w_down'], c['layer'], *c['ln2'], G['tm_mlp'], G['tf'], alpha)


def _pipelined_layer(G, c, W, alpha, guest=None):
    B, Tp, nc, tiles, nf = G['B'], G['Tp'], G['blk']['nc'], G['tiles'], G['nf']
    layer, (g2, b2) = c['layer'], c['ln2']
    h = jnp.zeros((G['M'], G['D']), F32) if G['spare'] is None else G['spare']
    h1 = guest_out = None
    per_seq = []
    for b in range(B):
        fused_bc = lambda i, f, n=(guest[4] if b == 0 and guest else nf): (0, i * n + f)
        if b == 0 and guest is None:
            parts = c['make_parts'](seq0=b, nblk=1, bc=lambda blk_id, ch: (0, ch))
            outs = _run_mixer(parts, 1, nc)
        elif b == 0:
            g_h1, g_tm, g_tf, g_tiles, g_nf = guest
            parts = c['make_parts'](seq0=b, nblk=1, bc=fused_bc)
            outs, guest_out = _run_mixer_mlp(parts, g_tiles, g_nf, g_h1, W['w_up'], W['w_down'], layer, g2, b2,
                                             g_tm, g_tf, alpha, 0, jnp.zeros(g_h1.shape, F32))
        else:
            parts = c['make_parts'](seq0=b, nblk=1, bc=fused_bc)
            outs, h = _run_mixer_mlp(parts, tiles, nf, h1, W['w_up'], W['w_down'], layer, g2, b2,
                                     G['tm_mlp'], G['tf'], alpha, (b - 1) * Tp, h)
        per_seq.append(outs[1:])
        h1 = _outproj_ln(outs[0], G['x2'], c['w_out'], c['j'], *c['ln1'], G['tm'], alpha, x_row0=b * Tp)
    h = _mlp_ln(h1, W['w_up'], W['w_down'], layer, g2, b2, G['tm_mlp'], G['tf'], alpha,
                out_row0=(B - 1) * Tp, into=h)
    c['states'] = [jnp.concatenate([st[k] for st in per_seq], axis=0) for k in range(len(per_seq[0]))]
    c['conv_off'] = parts.get('conv_off')
    return h, guest_out


def _layer_back(G, c, W, h):
    st, out = c['states'], G['out']
    if c['layer'] % 2 == 0:
        off = c['conv_off']
        out['conv'].append(jnp.transpose(st[0], (1, 0, 2)) if off is None else
                           st[0].reshape(G['B'], G['hist'], CONV_CH)[:, off:off + CONV_W - 1])
        out['delta'].append(st[1])
        out['lru'].append(st[2][:, 0])
    else:
        out['mc'].append(st[0])
        out['mn'].append(st[1])
        out['mm'].append(st[2][:, 0, :HEADS])
    G['x2'] = _ple(h, G['p2'], c['layer'], W['w_ple_gate'], W['w_ple'], G['tm'])
    G['spare'] = h


def _results(G):
    out = G['out']
    y = G['x2'].reshape(G['B'], G['Tp'], G['D'])[:, :G['T']]
    return (y,) + tuple(jnp.stack(out[k]) for k in ('conv', 'delta', 'lru', 'mc', 'mn', 'mm'))


def kernel(x_prompt, x_sample, p_prompt, p_sample, state_conv, state_delta, state_lru,
           state_mlstm_c, state_mlstm_n, state_mlstm_m, w_in_e, w_conv_e, b_conv_e, a_log_e,
           dt_bias_e, delta_norm_e, lru_wr_e, lru_br_e, lru_wi_e, lru_bi_e, lru_lambda_e, w_out_e,
           w_in_o, b_ig_o, b_fg_o, mlstm_norm_o, w_out_o, ln1_g, ln1_b, ln2_g, ln2_b, w_up, w_down,
           w_ple, w_ple_gate):
    depth = ln1_g.shape[0]
    n_even, n_odd = w_in_e.shape[0], w_in_o.shape[0]
    alpha = (2 * depth) ** 0.25
    wt_e = jnp.swapaxes(w_in_e, 1, 2)
    wt_o = jnp.swapaxes(w_in_o, 1, 2)
    W = dict(
        w_in_e=wt_e, w_in_o=wt_o,
        wg_e=[_gate_weight_t(wt_e[j]) for j in range(n_even)],
        wg_o=[_gate_weight_t(wt_o[j]) for j in range(n_odd)],
        w_out_e=w_out_e.astype(BF16), w_out_o=w_out_o.astype(BF16),
        w_up=w_up, w_down=w_down,
        w_ple=w_ple.astype(BF16), w_ple_gate=w_ple_gate,
        w_conv_e=w_conv_e, b_conv_e=b_conv_e, a_log_e=a_log_e, dt_bias_e=dt_bias_e,
        delta_norm_e=delta_norm_e, lru_br_e=lru_br_e,
        lru_wri_e=jnp.concatenate([lru_wr_e, lru_wi_e], axis=-1).astype(BF16),
        lru_bi_e=lru_bi_e, lru_lambda_e=lru_lambda_e, b_ig_o=b_ig_o, b_fg_o=b_fg_o,
        mlstm_norm_o=mlstm_norm_o, ln1_g=ln1_g, ln1_b=ln1_b, ln2_g=ln2_g, ln2_b=ln2_b)
    bp = x_prompt.shape[0]
    zeros = lambda a: jnp.zeros((a.shape[0], bp) + a.shape[2:], F32)
    P = _group(x_prompt, p_prompt, zeros(state_conv), zeros(state_delta), zeros(state_lru),
               zeros(state_mlstm_c), zeros(state_mlstm_n), zeros(state_mlstm_m), W, depth)
    S = _group(x_sample, p_sample, state_conv, state_delta, state_lru,
               state_mlstm_c, state_mlstm_n, state_mlstm_m, W, depth)
    for layer in range(depth):
        cs = _layer_front(S, W, layer)
        h1_s = _plain_mixer(S, cs, alpha)
        cp = _layer_front(P, W, layer)
        if P['pipelined']:
            tiles_s = S['M'] // S['tm_mlp']
            nf_s = P['blk']['nc'] // tiles_s if P['blk']['nc'] % tiles_s == 0 else 0
            ok = nf_s > 0 and S['FF'] % nf_s == 0 and (S['FF'] // nf_s) % MXU_N == 0
            guest = (h1_s, S['tm_mlp'], S['FF'] // nf_s, tiles_s, nf_s) if ok else None
            h_p, h_s = _pipelined_layer(P, cp, W, alpha, guest)
        else:
            h_p, h_s = _mlp(P, cp, W, _plain_mixer(P, cp, alpha), alpha), None
        if h_s is None:
            h_s = _mlp(S, cs, W, h1_s, alpha)
        _layer_back(S, cs, W, h_s)
        _layer_back(P, cp, W, h_p)
    out_p, out_s = _results(P), _results(S)
    return (out_p[0], out_s[0]) + tuple(out_p[1:]) + tuple(out_s[1:])
```

```python
import functools
import math

import jax
import jax.numpy as jnp
from jax import lax
from jax.experimental import pallas as pl
from jax.experimental.pallas import tpu as pltpu

F32 = jnp.float32
BF16 = jnp.bfloat16
HI = lax.Precision.HIGHEST

LANES = 128
SUBLANES = 8
VMEM_LIMIT = 60 * 1024 * 1024

HEADS = 8
DK = 128
A_DV = 128
C_DV = 256
LRU_W = 1024
LRU_BLOCKS = 8
LRU_C = 8.0
CONV_W = 4
CONV_CH = 4096
CHUNK = 64
LN_EPS = 1e-5
RMS_EPS = 1e-6
L2_EPS = 1e-6
PROJ_MAIN = 6144
PROJ_TN = 512
MLP_TF = 512
TM_PROJ = 2048
TM_MLP = 1024
TM_SMALL = 512
MXU_N = 256
MLP_ROW_BLOCKS = 2
CONV_TAIL = SUBLANES
SEQS_PER_BLOCK = 8
NEG_BIG = -1e30


def _dot_f32(a, b):
    return lax.dot_general(a, b, (((1,), (0,)), ((), ())), precision=HI,
                           preferred_element_type=F32)


def _mm(a, b):
    return lax.dot_general(a.astype(BF16), b.astype(BF16), (((1,), (0,)), ((), ())),
                           preferred_element_type=F32)


def _mm_nt(a, b):
    return lax.dot_general(a.astype(BF16), b.astype(BF16), (((1,), (1,)), ((), ())),
                           preferred_element_type=F32)


def _mm_tn(a, b):
    return lax.dot_general(a.astype(BF16), b.astype(BF16), (((0,), (0,)), ((), ())),
                           preferred_element_type=F32)


def _sigmoid(x):
    return 1.0 / (1.0 + jnp.exp(-x))


def _silu(x):
    return x * _sigmoid(x)


def _softplus(x):
    return jnp.maximum(x, 0.0) + jnp.log1p(jnp.exp(-jnp.abs(x)))


def _gelu_tanh(x):
    return 0.5 * x * (1.0 + jnp.tanh(math.sqrt(2.0 / math.pi) * (x + 0.044715 * (x * x * x))))


def _layer_norm(x, g, b):
    mu = jnp.mean(x, axis=-1, keepdims=True)
    xc = x - mu
    var = jnp.mean(xc * xc, axis=-1, keepdims=True)
    return xc * lax.rsqrt(var + LN_EPS) * g + b


def _col_to_row(col, eye):
    return jnp.sum(jnp.where(eye, col, 0.0), axis=0, keepdims=True)


def _seg_masks(L, seg):
    ri = lax.broadcasted_iota(jnp.int32, (L, L), 0)
    ci = lax.broadcasted_iota(jnp.int32, (L, L), 1)
    same = (ri // seg) == (ci // seg)
    return same, same & (ri >= ci), same & (ri > ci), ri == ci


def _seg_tile(a, s, seg):
    if seg >= SUBLANES:
        return a[s * seg:(s + 1) * seg]
    t = (s * seg) // SUBLANES
    return a[t * SUBLANES:(t + 1) * SUBLANES]


def _seg_only(a_tile, s, seg):
    if seg >= SUBLANES:
        return a_tile
    row = lax.broadcasted_iota(jnp.int32, (SUBLANES, 1), 0)
    lo = (s * seg) % SUBLANES
    return jnp.where((row >= lo) & (row < lo + seg), a_tile, 0.0)


def _seg_join(parts, seg):
    if seg >= SUBLANES:
        return jnp.concatenate(parts, axis=0)
    per_tile = SUBLANES // seg
    row = lax.broadcasted_iota(jnp.int32, (SUBLANES, 1), 0)
    tiles = []
    for t in range(len(parts) // per_tile):
        tile = parts[t * per_tile]
        for i in range(1, per_tile):
            tile = jnp.where(row >= i * seg, parts[t * per_tile + i], tile)
        tiles.append(tile)
    return jnp.concatenate(tiles, axis=0)


def _seg_rows(vals, L, seg):
    if seg >= SUBLANES:
        return jnp.concatenate([jnp.broadcast_to(v, (seg, v.shape[1])) for v in vals], axis=0)
    seg_id = lax.broadcasted_iota(jnp.int32, (L, 1), 0) // seg
    out = jnp.broadcast_to(vals[0], (L, vals[0].shape[1]))
    for i in range(1, len(vals)):
        out = jnp.where(seg_id == i, vals[i], out)
    return out


def _ticker(overlap, n_ticks, early=False):
    pending = list(overlap()) if overlap is not None else []
    total = len(pending)
    state = [0, 0]

    def tick(count=1):
        state[0] += 0 if count is None else count
        share = -(-state[0] * total // n_ticks) if early else (state[0] * total) // n_ticks
        target = total if count is None else min(total, share)
        while state[1] < target:
            pending[state[1]]()
            state[1] += 1
    return tick


def _params(sem):
    return pltpu.CompilerParams(dimension_semantics=sem, vmem_limit_bytes=VMEM_LIMIT)


def _proj_body(x_ref, wt_ref, wgt_ref, o_ref, og_ref, xb_ref):
    @pl.when(pl.program_id(1) == 0)
    def _():
        xb = x_ref[...].astype(BF16)
        xb_ref[...] = xb
        og_ref[...] = lax.dot_general(xb, wgt_ref[...].astype(BF16), (((1,), (1,)), ((), ())),
                                      preferred_element_type=F32)

    o_ref[...] = lax.dot_general(xb_ref[...], wt_ref[...].astype(BF16), (((1,), (1,)), ((), ())),
                                 preferred_element_type=F32)


def _proj(x, wt_stack, layer, wgt, tm, tn):
    M, K = x.shape
    return pl.pallas_call(
        _proj_body,
        grid=(M // tm, PROJ_MAIN // tn),
        in_specs=[pl.BlockSpec((tm, K), lambda i, j: (i, 0), pipeline_mode=pl.Buffered(1)),
                  pl.BlockSpec((None, tn, K), lambda i, j: (layer, j, 0)),
                  pl.BlockSpec((LANES, K), lambda i, j: (0, 0))],
        out_specs=[pl.BlockSpec((tm, tn), lambda i, j: (i, j)),
                   pl.BlockSpec((tm, LANES), lambda i, j: (i, 0))],
        out_shape=[jax.ShapeDtypeStruct((M, PROJ_MAIN), F32),
                   jax.ShapeDtypeStruct((M, LANES), F32)],
        scratch_shapes=[pltpu.VMEM((tm, K), BF16)],
        compiler_params=_params(("parallel", "arbitrary")),
        name="in_proj",
    )(x, wt_stack, wgt)


def _row_window(total_rows, rows, row0, tm):
    assert rows % tm == 0 and row0 % tm == 0
    return rows // tm, row0 // tm


def _into_operand(into):
    if into is None:
        return [], []
    return [into], [pl.BlockSpec(memory_space=pl.ANY)]


def _outproj_ln_body(y_ref, x_ref, w_ref, g_ref, b_ref, o_ref, *, alpha):
    mix = jnp.dot(y_ref[...].astype(BF16), w_ref[...], preferred_element_type=F32)
    o_ref[...] = _layer_norm(alpha * x_ref[...] + mix, g_ref[...], b_ref[...])


def _outproj_ln(y, x, w_stack, layer, g, b, tm, alpha, x_row0=0):
    M, K = y.shape
    D = w_stack.shape[2]
    n, off = _row_window(M, M, x_row0, tm)
    row = lambda i: (i, 0)
    fixed = lambda i: (0, 0)
    return pl.pallas_call(
        functools.partial(_outproj_ln_body, alpha=alpha),
        grid=(n,),
        in_specs=[pl.BlockSpec((tm, K), row), pl.BlockSpec((tm, D), lambda i: (off + i, 0)),
                  pl.BlockSpec((None, K, D), lambda i: (layer, 0, 0), pipeline_mode=pl.Buffered(1)),
                  pl.BlockSpec((1, D), fixed), pl.BlockSpec((1, D), fixed)],
        out_specs=pl.BlockSpec((tm, D), row),
        out_shape=jax.ShapeDtypeStruct((M, D), F32),
        compiler_params=_params(("parallel",)),
        name="out_proj_ln",
    )(y, x, w_stack, g, b)


def _mlp_pre(f, h_ref, o_ref, hb_ref):
    @pl.when(f == 0)
    def _():
        hb_ref[...] = h_ref[...].astype(BF16)
        o_ref[...] = jnp.zeros_like(o_ref)


def _mlp_main(wu_ref, wd_ref, o_ref, hb_ref):
    up = jnp.dot(hb_ref[...], wu_ref[...].astype(BF16), preferred_element_type=F32)
    act = jnp.square(jnp.maximum(up, 0.0)).astype(BF16)
    o_ref[...] += jnp.dot(act, wd_ref[...].astype(BF16), preferred_element_type=F32)


def _mlp_pieces(wu_ref, wd_ref, o_ref, hb_ref):
    tm, D = hb_ref.shape
    tf = wu_ref.shape[1]
    rb = tm // MLP_ROW_BLOCKS
    acts = {}
    pieces = []
    for r in range(MLP_ROW_BLOCKS):
        rows = slice(r * rb, (r + 1) * rb)
        for n in range(tf // MXU_N):
            def up_piece(r=r, n=n, rows=rows):
                w = wu_ref[:, n * MXU_N:(n + 1) * MXU_N].astype(BF16)
                up = jnp.dot(hb_ref[rows, :], w, preferred_element_type=F32)
                acts[r, n] = jnp.square(jnp.maximum(up, 0.0)).astype(BF16)
            pieces.append(up_piece)
        for k in range(D // MXU_N):
            def down_piece(r=r, k=k, rows=rows):
                act = jnp.concatenate([acts[r, n] for n in range(tf // MXU_N)], axis=1)
                w = wd_ref[:, k * MXU_N:(k + 1) * MXU_N].astype(BF16)
                o_ref[rows, k * MXU_N:(k + 1) * MXU_N] += jnp.dot(act, w, preferred_element_type=F32)
            pieces.append(down_piece)
    return pieces


def _mlp_post(f, nf, h_ref, g_ref, b_ref, o_ref, alpha):
    @pl.when(f == nf - 1)
    def _():
        o_ref[...] = _layer_norm(alpha * h_ref[...] + o_ref[...], g_ref[...], b_ref[...])


def _mlp_ln_body(h_ref, wu_ref, wd_ref, g_ref, b_ref, *rest, alpha):
    o_ref, hb_ref = rest[-2:]
    f = pl.program_id(1)
    _mlp_pre(f, h_ref, o_ref, hb_ref)
    _mlp_main(wu_ref, wd_ref, o_ref, hb_ref)
    _mlp_post(f, pl.num_programs(1), h_ref, g_ref, b_ref, o_ref, alpha)


def _mlp_specs(D, tm, tf, layer, out_off):
    in_specs = [pl.BlockSpec((tm, D), lambda i, f: (i, 0), pipeline_mode=pl.Buffered(1)),
                pl.BlockSpec((None, D, tf), lambda i, f: (layer, 0, f)),
                pl.BlockSpec((None, tf, D), lambda i, f: (layer, f, 0)),
                pl.BlockSpec((1, D), lambda i, f: (0, 0)),
                pl.BlockSpec((1, D), lambda i, f: (0, 0))]
    out_spec = pl.BlockSpec((tm, D), lambda i, f: (out_off + i, 0), pipeline_mode=pl.Buffered(1))
    return in_specs, out_spec, [pltpu.VMEM((tm, D), BF16)]


def _mlp_ln(h, wu_stack, wd_stack, layer, g, b, tm, tf, alpha, out_row0=0, into=None):
    M, D = h.shape
    FF = wu_stack.shape[2]
    n, off = _row_window(M, M, out_row0, tm)
    in_specs, out_spec, scratch = _mlp_specs(D, tm, tf, layer, off)
    extra, extra_specs = _into_operand(into)
    return pl.pallas_call(
        functools.partial(_mlp_ln_body, alpha=alpha),
        grid=(n, FF // tf),
        in_specs=in_specs + extra_specs,
        out_specs=out_spec,
        out_shape=jax.ShapeDtypeStruct(((M if into is None else into.shape[0]), D), F32),
        scratch_shapes=scratch,
        input_output_aliases={5: 0} if extra else {},
        compiler_params=_params(("parallel", "arbitrary")),
        name="mlp_ln",
    )(h, wu_stack, wd_stack, g, b, *extra)


def _ple_body(h_ref, p_ref, wg_ref, wp_ref, o_ref):
    h = h_ref[...]
    gate = _sigmoid(jnp.dot(h.astype(BF16), wg_ref[...].astype(BF16), preferred_element_type=F32))
    emb = jnp.dot(p_ref[...].astype(BF16), wp_ref[...], preferred_element_type=F32)
    o_ref[...] = h + gate * emb


def _ple(h, p_stack, layer, wg_stack, wp_stack, tm):
    M, D = h.shape
    P = p_stack.shape[2]
    row = lambda i: (i, 0)
    return pl.pallas_call(
        _ple_body,
        grid=(M // tm,),
        in_specs=[pl.BlockSpec((tm, D), row),
                  pl.BlockSpec((None, tm, P), lambda i: (layer, i, 0)),
                  pl.BlockSpec((None, D, D), lambda i: (layer, 0, 0), pipeline_mode=pl.Buffered(1)),
                  pl.BlockSpec((None, P, D), lambda i: (layer, 0, 0), pipeline_mode=pl.Buffered(1))],
        out_specs=pl.BlockSpec((tm, D), row),
        out_shape=jax.ShapeDtypeStruct((M, D), F32),
        compiler_params=_params(("parallel",)),
        name="ple_gate",
    )(h, p_stack, wg_stack, wp_stack)


def _mixer_e_body(proj_ref, gate_ref, conv0_ref, s0_ref, h0_ref, wconv_ref, bconv_ref, alog_ref,
                  dtb_ref, nw_ref, wri_ref, br_ref, bi_ref, lam_ref,
                  y_ref, convn_ref, s_ref, hl_ref,
                  cbuf, a_scr, bx_scr, hseq, *, L, seg, t_valid, conv_win,
                  chunk=None, n_chunks=None, overlap=None):
    c = pl.program_id(1) if chunk is None else chunk
    last = (pl.num_programs(1) if n_chunks is None else n_chunks) - 1
    heads = range(HEADS)
    nseg = L // seg
    segs = range(nseg)
    stride = CONV_TAIL + seg

    packed = seg < SUBLANES

    @pl.when(c == 0)
    def _():
        if not packed:
            for s in segs:
                cbuf[s * stride:s * stride + CONV_TAIL, :] = conv0_ref[s * CONV_TAIL:(s + 1) * CONV_TAIL, :]
        s_ref[...] = s0_ref[...]
        hl_ref[...] = h0_ref[...]

    tick = _ticker(overlap, 32, early=True)
    tick()
    same, tril, strict, eye = _seg_masks(L, seg)
    tril_f = jnp.where(tril, 1.0, 0.0).astype(F32)
    same_f = jnp.where(same, 1.0, 0.0).astype(F32)
    row_ok = (lax.broadcasted_iota(jnp.int32, (L, 1), 0) % seg) < t_valid
    s_old = [[s_ref[s, h] for h in heads] for s in segs]
    h_old = [hl_ref[s] for s in segs]

    if packed:
        u = proj_ref[:, 0:CONV_CH]
        lead = [jnp.zeros((seg - (CONV_W - 1), CONV_CH), F32)] if seg > CONV_W - 1 else []
        hist = jnp.concatenate([r for s in segs for r in
                                lead + [conv0_ref[i, s:s + 1, :] for i in range(CONV_W - 1)]], axis=0)
        rowm = lax.broadcasted_iota(jnp.int32, (L, 1), 0) % seg
        acc = bconv_ref[...] + u * wconv_ref[CONV_W - 1:CONV_W, :]
        for j in range(1, CONV_W):
            shifted = jnp.where(rowm >= j, pltpu.roll(u, j, axis=0), pltpu.roll(hist, (j - seg) % L, axis=0))
            acc = acc + shifted * wconv_ref[CONV_W - 1 - j:CONV_W - j, :]
        for i in range(CONV_W - 1):
            convn_ref[i] = jnp.concatenate([u[(s + 1) * seg - (CONV_W - 1) + i:(s + 1) * seg - (CONV_W - 1) + i + 1, :]
                                            for s in segs], axis=0)
    else:
        for s in segs:
            cbuf[s * stride + CONV_TAIL:(s + 1) * stride, :] = proj_ref[s * seg:(s + 1) * seg, 0:CONV_CH]
        full = cbuf[...]
        prev = pltpu.roll(full, 1, axis=0)
        tick()
        pair_new = full * wconv_ref[3:4, :] + prev * wconv_ref[2:3, :]
        tick()
        pair_old = full * wconv_ref[1:2, :] + prev * wconv_ref[0:1, :]
        tick()
        conv_full = bconv_ref[...] + pair_new + pltpu.roll(pair_old, 2, axis=0)
        acc = jnp.concatenate([conv_full[s * stride + CONV_TAIL:(s + 1) * stride, :] for s in segs], axis=0)

    tick()
    gp = gate_ref[...]
    g_all = -jnp.exp(alog_ref[...]) * _softplus(gp + dtb_ref[...])
    beta_all = _sigmoid(gp)
    if t_valid < seg:
        g_all = jnp.where(row_ok, g_all, 0.0)
        beta_all = jnp.where(row_ok, beta_all, 0.0)
    gc_all = _dot_f32(tril_f, g_all)
    gl_all = _dot_f32(same_f, g_all)
    tick()

    q_l, k_l, kb_l, gcol_l, egc_l, y_l = [], [], [], [], [], []
    qa_l = [_silu(acc[:, h * DK:(h + 1) * DK]) for h in heads]
    tick()
    ka_l = [_silu(acc[:, HEADS * DK + h * DK:HEADS * DK + (h + 1) * DK]) for h in heads]
    tick()
    qn_l = [lax.rsqrt(jnp.sum(qa_l[h] * qa_l[h], axis=-1, keepdims=True) + L2_EPS) * (DK ** -0.5)
            for h in heads]
    kn_l = [lax.rsqrt(jnp.sum(ka_l[h] * ka_l[h], axis=-1, keepdims=True) + L2_EPS) for h in heads]
    tick()
    for h in heads:
        q = qa_l[h] * qn_l[h]
        k = ka_l[h] * kn_l[h]
        v = _silu(acc[:, 2 * HEADS * DK + h * A_DV:2 * HEADS * DK + (h + 1) * A_DV])
        gcol = gc_all[:, h:h + 1]
        bcol = beta_all[:, HEADS + h:HEADS + h + 1]
        egc = jnp.exp(gcol)
        kb = k * bcol
        q_l.append(q)
        k_l.append(k)
        kb_l.append(kb)
        gcol_l.append(gcol)
        egc_l.append(egc)
        y_l.append(jnp.concatenate([v * bcol, kb * egc], axis=1))
        if h % 2:
            tick()
    kq_k = [_mm_nt(jnp.concatenate([kb_l[h], q_l[h]], axis=0), k_l[h]) for h in heads]
    tick()
    p_l, qk_l = [], []
    for h in heads:
        grow = _col_to_row(gcol_l[h], eye)
        decay = jnp.where(tril, jnp.exp(jnp.where(tril, gcol_l[h] - grow, 0.0)), 0.0)
        p_l.append(jnp.where(strict, -(kq_k[h][0:L] * decay), 0.0))
        qk_l.append(kq_k[h][L:2 * L] * decay)
        if h % 4 == 3:
            tick()

    n_levels = max(1, math.ceil(math.log2(t_valid)))
    r_l = p_l
    if n_levels >= 2:
        p_l = [_mm(p_l[h], p_l[h]) for h in heads]
        tick()
        for _ in range(n_levels - 2):
            out = [_mm(jnp.concatenate([p_l[h], r_l[h]], axis=0), p_l[h]) for h in heads]
            r_l = [r_l[h] + p_l[h] + out[h][L:2 * L] for h in heads]
            p_l = [out[h][0:L] for h in heads]
            tick()
        out = [_mm(r_l[h], p_l[h]) for h in heads]
        r_l = [r_l[h] + p_l[h] + out[h] for h in heads]
        tick()
    y_l = [y_l[h] + _mm(r_l[h], y_l[h]) for h in heads]
    tick()

    R = max(seg, SUBLANES)
    qe_l = [q_l[h] * egc_l[h] for h in heads]
    wq_s = [[_mm(jnp.concatenate([_seg_tile(y_l[h][:, A_DV:A_DV + DK], s, seg),
                                  _seg_tile(qe_l[h], s, seg)], axis=0), s_old[s][h])
             for h in heads] for s in segs]
    v_new = [y_l[h][:, 0:A_DV] - _seg_join([wq_s[s][h][0:R] for s in segs], seg) for h in heads]
    tick()
    o_intra = [_mm(qk_l[h], v_new[h]) for h in heads]
    tick()
    kd_l = [k_l[h] * jnp.exp(gl_all[:, h:h + 1] - gcol_l[h]) for h in heads]
    s_upd = [[s_old[s][h] * jnp.exp(gl_all[s * seg:s * seg + 1, h:h + 1])
              + _mm_tn(_seg_only(_seg_tile(kd_l[h], s, seg), s, seg), _seg_tile(v_new[h], s, seg))
              for h in heads] for s in segs]

    bw = LRU_W // LRU_BLOCKS
    xr = acc[:, 3 * HEADS * DK:3 * HEADS * DK + LRU_W]
    sp = _softplus(-lam_ref[...])
    for n in range(LRU_BLOCKS):
        sl = slice(n * bw, (n + 1) * bw)
        xb = xr[:, sl]
        ri_pre = _mm(xb, wri_ref[n])
        r_pre = ri_pre[:, 0:bw] + br_ref[:, sl]
        i_pre = ri_pre[:, bw:2 * bw] + bi_ref[:, sl]
        log_a = -LRU_C * _sigmoid(r_pre) * sp[:, sl]
        a = jnp.exp(log_a)
        a_scr[:, sl] = a
        bx_scr[:, sl] = jnp.sqrt(-jnp.tanh(log_a) * (a * a + 1.0)) * _sigmoid(i_pre) * xb
        tick()

    for s in segs:
        hc = h_old[s]
        for t in range(seg):
            row = s * seg + t
            if t < t_valid:
                hc = a_scr[row:row + 1, :] * hc + bx_scr[row:row + 1, :]
            hseq[row:row + 1, :] = hc
        hl_ref[s] = hc
    tick()

    o_l = [_seg_join([wq_s[s][h][R:2 * R] for s in segs], seg) + o_intra[h] for h in heads]
    rms_l = [lax.rsqrt(jnp.mean(o_l[h] * o_l[h], axis=-1, keepdims=True) + RMS_EPS) for h in heads]
    tick()
    for h in heads:
        z = proj_ref[:, CONV_CH + h * A_DV:CONV_CH + (h + 1) * A_DV]
        y_ref[:, h * A_DV:(h + 1) * A_DV] = o_l[h] * rms_l[h] * nw_ref[...] * _silu(z)
    for s in segs:
        for h in heads:
            s_ref[s, h] = s_upd[s][h]
    gate = proj_ref[:, CONV_CH + HEADS * A_DV:CONV_CH + HEADS * A_DV + LRU_W]
    y_ref[:, HEADS * A_DV:HEADS * A_DV + LRU_W] = hseq[...] * _gelu_tanh(gate)
    tick(None)

    if not packed:
        @pl.when(c == last)
        def _():
            for s in segs:
                convn_ref[s * SUBLANES:(s + 1) * SUBLANES, :] = (
                    cbuf[s * stride + conv_win:s * stride + conv_win + SUBLANES, :])

        for s in segs:
            cbuf[s * stride:s * stride + CONV_TAIL, :] = cbuf[s * stride + seg:s * stride + seg + CONV_TAIL, :]


def _mixer_e_parts(proj, gates, conv0, s0, h0, wconv, bconv, alog, dtb, nw, wri, br, bi, lam,
                   *, seq0, nblk, nc, L, seg, t_valid, bc):
    nseg = L // seg
    assert nc == 1 or nseg == 1
    assert t_valid >= CONV_W - 1
    packed = seg < SUBLANES
    if packed:
        assert nc == 1 and t_valid == seg and L % SUBLANES == 0
        hist_rows, conv_win, conv_off = nseg, 0, None
    else:
        conv_row = CONV_TAIL - (CONV_W - 1) + t_valid
        conv_win = (conv_row // SUBLANES) * SUBLANES
        conv_off = conv_row - conv_win
        assert conv_off + CONV_W - 1 <= SUBLANES
        hist_rows = nseg * SUBLANES
    mix = HEADS * A_DV + LRU_W

    def chunk(*g):
        b, c = bc(*g)
        return ((seq0 + b) * nc + c, 0)

    def chunk_out(*g):
        b, c = bc(*g)
        return (b * nc + c, 0)

    blk_in = lambda *g: seq0 + bc(*g)[0]
    blk_out = lambda *g: bc(*g)[0]
    fix2 = lambda *g: (0, 0)
    fix3 = lambda *g: (0, 0, 0)
    return dict(
        body=functools.partial(_mixer_e_body, L=L, seg=seg, t_valid=t_valid, conv_win=conv_win),
        operands=[proj, gates, conv0, s0, h0, wconv, bconv, alog, dtb, nw, wri, br, bi, lam],
        in_specs=[pl.BlockSpec((L, PROJ_MAIN), chunk),
                  pl.BlockSpec((L, LANES), chunk),
                  (pl.BlockSpec((CONV_W - 1, nseg, CONV_CH), lambda *g: (0, blk_in(*g), 0)) if packed else
                   pl.BlockSpec((hist_rows, CONV_CH), lambda *g: (blk_in(*g), 0))),
                  pl.BlockSpec((nseg, HEADS, DK, A_DV), lambda *g: (blk_in(*g), 0, 0, 0)),
                  pl.BlockSpec((nseg, 1, LRU_W), lambda *g: (blk_in(*g), 0, 0)),
                  pl.BlockSpec((CONV_W, CONV_CH), fix2),
                  pl.BlockSpec((1, CONV_CH), fix2),
                  pl.BlockSpec((1, LANES), fix2),
                  pl.BlockSpec((1, LANES), fix2),
                  pl.BlockSpec((1, A_DV), fix2),
                  pl.BlockSpec((LRU_BLOCKS, LRU_W // LRU_BLOCKS, 2 * LRU_W // LRU_BLOCKS), fix3),
                  pl.BlockSpec((1, LRU_W), fix2),
                  pl.BlockSpec((1, LRU_W), fix2),
                  pl.BlockSpec((1, LRU_W), fix2)],
        out_specs=[pl.BlockSpec((L, mix), chunk_out),
                   (pl.BlockSpec((CONV_W - 1, nseg, CONV_CH), lambda *g: (0, blk_out(*g), 0)) if packed else
                    pl.BlockSpec((hist_rows, CONV_CH), lambda *g: (blk_out(*g), 0))),
                   pl.BlockSpec((nseg, HEADS, DK, A_DV), lambda *g: (blk_out(*g), 0, 0, 0)),
                   pl.BlockSpec((nseg, 1, LRU_W), lambda *g: (blk_out(*g), 0, 0))],
        out_shape=[jax.ShapeDtypeStruct((nblk * nc * L, mix), F32),
                   jax.ShapeDtypeStruct(((CONV_W - 1, nblk * nseg, CONV_CH) if packed else
                                         (nblk * hist_rows, CONV_CH)), F32),
                   jax.ShapeDtypeStruct((nblk * nseg, HEADS, DK, A_DV), F32),
                   jax.ShapeDtypeStruct((nblk * nseg, 1, LRU_W), F32)],
        scratch_shapes=[pltpu.VMEM((nseg * (CONV_TAIL + seg), CONV_CH), F32),
                        pltpu.VMEM((L, LRU_W), F32),
                        pltpu.VMEM((L, LRU_W), F32),
                        pltpu.VMEM((L, LRU_W), F32)],
        name="delta_lru_mixer", conv_off=conv_off)


def _mixer_o_body(proj_ref, gate_ref, c0_ref, n0_ref, m0_ref, big_ref, bfg_ref, nw_ref,
                  y_ref, c_ref, n_ref, m_ref, *, L, seg, t_valid, chunk=None, n_chunks=None, overlap=None):
    ci = pl.program_id(1) if chunk is None else chunk
    heads = range(HEADS)
    nseg = L // seg
    segs = range(nseg)

    @pl.when(ci == 0)
    def _():
        c_ref[...] = c0_ref[...]
        n_ref[...] = n0_ref[...]
        m_ref[...] = m0_ref[...]

    tick = _ticker(overlap, 16)
    same, tril, _, eye = _seg_masks(L, seg)
    tril_f = jnp.where(tril, 1.0, 0.0).astype(F32)
    same_f = jnp.where(same, 1.0, 0.0).astype(F32)
    row_id = lax.broadcasted_iota(jnp.int32, (L, 1), 0)
    row_ok = (row_id % seg) < t_valid
    c_old = [[c_ref[s, h] for h in heads] for s in segs]
    n_old = [n_ref[s] for s in segs]
    m_old = [m_ref[s] for s in segs]

    gp = gate_ref[...]
    ig_all = gp + big_ref[...]
    logf_all = -_softplus(-(gp + bfg_ref[...]))
    if t_valid < seg:
        ig_all = jnp.where(row_ok, ig_all, NEG_BIG)
        logf_all = jnp.where(row_ok, logf_all, 0.0)
    bcum_all = _dot_f32(tril_f, logf_all)
    blast_all = _dot_f32(same_f, logf_all)
    mrow_all = _seg_rows(m_old, L, seg)
    tick()

    q_l = [proj_ref[:, h * DK:(h + 1) * DK] for h in heads]
    k_l = [proj_ref[:, HEADS * DK + h * DK:HEADS * DK + (h + 1) * DK] * (DK ** -0.5) for h in heads]
    v_l = [proj_ref[:, 2 * HEADS * DK + h * C_DV:2 * HEADS * DK + (h + 1) * C_DV] for h in heads]
    qk_l = [_mm_nt(q_l[h], k_l[h]) for h in heads]
    tick()
    qc_l = [_seg_join([_mm_nt(_seg_tile(q_l[h], s, seg), c_old[s][h]) for s in segs], seg) for h in heads]
    tick()

    s_l, e_l, mt_l, sc_l, ws_l, mnew_l = [], [], [], [], [], []
    for h in heads:
        bcol = bcum_all[:, HEADS + h:HEADS + h + 1]
        blast = blast_all[:, HEADS + h:HEADS + h + 1]
        icol = ig_all[:, h:h + 1]
        mcol = mrow_all[:, h:h + 1]
        brow = _col_to_row(bcol, eye)
        irow = _col_to_row(icol, eye)
        d_intra = jnp.where(tril, bcol - brow + irow, NEG_BIG)
        ge_mat = jnp.where(same, blast - brow + irow, NEG_BIG)
        ge_col = blast - bcol + icol
        inter = bcol + mcol
        m_t = jnp.maximum(inter, jnp.max(d_intra, axis=-1, keepdims=True))
        m_new = jnp.maximum(blast + mcol, jnp.max(ge_mat, axis=-1, keepdims=True))
        e_l.append(jnp.exp(inter - m_t))
        mt_l.append(m_t)
        s_l.append(qk_l[h] * jnp.exp(d_intra - m_t))
        sc_l.append(jnp.exp(blast + mcol - m_new))
        ws_l.append(jnp.exp(ge_col - m_new))
        mnew_l.append(m_new)
        tick()

    sv = [_mm(s_l[h], v_l[h]) for h in heads]
    tick()
    vw_l = [v_l[h] * ws_l[h] for h in heads]
    kw_l = [k_l[h] * ws_l[h] for h in heads]
    kv = [[_mm_tn(_seg_only(_seg_tile(vw_l[h], s, seg), s, seg), _seg_tile(k_l[h], s, seg)) for h in heads]
          for s in segs]

    nrow_l = [_seg_rows([n_old[s][h:h + 1, :] for s in segs], L, seg) for h in heads]
    den_l = [e_l[h] * jnp.sum(q_l[h] * nrow_l[h], axis=-1, keepdims=True)
             + jnp.sum(s_l[h], axis=-1, keepdims=True) for h in heads]
    inv_l = [1.0 / jnp.maximum(jnp.abs(den_l[h]), jnp.exp(-mt_l[h])) for h in heads]
    tick()
    hh_l = [(e_l[h] * qc_l[h] + sv[h]) * inv_l[h] for h in heads]
    rms_l = [lax.rsqrt(jnp.mean(hh_l[h] * hh_l[h], axis=-1, keepdims=True) + RMS_EPS) for h in heads]
    tick()
    for h in heads:
        o_pre = proj_ref[:, 2 * HEADS * DK + HEADS * C_DV + h * C_DV:
                         2 * HEADS * DK + HEADS * C_DV + (h + 1) * C_DV]
        y_ref[:, h * C_DV:(h + 1) * C_DV] = (hh_l[h] * rms_l[h] * nw_ref[:, h * C_DV:(h + 1) * C_DV]
                                             * _sigmoid(o_pre))
        if h % 4 == 3:
            tick()
    lane = lax.broadcasted_iota(jnp.int32, (1, LANES), 1)
    for s in segs:
        r0 = s * seg
        m_row = m_old[s]
        n_rows = []
        for h in heads:
            sc = sc_l[h][r0:r0 + 1, :]
            c_ref[s, h] = sc * c_old[s][h] + kv[s][h]
            n_rows.append(sc * n_old[s][h:h + 1, :]
                          + jnp.sum(_seg_only(_seg_tile(kw_l[h], s, seg), s, seg), axis=0, keepdims=True))
            m_row = jnp.where(lane == h, mnew_l[h][r0:r0 + 1, :], m_row)
        n_ref[s] = jnp.concatenate(n_rows, axis=0)
        m_ref[s] = m_row
    tick(None)


def _mixer_o_parts(proj, gates, c0, n0, m0, big, bfg, nw, *, seq0, nblk, nc, L, seg, t_valid, bc):
    nseg = L // seg
    assert nc == 1 or nseg == 1
    cv = HEADS * C_DV

    def chunk(*g):
        b, c = bc(*g)
        return ((seq0 + b) * nc + c, 0)

    def chunk_out(*g):
        b, c = bc(*g)
        return (b * nc + c, 0)

    blk_in = lambda *g: seq0 + bc(*g)[0]
    blk_out = lambda *g: bc(*g)[0]
    fix2 = lambda *g: (0, 0)
    return dict(
        body=functools.partial(_mixer_o_body, L=L, seg=seg, t_valid=t_valid),
        operands=[proj, gates, c0, n0, m0, big, bfg, nw],
        in_specs=[pl.BlockSpec((L, PROJ_MAIN), chunk),
                  pl.BlockSpec((L, LANES), chunk),
                  pl.BlockSpec((nseg, HEADS, C_DV, DK), lambda *g: (blk_in(*g), 0, 0, 0)),
                  pl.BlockSpec((nseg, HEADS, DK), lambda *g: (blk_in(*g), 0, 0)),
                  pl.BlockSpec((nseg, 1, LANES), lambda *g: (blk_in(*g), 0, 0)),
                  pl.BlockSpec((1, LANES), fix2),
                  pl.BlockSpec((1, LANES), fix2),
                  pl.BlockSpec((1, cv), fix2)],
        out_specs=[pl.BlockSpec((L, cv), chunk_out),
                   pl.BlockSpec((nseg, HEADS, C_DV, DK), lambda *g: (blk_out(*g), 0, 0, 0)),
                   pl.BlockSpec((nseg, HEADS, DK), lambda *g: (blk_out(*g), 0, 0)),
                   pl.BlockSpec((nseg, 1, LANES), lambda *g: (blk_out(*g), 0, 0))],
        out_shape=[jax.ShapeDtypeStruct((nblk * nc * L, cv), F32),
                   jax.ShapeDtypeStruct((nblk * nseg, HEADS, C_DV, DK), F32),
                   jax.ShapeDtypeStruct((nblk * nseg, HEADS, DK), F32),
                   jax.ShapeDtypeStruct((nblk * nseg, 1, LANES), F32)],
        scratch_shapes=[],
        name="mlstm_mixer")


def _run_mixer(parts, nblk, nc):
    return pl.pallas_call(
        parts['body'],
        grid=(nblk, nc),
        in_specs=parts['in_specs'],
        out_specs=parts['out_specs'],
        out_shape=parts['out_shape'],
        scratch_shapes=parts['scratch_shapes'],
        compiler_params=_params(("parallel", "arbitrary")),
        name=parts['name'],
    )(*parts['operands'])


def _fused_body(*refs, mixer_body, n_mi, n_mo, n_ms, alpha):
    mi = refs[:n_mi]
    h_ref, wu_ref, wd_ref, g_ref, b_ref = refs[n_mi:n_mi + 5]
    outs = refs[n_mi + 6:]
    mo = outs[:n_mo]
    o_ref = outs[n_mo]
    ms = outs[n_mo + 1:n_mo + 1 + n_ms]
    hb_ref = outs[n_mo + 1 + n_ms]
    i, f = pl.program_id(0), pl.program_id(1)
    nf = pl.num_programs(1)
    _mlp_pre(f, h_ref, o_ref, hb_ref)
    mixer_body(*mi, *mo, *ms, chunk=i * nf + f, n_chunks=pl.num_programs(0) * nf,
               overlap=lambda: _mlp_pieces(wu_ref, wd_ref, o_ref, hb_ref))
    _mlp_post(f, nf, h_ref, g_ref, b_ref, o_ref, alpha)


def _run_mixer_mlp(parts, tiles, nf, h, wu_stack, wd_stack, layer, g, b, tm, tf, alpha, out_row0, h2_into):
    M, D = h.shape
    assert M == tiles * tm
    mlp_in, mlp_out, mlp_scr = _mlp_specs(D, tm, tf, layer, out_row0 // tm)
    n_mi, n_mo, n_ms = len(parts['operands']), len(parts['out_specs']), len(parts['scratch_shapes'])
    outs = pl.pallas_call(
        functools.partial(_fused_body, mixer_body=parts['body'], n_mi=n_mi, n_mo=n_mo, n_ms=n_ms, alpha=alpha),
        grid=(tiles, nf),
        in_specs=parts['in_specs'] + mlp_in + [pl.BlockSpec(memory_space=pl.ANY)],
        out_specs=parts['out_specs'] + [mlp_out],
        out_shape=parts['out_shape'] + [jax.ShapeDtypeStruct(h2_into.shape, F32)],
        scratch_shapes=parts['scratch_shapes'] + mlp_scr,
        input_output_aliases={n_mi + 5: n_mo},
        compiler_params=_params(("arbitrary", "arbitrary")),
        name=parts['name'] + "_mlp",
    )(*parts['operands'], h, wu_stack, wd_stack, g, b, h2_into)
    return outs[:n_mo], outs[n_mo]


def _pad_lanes(v, offset=0):
    return jnp.pad(v.astype(F32), (offset, LANES - offset - v.shape[0]))[None, :]


def _gate_weight_t(wt):
    wg = wt[PROJ_MAIN:, :]
    return jnp.pad(wg, ((0, LANES - wg.shape[0]), (0, 0)))


def _row_tile(m, cap):
    t = cap
    while t >= SUBLANES:
        if m % t == 0:
            return t
        t //= 2
    raise ValueError(f"row count {m} is not a multiple of {SUBLANES}")


def _blocking(B, T):
    t_valid = math.gcd(T, CHUNK)
    nc = T // t_valid
    if SUBLANES % t_valid == 0 and nc == 1:
        seg = t_valid
    else:
        seg = -(-t_valid // SUBLANES) * SUBLANES
    assert nc == 1 or seg == t_valid
    nseg = 1
    if nc == 1:
        nseg = max(1, min(B, SEQS_PER_BLOCK))
        while B % nseg or (nseg * seg) % SUBLANES:
            nseg -= 1
        assert nseg >= 1
    return t_valid, seg, nseg * seg, B // nseg, nc


def _group(x, p, conv, delta, lru, mc, mn, mm, W, depth):
    B, T, D = x.shape
    t_valid, seg, L, nblk, nc = _blocking(B, T)
    Tp = nc * seg
    if Tp != T:
        x = jnp.pad(x, ((0, 0), (0, Tp - T), (0, 0)))
        p = jnp.pad(p, ((0, 0), (0, 0), (0, Tp - T), (0, 0)))
    M = B * Tp
    FF = W['w_up'].shape[2]
    tiles = max(1, Tp // TM_MLP)
    pipelined = B > 1 and nc > 1 and nc % tiles == 0 and Tp % tiles == 0 and FF % (nc // tiles) == 0
    nf = nc // tiles if pipelined else FF // MLP_TF
    if pipelined:
        tm_mlp, tf = Tp // tiles, FF // nf
        pipelined = tf % LANES == 0 and tm_mlp % SUBLANES == 0
    if not pipelined:
        tm_mlp, tf, nf = _row_tile(M, TM_MLP), MLP_TF, FF // MLP_TF
    return dict(
        B=B, T=T, Tp=Tp, M=M, D=D, FF=FF, depth=depth, nblk=nblk, nc=nc, tiles=tiles, nf=nf,
        pipelined=pipelined, tm=_row_tile(Tp if pipelined else M, TM_SMALL), tm_proj=_row_tile(M, TM_PROJ),
        tm_mlp=tm_mlp, tf=tf, blk=dict(nc=nc, L=L, seg=seg, t_valid=t_valid),
        hist=seg if seg < SUBLANES else CONV_TAIL,
        x2=x.reshape(M, D), p2=p.reshape(depth, M, -1), conv=conv, delta=delta, lru=lru, mc=mc, mn=mn, mm=mm,
        spare=None, out=dict(conv=[], delta=[], lru=[], mc=[], mn=[], mm=[]))


def _layer_front(G, W, layer):
    j = layer // 2
    c = dict(j=j, layer=layer,
             ln1=(W['ln1_g'][layer][None, :], W['ln1_b'][layer][None, :]),
             ln2=(W['ln2_g'][layer][None, :], W['ln2_b'][layer][None, :]))
    if layer % 2 == 0:
        proj, gates = _proj(G['x2'], W['w_in_e'], j, W['wg_e'][j], G['tm_proj'], PROJ_TN)
        hist = G['hist']
        if G['blk']['seg'] < SUBLANES:
            conv0 = jnp.transpose(G['conv'][j], (1, 0, 2))
        else:
            conv0 = jnp.pad(G['conv'][j], ((0, 0), (hist - (CONV_W - 1), 0), (0, 0))).reshape(G['B'] * hist, CONV_CH)
        c['make_parts'] = functools.partial(
            _mixer_e_parts, proj, gates, conv0, G['delta'][j],
            G['lru'][j][:, None, :], W['w_conv_e'][j], W['b_conv_e'][j][None, :],
            _pad_lanes(W['a_log_e'][j]), _pad_lanes(W['dt_bias_e'][j]),
            W['delta_norm_e'][j][None, :], W['lru_wri_e'][j], W['lru_br_e'][j][None, :],
            W['lru_bi_e'][j][None, :], W['lru_lambda_e'][j][None, :], **G['blk'])
        c['w_out'] = W['w_out_e']
    else:
        proj, gates = _proj(G['x2'], W['w_in_o'], j, W['wg_o'][j], G['tm_proj'], PROJ_TN)
        m0 = jnp.pad(G['mm'][j], ((0, 0), (0, LANES - HEADS)))[:, None, :]
        c['make_parts'] = functools.partial(
            _mixer_o_parts, proj, gates, G['mc'][j], G['mn'][j], m0,
            _pad_lanes(W['b_ig_o'][j]), _pad_lanes(W['b_fg_o'][j], HEADS),
            W['mlstm_norm_o'][j][None, :], **G['blk'])
        c['w_out'] = W['w_out_o']
    return c


def _plain_mixer(G, c, alpha):
    parts = c['make_parts'](seq0=0, nblk=G['nblk'], bc=lambda b, ch: (b, ch))
    outs = _run_mixer(parts, G['nblk'], G['blk']['nc'])
    c['states'], c['conv_off'] = outs[1:], parts.get('conv_off')
    return _outproj_ln(outs[0], G['x2'], c['w_out'], c['j'], *c['ln1'], G['tm'], alpha)


def _mlp(G, c, W, h1, alpha):
    return _mlp_ln(h1, W['w_up'], W['w_down'], c['layer'], *c['ln2'], G['tm_mlp'], G['tf'], alpha)


def _pipelined_layer(G, c, W, alpha, guest=None):
    B, Tp, nc, tiles, nf = G['B'], G['Tp'], G['blk']['nc'], G['tiles'], G['nf']
    layer, (g2, b2) = c['layer'], c['ln2']
    h = jnp.zeros((G['M'], G['D']), F32) if G['spare'] is None else G['spare']
    h1 = guest_out = None
    per_seq = []
    for b in range(B):
        fused_bc = lambda i, f, n=(guest[4] if b == 0 and guest else nf): (0, i * n + f)
        if b == 0 and guest is None:
            parts = c['make_parts'](seq0=b, nblk=1, bc=lambda blk_id, ch: (0, ch))
            outs = _run_mixer(parts, 1, nc)
        elif b == 0:
            g_h1, g_tm, g_tf, g_tiles, g_nf = guest
            parts = c['make_parts'](seq0=b, nblk=1, bc=fused_bc)
            outs, guest_out = _run_mixer_mlp(parts, g_tiles, g_nf, g_h1, W['w_up'], W['w_down'], layer, g2, b2,
                                             g_tm, g_tf, alpha, 0, jnp.zeros(g_h1.shape, F32))
        else:
            parts = c['make_parts'](seq0=b, nblk=1, bc=fused_bc)
            outs, h = _run_mixer_mlp(parts, tiles, nf, h1, W['w_up'], W['w_down'], layer, g2, b2,
                                     G['tm_mlp'], G['tf'], alpha, (b - 1) * Tp, h)
        per_seq.append(outs[1:])
        h1 = _outproj_ln(outs[0], G['x2'], c['w_out'], c['j'], *c['ln1'], G['tm'], alpha, x_row0=b * Tp)
    h = _mlp_ln(h1, W['w_up'], W['w_down'], layer, g2, b2, G['tm_mlp'], G['tf'], alpha,
                out_row0=(B - 1) * Tp, into=h)
    c['states'] = [jnp.concatenate([st[k] for st in per_seq], axis=0) for k in range(len(per_seq[0]))]
    c['conv_off'] = parts.get('conv_off')
    return h, guest_out


def _layer_back(G, c, W, h):
    st, out = c['states'], G['out']
    if c['layer'] % 2 == 0:
        off = c['conv_off']
        out['conv'].append(jnp.transpose(st[0], (1, 0, 2)) if off is None else
                           st[0].reshape(G['B'], G['hist'], CONV_CH)[:, off:off + CONV_W - 1])
        out['delta'].append(st[1])
        out['lru'].append(st[2][:, 0])
    else:
        out['mc'].append(st[0])
        out['mn'].append(st[1])
        out['mm'].append(st[2][:, 0, :HEADS])
    G['x2'] = _ple(h, G['p2'], c['layer'], W['w_ple_gate'], W['w_ple'], G['tm'])
    G['spare'] = h


def _results(G):
    out = G['out']
    y = G['x2'].reshape(G['B'], G['Tp'], G['D'])[:, :G['T']]
    return (y,) + tuple(jnp.stack(out[k]) for k in ('conv', 'delta', 'lru', 'mc', 'mn', 'mm'))


def kernel(x_prompt, x_sample, p_prompt, p_sample, state_conv, state_delta, state_lru,
           state_mlstm_c, state_mlstm_n, state_mlstm_m, w_in_e, w_conv_e, b_conv_e, a_log_e,
           dt_bias_e, delta_norm_e, lru_wr_e, lru_br_e, lru_wi_e, lru_bi_e, lru_lambda_e, w_out_e,
           w_in_o, b_ig_o, b_fg_o, mlstm_norm_o, w_out_o, ln1_g, ln1_b, ln2_g, ln2_b, w_up, w_down,
           w_ple, w_ple_gate):
    depth = ln1_g.shape[0]
    n_even, n_odd = w_in_e.shape[0], w_in_o.shape[0]
    alpha = (2 * depth) ** 0.25
    wt_e = jnp.swapaxes(w_in_e, 1, 2)
    wt_o = jnp.swapaxes(w_in_o, 1, 2)
    W = dict(
        w_in_e=wt_e, w_in_o=wt_o,
        wg_e=[_gate_weight_t(wt_e[j]) for j in range(n_even)],
        wg_o=[_gate_weight_t(wt_o[j]) for j in range(n_odd)],
        w_out_e=w_out_e.astype(BF16), w_out_o=w_out_o.astype(BF16),
        w_up=w_up, w_down=w_down,
        w_ple=w_ple.astype(BF16), w_ple_gate=w_ple_gate,
        w_conv_e=w_conv_e, b_conv_e=b_conv_e, a_log_e=a_log_e, dt_bias_e=dt_bias_e,
        delta_norm_e=delta_norm_e, lru_br_e=lru_br_e,
        lru_wri_e=jnp.concatenate([lru_wr_e, lru_wi_e], axis=-1).astype(BF16),
        lru_bi_e=lru_bi_e, lru_lambda_e=lru_lambda_e, b_ig_o=b_ig_o, b_fg_o=b_fg_o,
        mlstm_norm_o=mlstm_norm_o, ln1_g=ln1_g, ln1_b=ln1_b, ln2_g=ln2_g, ln2_b=ln2_b)
    bp = x_prompt.shape[0]
    zeros = lambda a: jnp.zeros((a.shape[0], bp) + a.shape[2:], F32)
    P = _group(x_prompt, p_prompt, zeros(state_conv), zeros(state_delta), zeros(state_lru),
               zeros(state_mlstm_c), zeros(state_mlstm_n), zeros(state_mlstm_m), W, depth)
    S = _group(x_sample, p_sample, state_conv, state_delta, state_lru,
               state_mlstm_c, state_mlstm_n, state_mlstm_m, W, depth)
    for layer in range(depth):
        cs = _layer_front(S, W, layer)
        h1_s = _plain_mixer(S, cs, alpha)
        cp = _layer_front(P, W, layer)
        if P['pipelined']:
            tiles_s = S['M'] // S['tm_mlp']
            nf_s = P['blk']['nc'] // tiles_s if P['blk']['nc'] % tiles_s == 0 else 0
            ok = nf_s > 0 and S['FF'] % nf_s == 0 and (S['FF'] // nf_s) % MXU_N == 0
            guest = (h1_s, S['tm_mlp'], S['FF'] // nf_s, tiles_s, nf_s) if ok else None
            h_p, h_s = _pipelined_layer(P, cp, W, alpha, guest)
        else:
            h_p, h_s = _mlp(P, cp, W, _plain_mixer(P, cp, alpha), alpha), None
        if h_s is None:
            h_s = _mlp(S, cs, W, h1_s, alpha)
        _layer_back(S, cs, W, h_s)
        _layer_back(P, cp, W, h_p)
    out_p, out_s = _results(P), _results(S)
    return (out_p[0], out_s[0]) + tuple(out_p[1:]) + tuple(out_s[1:])
```

```python
import functools
import math

import jax
import jax.numpy as jnp
from jax import lax
from jax.experimental import pallas as pl
from jax.experimental.pallas import tpu as pltpu

F32 = jnp.float32
BF16 = jnp.bfloat16
HI = lax.Precision.HIGHEST

LANES = 128
SUBLANES = 8
VMEM_LIMIT = 60 * 1024 * 1024

HEADS = 8
DK = 128
A_DV = 128
C_DV = 256
LRU_W = 1024
LRU_BLOCKS = 8
LRU_C = 8.0
CONV_W = 4
CONV_CH = 4096
CHUNK = 64
LN_EPS = 1e-5
RMS_EPS = 1e-6
L2_EPS = 1e-6
PROJ_MAIN = 6144
PROJ_TN = 512
MLP_TF = 512
TM_PROJ = 2048
TM_MLP = 1024
TM_SMALL = 512
MXU_N = 256
MLP_ROW_BLOCKS = 2
CONV_TAIL = SUBLANES
SEQS_PER_BLOCK = 8
NEG_BIG = -1e30


def _dot_f32(a, b):
    return lax.dot_general(a, b, (((1,), (0,)), ((), ())), precision=HI,
                           preferred_element_type=F32)


def _mm(a, b):
    return lax.dot_general(a.astype(BF16), b.astype(BF16), (((1,), (0,)), ((), ())),
                           preferred_element_type=F32)


def _mm_nt(a, b):
    return lax.dot_general(a.astype(BF16), b.astype(BF16), (((1,), (1,)), ((), ())),
                           preferred_element_type=F32)


def _mm_tn(a, b):
    return lax.dot_general(a.astype(BF16), b.astype(BF16), (((0,), (0,)), ((), ())),
                           preferred_element_type=F32)


def _sigmoid(x):
    return 1.0 / (1.0 + jnp.exp(-x))


def _silu(x):
    return x * _sigmoid(x)


def _softplus(x):
    return jnp.maximum(x, 0.0) + jnp.log1p(jnp.exp(-jnp.abs(x)))


def _gelu_tanh(x):
    return 0.5 * x * (1.0 + jnp.tanh(math.sqrt(2.0 / math.pi) * (x + 0.044715 * (x * x * x))))


def _layer_norm(x, g, b):
    mu = jnp.mean(x, axis=-1, keepdims=True)
    xc = x - mu
    var = jnp.mean(xc * xc, axis=-1, keepdims=True)
    return xc * lax.rsqrt(var + LN_EPS) * g + b


def _col_to_row(col, eye):
    return jnp.sum(jnp.where(eye, col, 0.0), axis=0, keepdims=True)


def _seg_masks(L, seg):
    ri = lax.broadcasted_iota(jnp.int32, (L, L), 0)
    ci = lax.broadcasted_iota(jnp.int32, (L, L), 1)
    same = (ri // seg) == (ci // seg)
    return same, same & (ri >= ci), same & (ri > ci), ri == ci


def _seg_tile(a, s, seg):
    if seg >= SUBLANES:
        return a[s * seg:(s + 1) * seg]
    t = (s * seg) // SUBLANES
    return a[t * SUBLANES:(t + 1) * SUBLANES]


def _seg_only(a_tile, s, seg):
    if seg >= SUBLANES:
        return a_tile
    row = lax.broadcasted_iota(jnp.int32, (SUBLANES, 1), 0)
    lo = (s * seg) % SUBLANES
    return jnp.where((row >= lo) & (row < lo + seg), a_tile, 0.0)


def _seg_join(parts, seg):
    if seg >= SUBLANES:
        return jnp.concatenate(parts, axis=0)
    per_tile = SUBLANES // seg
    row = lax.broadcasted_iota(jnp.int32, (SUBLANES, 1), 0)
    tiles = []
    for t in range(len(parts) // per_tile):
        tile = parts[t * per_tile]
        for i in range(1, per_tile):
            tile = jnp.where(row >= i * seg, parts[t * per_tile + i], tile)
        tiles.append(tile)
    return jnp.concatenate(tiles, axis=0)


def _seg_rows(vals, L, seg):
    if seg >= SUBLANES:
        return jnp.concatenate([jnp.broadcast_to(v, (seg, v.shape[1])) for v in vals], axis=0)
    seg_id = lax.broadcasted_iota(jnp.int32, (L, 1), 0) // seg
    out = jnp.broadcast_to(vals[0], (L, vals[0].shape[1]))
    for i in range(1, len(vals)):
        out = jnp.where(seg_id == i, vals[i], out)
    return out


def _ticker(overlap, n_ticks, early=False):
    pending = list(overlap()) if overlap is not None else []
    total = len(pending)
    state = [0, 0]

    def tick(count=1):
        state[0] += 0 if count is None else count
        share = -(-state[0] * total // n_ticks) if early else (state[0] * total) // n_ticks
        target = total if count is None else min(total, share)
        while state[1] < target:
            pending[state[1]]()
            state[1] += 1
    return tick


def _params(sem):
    return pltpu.CompilerParams(dimension_semantics=sem, vmem_limit_bytes=VMEM_LIMIT)


def _proj_body(x_ref, wt_ref, wgt_ref, o_ref, og_ref, xb_ref):
    @pl.when(pl.program_id(1) == 0)
    def _():
        xb = x_ref[...].astype(BF16)
        xb_ref[...] = xb
        og_ref[...] = lax.dot_general(xb, wgt_ref[...].astype(BF16), (((1,), (1,)), ((), ())),
                                      preferred_element_type=F32)

    o_ref[...] = lax.dot_general(xb_ref[...], wt_ref[...].astype(BF16), (((1,), (1,)), ((), ())),
                                 preferred_element_type=F32)


def _proj(x, wt_stack, layer, wgt, tm, tn):
    M, K = x.shape
    return pl.pallas_call(
        _proj_body,
        grid=(M // tm, PROJ_MAIN // tn),
        in_specs=[pl.BlockSpec((tm, K), lambda i, j: (i, 0), pipeline_mode=pl.Buffered(1)),
                  pl.BlockSpec((None, tn, K), lambda i, j: (layer, j, 0)),
                  pl.BlockSpec((LANES, K), lambda i, j: (0, 0))],
        out_specs=[pl.BlockSpec((tm, tn), lambda i, j: (i, j)),
                   pl.BlockSpec((tm, LANES), lambda i, j: (i, 0))],
        out_shape=[jax.ShapeDtypeStruct((M, PROJ_MAIN), F32),
                   jax.ShapeDtypeStruct((M, LANES), F32)],
        scratch_shapes=[pltpu.VMEM((tm, K), BF16)],
        compiler_params=_params(("parallel", "arbitrary")),
        name="in_proj",
    )(x, wt_stack, wgt)


def _row_window(total_rows, rows, row0, tm):
    assert rows % tm == 0 and row0 % tm == 0
    return rows // tm, row0 // tm


def _into_operand(into):
    if into is None:
        return [], []
    return [into], [pl.BlockSpec(memory_space=pl.ANY)]


def _outproj_ln_body(y_ref, x_ref, w_ref, g_ref, b_ref, o_ref, *, alpha):
    mix = jnp.dot(y_ref[...].astype(BF16), w_ref[...], preferred_element_type=F32)
    o_ref[...] = _layer_norm(alpha * x_ref[...] + mix, g_ref[...], b_ref[...])


def _outproj_ln(y, x, w_stack, layer, g, b, tm, alpha, x_row0=0):
    M, K = y.shape
    D = w_stack.shape[2]
    n, off = _row_window(M, M, x_row0, tm)
    row = lambda i: (i, 0)
    fixed = lambda i: (0, 0)
    return pl.pallas_call(
        functools.partial(_outproj_ln_body, alpha=alpha),
        grid=(n,),
        in_specs=[pl.BlockSpec((tm, K), row), pl.BlockSpec((tm, D), lambda i: (off + i, 0)),
                  pl.BlockSpec((None, K, D), lambda i: (layer, 0, 0), pipeline_mode=pl.Buffered(1)),
                  pl.BlockSpec((1, D), fixed), pl.BlockSpec((1, D), fixed)],
        out_specs=pl.BlockSpec((tm, D), row),
        out_shape=jax.ShapeDtypeStruct((M, D), F32),
        compiler_params=_params(("parallel",)),
        name="out_proj_ln",
    )(y, x, w_stack, g, b)


def _mlp_pre(f, h_ref, o_ref, hb_ref):
    @pl.when(f == 0)
    def _():
        hb_ref[...] = h_ref[...].astype(BF16)
        o_ref[...] = jnp.zeros_like(o_ref)


def _mlp_main(wu_ref, wd_ref, o_ref, hb_ref):
    up = jnp.dot(hb_ref[...], wu_ref[...].astype(BF16), preferred_element_type=F32)
    act = jnp.square(jnp.maximum(up, 0.0)).astype(BF16)
    o_ref[...] += jnp.dot(act, wd_ref[...].astype(BF16), preferred_element_type=F32)


def _mlp_pieces(wu_ref, wd_ref, o_ref, hb_ref):
    tm, D = hb_ref.shape
    tf = wu_ref.shape[1]
    rb = tm // MLP_ROW_BLOCKS
    acts = {}
    pieces = []
    for r in range(MLP_ROW_BLOCKS):
        rows = slice(r * rb, (r + 1) * rb)
        for n in range(tf // MXU_N):
            def up_piece(r=r, n=n, rows=rows):
                w = wu_ref[:, n * MXU_N:(n + 1) * MXU_N].astype(BF16)
                up = jnp.dot(hb_ref[rows, :], w, preferred_element_type=F32)
                acts[r, n] = jnp.square(jnp.maximum(up, 0.0)).astype(BF16)
            pieces.append(up_piece)
        for k in range(D // MXU_N):
            def down_piece(r=r, k=k, rows=rows):
                act = jnp.concatenate([acts[r, n] for n in range(tf // MXU_N)], axis=1)
                w = wd_ref[:, k * MXU_N:(k + 1) * MXU_N].astype(BF16)
                o_ref[rows, k * MXU_N:(k + 1) * MXU_N] += jnp.dot(act, w, preferred_element_type=F32)
            pieces.append(down_piece)
    return pieces


def _mlp_post(f, nf, h_ref, g_ref, b_ref, o_ref, alpha):
    @pl.when(f == nf - 1)
    def _():
        o_ref[...] = _layer_norm(alpha * h_ref[...] + o_ref[...], g_ref[...], b_ref[...])


def _mlp_ln_body(h_ref, wu_ref, wd_ref, g_ref, b_ref, *rest, alpha):
    o_ref, hb_ref = rest[-2:]
    f = pl.program_id(1)
    _mlp_pre(f, h_ref, o_ref, hb_ref)
    _mlp_main(wu_ref, wd_ref, o_ref, hb_ref)
    _mlp_post(f, pl.num_programs(1), h_ref, g_ref, b_ref, o_ref, alpha)


def _mlp_specs(D, tm, tf, layer, out_off):
    in_specs = [pl.BlockSpec((tm, D), lambda i, f: (i, 0), pipeline_mode=pl.Buffered(1)),
                pl.BlockSpec((None, D, tf), lambda i, f: (layer, 0, f)),
                pl.BlockSpec((None, tf, D), lambda i, f: (layer, f, 0)),
                pl.BlockSpec((1, D), lambda i, f: (0, 0)),
                pl.BlockSpec((1, D), lambda i, f: (0, 0))]
    out_spec = pl.BlockSpec((tm, D), lambda i, f: (out_off + i, 0), pipeline_mode=pl.Buffered(1))
    return in_specs, out_spec, [pltpu.VMEM((tm, D), BF16)]


def _mlp_ln(h, wu_stack, wd_stack, layer, g, b, tm, tf, alpha, out_row0=0, into=None):
    M, D = h.shape
    FF = wu_stack.shape[2]
    n, off = _row_window(M, M, out_row0, tm)
    in_specs, out_spec, scratch = _mlp_specs(D, tm, tf, layer, off)
    extra, extra_specs = _into_operand(into)
    return pl.pallas_call(
        functools.partial(_mlp_ln_body, alpha=alpha),
        grid=(n, FF // tf),
        in_specs=in_specs + extra_specs,
        out_specs=out_spec,
        out_shape=jax.ShapeDtypeStruct(((M if into is None else into.shape[0]), D), F32),
        scratch_shapes=scratch,
        input_output_aliases={5: 0} if extra else {},
        compiler_params=_params(("parallel", "arbitrary")),
        name="mlp_ln",
    )(h, wu_stack, wd_stack, g, b, *extra)


def _ple_body(h_ref, p_ref, wg_ref, wp_ref, o_ref):
    h = h_ref[...]
    gate = _sigmoid(jnp.dot(h.astype(BF16), wg_ref[...].astype(BF16), preferred_element_type=F32))
    emb = jnp.dot(p_ref[...].astype(BF16), wp_ref[...], preferred_element_type=F32)
    o_ref[...] = h + gate * emb


def _ple(h, p_stack, layer, wg_stack, wp_stack, tm):
    M, D = h.shape
    P = p_stack.shape[2]
    row = lambda i: (i, 0)
    return pl.pallas_call(
        _ple_body,
        grid=(M // tm,),
        in_specs=[pl.BlockSpec((tm, D), row),
                  pl.BlockSpec((None, tm, P), lambda i: (layer, i, 0)),
                  pl.BlockSpec((None, D, D), lambda i: (layer, 0, 0), pipeline_mode=pl.Buffered(1)),
                  pl.BlockSpec((None, P, D), lambda i: (layer, 0, 0), pipeline_mode=pl.Buffered(1))],
        out_specs=pl.BlockSpec((tm, D), row),
        out_shape=jax.ShapeDtypeStruct((M, D), F32),
        compiler_params=_params(("parallel",)),
        name="ple_gate",
    )(h, p_stack, wg_stack, wp_stack)


def _mixer_e_body(proj_ref, gate_ref, conv0_ref, s0_ref, h0_ref, wconv_ref, bconv_ref, alog_ref,
                  dtb_ref, nw_ref, wri_ref, br_ref, bi_ref, lam_ref,
                  y_ref, convn_ref, s_ref, hl_ref,
                  cbuf, a_scr, bx_scr, hseq, *, L, seg, t_valid, conv_win,
                  chunk=None, n_chunks=None, overlap=None):
    c = pl.program_id(1) if chunk is None else chunk
    last = (pl.num_programs(1) if n_chunks is None else n_chunks) - 1
    heads = range(HEADS)
    nseg = L // seg
    segs = range(nseg)
    stride = CONV_TAIL + seg

    packed = seg < SUBLANES

    @pl.when(c == 0)
    def _():
        if not packed:
            for s in segs:
                cbuf[s * stride:s * stride + CONV_TAIL, :] = conv0_ref[s * CONV_TAIL:(s + 1) * CONV_TAIL, :]
        s_ref[...] = s0_ref[...]
        hl_ref[...] = h0_ref[...]

    tick = _ticker(overlap, 32, early=True)
    tick()
    same, tril, strict, eye = _seg_masks(L, seg)
    tril_f = jnp.where(tril, 1.0, 0.0).astype(F32)
    same_f = jnp.where(same, 1.0, 0.0).astype(F32)
    row_ok = (lax.broadcasted_iota(jnp.int32, (L, 1), 0) % seg) < t_valid
    s_old = [[s_ref[s, h] for h in heads] for s in segs]
    h_old = [hl_ref[s] for s in segs]

    if packed:
        u = proj_ref[:, 0:CONV_CH]
        lead = [jnp.zeros((seg - (CONV_W - 1), CONV_CH), F32)] if seg > CONV_W - 1 else []
        hist = jnp.concatenate([r for s in segs for r in
                                lead + [conv0_ref[i, s:s + 1, :] for i in range(CONV_W - 1)]], axis=0)
        rowm = lax.broadcasted_iota(jnp.int32, (L, 1), 0) % seg
        acc = bconv_ref[...] + u * wconv_ref[CONV_W - 1:CONV_W, :]
        for j in range(1, CONV_W):
            shifted = jnp.where(rowm >= j, pltpu.roll(u, j, axis=0), pltpu.roll(hist, (j - seg) % L, axis=0))
            acc = acc + shifted * wconv_ref[CONV_W - 1 - j:CONV_W - j, :]
        for i in range(CONV_W - 1):
            convn_ref[i] = jnp.concatenate([u[(s + 1) * seg - (CONV_W - 1) + i:(s + 1) * seg - (CONV_W - 1) + i + 1, :]
                                            for s in segs], axis=0)
    else:
        for s in segs:
            cbuf[s * stride + CONV_TAIL:(s + 1) * stride, :] = proj_ref[s * seg:(s + 1) * seg, 0:CONV_CH]
        full = cbuf[...]
        prev = pltpu.roll(full, 1, axis=0)
        tick()
        pair_new = full * wconv_ref[3:4, :] + prev * wconv_ref[2:3, :]
        tick()
        pair_old = full * wconv_ref[1:2, :] + prev * wconv_ref[0:1, :]
        tick()
        conv_full = bconv_ref[...] + pair_new + pltpu.roll(pair_old, 2, axis=0)
        acc = jnp.concatenate([conv_full[s * stride + CONV_TAIL:(s + 1) * stride, :] for s in segs], axis=0)

    tick()
    gp = gate_ref[...]
    g_all = -jnp.exp(alog_ref[...]) * _softplus(gp + dtb_ref[...])
    beta_all = _sigmoid(gp)
    if t_valid < seg:
        g_all = jnp.where(row_ok, g_all, 0.0)
        beta_all = jnp.where(row_ok, beta_all, 0.0)
    gc_all = _dot_f32(tril_f, g_all)
    gl_all = _dot_f32(same_f, g_all)
    tick()

    q_l, k_l, kb_l, gcol_l, egc_l, y_l = [], [], [], [], [], []
    qa_l = [_silu(acc[:, h * DK:(h + 1) * DK]) for h in heads]
    tick()
    ka_l = [_silu(acc[:, HEADS * DK + h * DK:HEADS * DK + (h + 1) * DK]) for h in heads]
    tick()
    qn_l = [lax.rsqrt(jnp.sum(qa_l[h] * qa_l[h], axis=-1, keepdims=True) + L2_EPS) * (DK ** -0.5)
            for h in heads]
    kn_l = [lax.rsqrt(jnp.sum(ka_l[h] * ka_l[h], axis=-1, keepdims=True) + L2_EPS) for h in heads]
    tick()
    for h in heads:
        q = qa_l[h] * qn_l[h]
        k = ka_l[h] * kn_l[h]
        v = _silu(acc[:, 2 * HEADS * DK + h * A_DV:2 * HEADS * DK + (h + 1) * A_DV])
        gcol = gc_all[:, h:h + 1]
        bcol = beta_all[:, HEADS + h:HEADS + h + 1]
        egc = jnp.exp(gcol)
        kb = k * bcol
        q_l.append(q)
        k_l.append(k)
        kb_l.append(kb)
        gcol_l.append(gcol)
        egc_l.append(egc)
        y_l.append(jnp.concatenate([v * bcol, kb * egc], axis=1))
        if h % 2:
            tick()
    kq_k = [_mm_nt(jnp.concatenate([kb_l[h], q_l[h]], axis=0), k_l[h]) for h in heads]
    tick()
    p_l, qk_l = [], []
    for h in heads:
        grow = _col_to_row(gcol_l[h], eye)
        decay = jnp.where(tril, jnp.exp(jnp.where(tril, gcol_l[h] - grow, 0.0)), 0.0)
        p_l.append(jnp.where(strict, -(kq_k[h][0:L] * decay), 0.0))
        qk_l.append(kq_k[h][L:2 * L] * decay)
        if h % 4 == 3:
            tick()

    n_levels = max(1, math.ceil(math.log2(t_valid)))
    G = max(1, min(HEADS, MXU_N // L))
    groups = range(HEADS // G)
    lane_blk = lax.broadcasted_iota(jnp.int32, (L, G * L), 1) // L

    def block_diag(wide):
        return jnp.concatenate([jnp.where(lane_blk == i, wide, 0.0) for i in range(G)], axis=0)

    pg_l = [jnp.concatenate(p_l[g * G:(g + 1) * G], axis=1) for g in groups]
    rg_l = pg_l
    if n_levels >= 2:
        pg_l = [_mm(pg_l[g], block_diag(pg_l[g])) for g in groups]
        tick()
        for _ in range(n_levels - 2):
            out = [_mm(jnp.concatenate([pg_l[g], rg_l[g]], axis=0), block_diag(pg_l[g])) for g in groups]
            rg_l = [rg_l[g] + pg_l[g] + out[g][L:2 * L] for g in groups]
            pg_l = [out[g][0:L] for g in groups]
            tick()
        out = [_mm(rg_l[g], block_diag(pg_l[g])) for g in groups]
        rg_l = [rg_l[g] + pg_l[g] + out[g] for g in groups]
        tick()
    r_l = [rg_l[h // G][:, (h % G) * L:(h % G + 1) * L] for h in heads]
    y_l = [y_l[h] + _mm(r_l[h], y_l[h]) for h in heads]
    tick()

    R = max(seg, SUBLANES)
    qe_l = [q_l[h] * egc_l[h] for h in heads]
    wq_s = [[_mm(jnp.concatenate([_seg_tile(y_l[h][:, A_DV:A_DV + DK], s, seg),
                                  _seg_tile(qe_l[h], s, seg)], axis=0), s_old[s][h])
             for h in heads] for s in segs]
    v_new = [y_l[h][:, 0:A_DV] - _seg_join([wq_s[s][h][0:R] for s in segs], seg) for h in heads]
    tick()
    o_intra = [_mm(qk_l[h], v_new[h]) for h in heads]
    tick()
    kd_l = [k_l[h] * jnp.exp(gl_all[:, h:h + 1] - gcol_l[h]) for h in heads]
    s_upd = [[s_old[s][h] * jnp.exp(gl_all[s * seg:s * seg + 1, h:h + 1])
              + _mm_tn(_seg_only(_seg_tile(kd_l[h], s, seg), s, seg), _seg_tile(v_new[h], s, seg))
              for h in heads] for s in segs]

    bw = LRU_W // LRU_BLOCKS
    xr = acc[:, 3 * HEADS * DK:3 * HEADS * DK + LRU_W]
    sp = _softplus(-lam_ref[...])
    for n in range(LRU_BLOCKS):
        sl = slice(n * bw, (n + 1) * bw)
        xb = xr[:, sl]
        ri_pre = _mm(xb, wri_ref[n])
        r_pre = ri_pre[:, 0:bw] + br_ref[:, sl]
        i_pre = ri_pre[:, bw:2 * bw] + bi_ref[:, sl]
        log_a = -LRU_C * _sigmoid(r_pre) * sp[:, sl]
        a = jnp.exp(log_a)
        a_scr[:, sl] = a
        bx_scr[:, sl] = jnp.sqrt(-jnp.tanh(log_a) * (a * a + 1.0)) * _sigmoid(i_pre) * xb
        tick()

    for s in segs:
        hc = h_old[s]
        for t in range(seg):
            row = s * seg + t
            if t < t_valid:
                hc = a_scr[row:row + 1, :] * hc + bx_scr[row:row + 1, :]
            hseq[row:row + 1, :] = hc
        hl_ref[s] = hc
    tick()

    o_l = [_seg_join([wq_s[s][h][R:2 * R] for s in segs], seg) + o_intra[h] for h in heads]
    rms_l = [lax.rsqrt(jnp.mean(o_l[h] * o_l[h], axis=-1, keepdims=True) + RMS_EPS) for h in heads]
    tick()
    for h in heads:
        z = proj_ref[:, CONV_CH + h * A_DV:CONV_CH + (h + 1) * A_DV]
        y_ref[:, h * A_DV:(h + 1) * A_DV] = o_l[h] * rms_l[h] * nw_ref[...] * _silu(z)
    for s in segs:
        for h in heads:
            s_ref[s, h] = s_upd[s][h]
    gate = proj_ref[:, CONV_CH + HEADS * A_DV:CONV_CH + HEADS * A_DV + LRU_W]
    y_ref[:, HEADS * A_DV:HEADS * A_DV + LRU_W] = hseq[...] * _gelu_tanh(gate)
    tick(None)

    if not packed:
        @pl.when(c == last)
        def _():
            for s in segs:
                convn_ref[s * SUBLANES:(s + 1) * SUBLANES, :] = (
                    cbuf[s * stride + conv_win:s * stride + conv_win + SUBLANES, :])

        for s in segs:
            cbuf[s * stride:s * stride + CONV_TAIL, :] = cbuf[s * stride + seg:s * stride + seg + CONV_TAIL, :]


def _mixer_e_parts(proj, gates, conv0, s0, h0, wconv, bconv, alog, dtb, nw, wri, br, bi, lam,
                   *, seq0, nblk, nc, L, seg, t_valid, bc):
    nseg = L // seg
    assert nc == 1 or nseg == 1
    assert t_valid >= CONV_W - 1
    packed = seg < SUBLANES
    if packed:
        assert nc == 1 and t_valid == seg and L % SUBLANES == 0
        hist_rows, conv_win, conv_off = nseg, 0, None
    else:
        conv_row = CONV_TAIL - (CONV_W - 1) + t_valid
        conv_win = (conv_row // SUBLANES) * SUBLANES
        conv_off = conv_row - conv_win
        assert conv_off + CONV_W - 1 <= SUBLANES
        hist_rows = nseg * SUBLANES
    mix = HEADS * A_DV + LRU_W

    def chunk(*g):
        b, c = bc(*g)
        return ((seq0 + b) * nc + c, 0)

    def chunk_out(*g):
        b, c = bc(*g)
        return (b * nc + c, 0)

    blk_in = lambda *g: seq0 + bc(*g)[0]
    blk_out = lambda *g: bc(*g)[0]
    fix2 = lambda *g: (0, 0)
    fix3 = lambda *g: (0, 0, 0)
    return dict(
        body=functools.partial(_mixer_e_body, L=L, seg=seg, t_valid=t_valid, conv_win=conv_win),
        operands=[proj, gates, conv0, s0, h0, wconv, bconv, alog, dtb, nw, wri, br, bi, lam],
        in_specs=[pl.BlockSpec((L, PROJ_MAIN), chunk),
                  pl.BlockSpec((L, LANES), chunk),
                  (pl.BlockSpec((CONV_W - 1, nseg, CONV_CH), lambda *g: (0, blk_in(*g), 0)) if packed else
                   pl.BlockSpec((hist_rows, CONV_CH), lambda *g: (blk_in(*g), 0))),
                  pl.BlockSpec((nseg, HEADS, DK, A_DV), lambda *g: (blk_in(*g), 0, 0, 0)),
                  pl.BlockSpec((nseg, 1, LRU_W), lambda *g: (blk_in(*g), 0, 0)),
                  pl.BlockSpec((CONV_W, CONV_CH), fix2),
                  pl.BlockSpec((1, CONV_CH), fix2),
                  pl.BlockSpec((1, LANES), fix2),
                  pl.BlockSpec((1, LANES), fix2),
                  pl.BlockSpec((1, A_DV), fix2),
                  pl.BlockSpec((LRU_BLOCKS, LRU_W // LRU_BLOCKS, 2 * LRU_W // LRU_BLOCKS), fix3),
                  pl.BlockSpec((1, LRU_W), fix2),
                  pl.BlockSpec((1, LRU_W), fix2),
                  pl.BlockSpec((1, LRU_W), fix2)],
        out_specs=[pl.BlockSpec((L, mix), chunk_out),
                   (pl.BlockSpec((CONV_W - 1, nseg, CONV_CH), lambda *g: (0, blk_out(*g), 0)) if packed else
                    pl.BlockSpec((hist_rows, CONV_CH), lambda *g: (blk_out(*g), 0))),
                   pl.BlockSpec((nseg, HEADS, DK, A_DV), lambda *g: (blk_out(*g), 0, 0, 0)),
                   pl.BlockSpec((nseg, 1, LRU_W), lambda *g: (blk_out(*g), 0, 0))],
        out_shape=[jax.ShapeDtypeStruct((nblk * nc * L, mix), F32),
                   jax.ShapeDtypeStruct(((CONV_W - 1, nblk * nseg, CONV_CH) if packed else
                                         (nblk * hist_rows, CONV_CH)), F32),
                   jax.ShapeDtypeStruct((nblk * nseg, HEADS, DK, A_DV), F32),
                   jax.ShapeDtypeStruct((nblk * nseg, 1, LRU_W), F32)],
        scratch_shapes=[pltpu.VMEM((nseg * (CONV_TAIL + seg), CONV_CH), F32),
                        pltpu.VMEM((L, LRU_W), F32),
                        pltpu.VMEM((L, LRU_W), F32),
                        pltpu.VMEM((L, LRU_W), F32)],
        name="delta_lru_mixer", conv_off=conv_off)


def _mixer_o_body(proj_ref, gate_ref, c0_ref, n0_ref, m0_ref, big_ref, bfg_ref, nw_ref,
                  y_ref, c_ref, n_ref, m_ref, *, L, seg, t_valid, chunk=None, n_chunks=None, overlap=None):
    ci = pl.program_id(1) if chunk is None else chunk
    heads = range(HEADS)
    nseg = L // seg
    segs = range(nseg)

    @pl.when(ci == 0)
    def _():
        c_ref[...] = c0_ref[...]
        n_ref[...] = n0_ref[...]
        m_ref[...] = m0_ref[...]

    tick = _ticker(overlap, 16)
    same, tril, _, eye = _seg_masks(L, seg)
    tril_f = jnp.where(tril, 1.0, 0.0).astype(F32)
    same_f = jnp.where(same, 1.0, 0.0).astype(F32)
    row_id = lax.broadcasted_iota(jnp.int32, (L, 1), 0)
    row_ok = (row_id % seg) < t_valid
    c_old = [[c_ref[s, h] for h in heads] for s in segs]
    n_old = [n_ref[s] for s in segs]
    m_old = [m_ref[s] for s in segs]

    gp = gate_ref[...]
    ig_all = gp + big_ref[...]
    logf_all = -_softplus(-(gp + bfg_ref[...]))
    if t_valid < seg:
        ig_all = jnp.where(row_ok, ig_all, NEG_BIG)
        logf_all = jnp.where(row_ok, logf_all, 0.0)
    bcum_all = _dot_f32(tril_f, logf_all)
    blast_all = _dot_f32(same_f, logf_all)
    mrow_all = _seg_rows(m_old, L, seg)
    tick()

    q_l = [proj_ref[:, h * DK:(h + 1) * DK] for h in heads]
    k_l = [proj_ref[:, HEADS * DK + h * DK:HEADS * DK + (h + 1) * DK] * (DK ** -0.5) for h in heads]
    v_l = [proj_ref[:, 2 * HEADS * DK + h * C_DV:2 * HEADS * DK + (h + 1) * C_DV] for h in heads]
    qk_l = [_mm_nt(q_l[h], k_l[h]) for h in heads]
    tick()
    qc_l = [_seg_join([_mm_nt(_seg_tile(q_l[h], s, seg), c_old[s][h]) for s in segs], seg) for h in heads]
    tick()

    s_l, e_l, mt_l, sc_l, ws_l, mnew_l = [], [], [], [], [], []
    for h in heads:
        bcol = bcum_all[:, HEADS + h:HEADS + h + 1]
        blast = blast_all[:, HEADS + h:HEADS + h + 1]
        icol = ig_all[:, h:h + 1]
        mcol = mrow_all[:, h:h + 1]
        brow = _col_to_row(bcol, eye)
        irow = _col_to_row(icol, eye)
        d_intra = jnp.where(tril, bcol - brow + irow, NEG_BIG)
        ge_mat = jnp.where(same, blast - brow + irow, NEG_BIG)
        ge_col = blast - bcol + icol
        inter = bcol + mcol
        m_t = jnp.maximum(inter, jnp.max(d_intra, axis=-1, keepdims=True))
        m_new = jnp.maximum(blast + mcol, jnp.max(ge_mat, axis=-1, keepdims=True))
        e_l.append(jnp.exp(inter - m_t))
        mt_l.append(m_t)
        s_l.append(qk_l[h] * jnp.exp(d_intra - m_t))
        sc_l.append(jnp.exp(blast + mcol - m_new))
        ws_l.append(jnp.exp(ge_col - m_new))
        mnew_l.append(m_new)
        tick()

    sv = [_mm(s_l[h], v_l[h]) for h in heads]
    tick()
    vw_l = [v_l[h] * ws_l[h] for h in heads]
    kw_l = [k_l[h] * ws_l[h] for h in heads]
    kv = [[_mm_tn(_seg_only(_seg_tile(vw_l[h], s, seg), s, seg), _seg_tile(k_l[h], s, seg)) for h in heads]
          for s in segs]

    nrow_l = [_seg_rows([n_old[s][h:h + 1, :] for s in segs], L, seg) for h in heads]
    den_l = [e_l[h] * jnp.sum(q_l[h] * nrow_l[h], axis=-1, keepdims=True)
             + jnp.sum(s_l[h], axis=-1, keepdims=True) for h in heads]
    inv_l = [1.0 / jnp.maximum(jnp.abs(den_l[h]), jnp.exp(-mt_l[h])) for h in heads]
    tick()
    hh_l = [(e_l[h] * qc_l[h] + sv[h]) * inv_l[h] for h in heads]
    rms_l = [lax.rsqrt(jnp.mean(hh_l[h] * hh_l[h], axis=-1, keepdims=True) + RMS_EPS) for h in heads]
    tick()
    for h in heads:
        o_pre = proj_ref[:, 2 * HEADS * DK + HEADS * C_DV + h * C_DV:
                         2 * HEADS * DK + HEADS * C_DV + (h + 1) * C_DV]
        y_ref[:, h * C_DV:(h + 1) * C_DV] = (hh_l[h] * rms_l[h] * nw_ref[:, h * C_DV:(h + 1) * C_DV]
                                             * _sigmoid(o_pre))
        if h % 4 == 3:
            tick()
    lane = lax.broadcasted_iota(jnp.int32, (1, LANES), 1)
    for s in segs:
        r0 = s * seg
        m_row = m_old[s]
        n_rows = []
        for h in heads:
            sc = sc_l[h][r0:r0 + 1, :]
            c_ref[s, h] = sc * c_old[s][h] + kv[s][h]
            n_rows.append(sc * n_old[s][h:h + 1, :]
                          + jnp.sum(_seg_only(_seg_tile(kw_l[h], s, seg), s, seg), axis=0, keepdims=True))
            m_row = jnp.where(lane == h, mnew_l[h][r0:r0 + 1, :], m_row)
        n_ref[s] = jnp.concatenate(n_rows, axis=0)
        m_ref[s] = m_row
    tick(None)


def _mixer_o_parts(proj, gates, c0, n0, m0, big, bfg, nw, *, seq0, nblk, nc, L, seg, t_valid, bc):
    nseg = L // seg
    assert nc == 1 or nseg == 1
    cv = HEADS * C_DV

    def chunk(*g):
        b, c = bc(*g)
        return ((seq0 + b) * nc + c, 0)

    def chunk_out(*g):
        b, c = bc(*g)
        return (b * nc + c, 0)

    blk_in = lambda *g: seq0 + bc(*g)[0]
    blk_out = lambda *g: bc(*g)[0]
    fix2 = lambda *g: (0, 0)
    return dict(
        body=functools.partial(_mixer_o_body, L=L, seg=seg, t_valid=t_valid),
        operands=[proj, gates, c0, n0, m0, big, bfg, nw],
        in_specs=[pl.BlockSpec((L, PROJ_MAIN), chunk),
                  pl.BlockSpec((L, LANES), chunk),
                  pl.BlockSpec((nseg, HEADS, C_DV, DK), lambda *g: (blk_in(*g), 0, 0, 0)),
                  pl.BlockSpec((nseg, HEADS, DK), lambda *g: (blk_in(*g), 0, 0)),
                  pl.BlockSpec((nseg, 1, LANES), lambda *g: (blk_in(*g), 0, 0)),
                  pl.BlockSpec((1, LANES), fix2),
                  pl.BlockSpec((1, LANES), fix2),
                  pl.BlockSpec((1, cv), fix2)],
        out_specs=[pl.BlockSpec((L, cv), chunk_out),
                   pl.BlockSpec((nseg, HEADS, C_DV, DK), lambda *g: (blk_out(*g), 0, 0, 0)),
                   pl.BlockSpec((nseg, HEADS, DK), lambda *g: (blk_out(*g), 0, 0)),
                   pl.BlockSpec((nseg, 1, LANES), lambda *g: (blk_out(*g), 0, 0))],
        out_shape=[jax.ShapeDtypeStruct((nblk * nc * L, cv), F32),
                   jax.ShapeDtypeStruct((nblk * nseg, HEADS, C_DV, DK), F32),
                   jax.ShapeDtypeStruct((nblk * nseg, HEADS, DK), F32),
                   jax.ShapeDtypeStruct((nblk * nseg, 1, LANES), F32)],
        scratch_shapes=[],
        name="mlstm_mixer")


def _run_mixer(parts, nblk, nc):
    return pl.pallas_call(
        parts['body'],
        grid=(nblk, nc),
        in_specs=parts['in_specs'],
        out_specs=parts['out_specs'],
        out_shape=parts['out_shape'],
        scratch_shapes=parts['scratch_shapes'],
        compiler_params=_params(("parallel", "arbitrary")),
        name=parts['name'],
    )(*parts['operands'])


def _fused_body(*refs, mixer_body, n_mi, n_mo, n_ms, alpha):
    mi = refs[:n_mi]
    h_ref, wu_ref, wd_ref, g_ref, b_ref = refs[n_mi:n_mi + 5]
    outs = refs[n_mi + 6:]
    mo = outs[:n_mo]
    o_ref = outs[n_mo]
    ms = outs[n_mo + 1:n_mo + 1 + n_ms]
    hb_ref = outs[n_mo + 1 + n_ms]
    i, f = pl.program_id(0), pl.program_id(1)
    nf = pl.num_programs(1)
    _mlp_pre(f, h_ref, o_ref, hb_ref)
    mixer_body(*mi, *mo, *ms, chunk=i * nf + f, n_chunks=pl.num_programs(0) * nf,
               overlap=lambda: _mlp_pieces(wu_ref, wd_ref, o_ref, hb_ref))
    _mlp_post(f, nf, h_ref, g_ref, b_ref, o_ref, alpha)


def _run_mixer_mlp(parts, tiles, nf, h, wu_stack, wd_stack, layer, g, b, tm, tf, alpha, out_row0, h2_into):
    M, D = h.shape
    assert M == tiles * tm
    mlp_in, mlp_out, mlp_scr = _mlp_specs(D, tm, tf, layer, out_row0 // tm)
    n_mi, n_mo, n_ms = len(parts['operands']), len(parts['out_specs']), len(parts['scratch_shapes'])
    outs = pl.pallas_call(
        functools.partial(_fused_body, mixer_body=parts['body'], n_mi=n_mi, n_mo=n_mo, n_ms=n_ms, alpha=alpha),
        grid=(tiles, nf),
        in_specs=parts['in_specs'] + mlp_in + [pl.BlockSpec(memory_space=pl.ANY)],
        out_specs=parts['out_specs'] + [mlp_out],
        out_shape=parts['out_shape'] + [jax.ShapeDtypeStruct(h2_into.shape, F32)],
        scratch_shapes=parts['scratch_shapes'] + mlp_scr,
        input_output_aliases={n_mi + 5: n_mo},
        compiler_params=_params(("arbitrary", "arbitrary")),
        name=parts['name'] + "_mlp",
    )(*parts['operands'], h, wu_stack, wd_stack, g, b, h2_into)
    return outs[:n_mo], outs[n_mo]


def _pad_lanes(v, offset=0):
    return jnp.pad(v.astype(F32), (offset, LANES - offset - v.shape[0]))[None, :]


def _gate_weight_t(wt):
    wg = wt[PROJ_MAIN:, :]
    return jnp.pad(wg, ((0, LANES - wg.shape[0]), (0, 0)))


def _row_tile(m, cap):
    t = cap
    while t >= SUBLANES:
        if m % t == 0:
            return t
        t //= 2
    raise ValueError(f"row count {m} is not a multiple of {SUBLANES}")


def _blocking(B, T):
    t_valid = math.gcd(T, CHUNK)
    nc = T // t_valid
    if SUBLANES % t_valid == 0 and nc == 1:
        seg = t_valid
    else:
        seg = -(-t_valid // SUBLANES) * SUBLANES
    assert nc == 1 or seg == t_valid
    nseg = 1
    if nc == 1:
        nseg = max(1, min(B, SEQS_PER_BLOCK))
        while B % nseg or (nseg * seg) % SUBLANES:
            nseg -= 1
        assert nseg >= 1
    return t_valid, seg, nseg * seg, B // nseg, nc


def _group(x, p, conv, delta, lru, mc, mn, mm, W, depth):
    B, T, D = x.shape
    t_valid, seg, L, nblk, nc = _blocking(B, T)
    Tp = nc * seg
    if Tp != T:
        x = jnp.pad(x, ((0, 0), (0, Tp - T), (0, 0)))
        p = jnp.pad(p, ((0, 0), (0, 0), (0, Tp - T), (0, 0)))
    M = B * Tp
    FF = W['w_up'].shape[2]
    tiles = max(1, Tp // TM_MLP)
    pipelined = B > 1 and nc > 1 and nc % tiles == 0 and Tp % tiles == 0 and FF % (nc // tiles) == 0
    nf = nc // tiles if pipelined else FF // MLP_TF
    if pipelined:
        tm_mlp, tf = Tp // tiles, FF // nf
        pipelined = tf % LANES == 0 and tm_mlp % SUBLANES == 0
    if not pipelined:
        tm_mlp, tf, nf = _row_tile(M, TM_MLP), MLP_TF, FF // MLP_TF
    return dict(
        B=B, T=T, Tp=Tp, M=M, D=D, FF=FF, depth=depth, nblk=nblk, nc=nc, tiles=tiles, nf=nf,
        pipelined=pipelined, tm=_row_tile(Tp if pipelined else M, TM_SMALL), tm_proj=_row_tile(M, TM_PROJ),
        tm_mlp=tm_mlp, tf=tf, blk=dict(nc=nc, L=L, seg=seg, t_valid=t_valid),
        hist=seg if seg < SUBLANES else CONV_TAIL,
        x2=x.reshape(M, D), p2=p.reshape(depth, M, -1), conv=conv, delta=delta, lru=lru, mc=mc, mn=mn, mm=mm,
        spare=None, out=dict(conv=[], delta=[], lru=[], mc=[], mn=[], mm=[]))


def _layer_front(G, W, layer):
    j = layer // 2
    c = dict(j=j, layer=layer,
             ln1=(W['ln1_g'][layer][None, :], W['ln1_b'][layer][None, :]),
             ln2=(W['ln2_g'][layer][None, :], W['ln2_b'][layer][None, :]))
    if layer % 2 == 0:
        proj, gates = _proj(G['x2'], W['w_in_e'], j, W['wg_e'][j], G['tm_proj'], PROJ_TN)
        hist = G['hist']
        if G['blk']['seg'] < SUBLANES:
            conv0 = jnp.transpose(G['conv'][j], (1, 0, 2))
        else:
            conv0 = jnp.pad(G['conv'][j], ((0, 0), (hist - (CONV_W - 1), 0), (0, 0))).reshape(G['B'] * hist, CONV_CH)
        c['make_parts'] = functools.partial(
            _mixer_e_parts, proj, gates, conv0, G['delta'][j],
            G['lru'][j][:, None, :], W['w_conv_e'][j], W['b_conv_e'][j][None, :],
            _pad_lanes(W['a_log_e'][j]), _pad_lanes(W['dt_bias_e'][j]),
            W['delta_norm_e'][j][None, :], W['lru_wri_e'][j], W['lru_br_e'][j][None, :],
            W['lru_bi_e'][j][None, :], W['lru_lambda_e'][j][None, :], **G['blk'])
        c['w_out'] = W['w_out_e']
    else:
        proj, gates = _proj(G['x2'], W['w_in_o'], j, W['wg_o'][j], G['tm_proj'], PROJ_TN)
        m0 = jnp.pad(G['mm'][j], ((0, 0), (0, LANES - HEADS)))[:, None, :]
        c['make_parts'] = functools.partial(
            _mixer_o_parts, proj, gates, G['mc'][j], G['mn'][j], m0,
            _pad_lanes(W['b_ig_o'][j]), _pad_lanes(W['b_fg_o'][j], HEADS),
            W['mlstm_norm_o'][j][None, :], **G['blk'])
        c['w_out'] = W['w_out_o']
    return c


def _plain_mixer(G, c, alpha):
    parts = c['make_parts'](seq0=0, nblk=G['nblk'], bc=lambda b, ch: (b, ch))
    outs = _run_mixer(parts, G['nblk'], G['blk']['nc'])
    c['states'], c['conv_off'] = outs[1:], parts.get('conv_off')
    return _outproj_ln(outs[0], G['x2'], c['w_out'], c['j'], *c['ln1'], G['tm'], alpha)


def _mlp(G, c, W, h1, alpha):
    return _mlp_ln(h1, W['w_up'], W['w_down'], c['layer'], *c['ln2'], G['tm_mlp'], G['tf'], alpha)


def _pipelined_layer(G, c, W, alpha, guest=None):
    B, Tp, nc, tiles, nf = G['B'], G['Tp'], G['blk']['nc'], G['tiles'], G['nf']
    layer, (g2, b2) = c['layer'], c['ln2']
    h = jnp.zeros((G['M'], G['D']), F32) if G['spare'] is None else G['spare']
    h1 = guest_out = None
    per_seq = []
    for b in range(B):
        fused_bc = lambda i, f, n=(guest[4] if b == 0 and guest else nf): (0, i * n + f)
        if b == 0 and guest is None:
            parts = c['make_parts'](seq0=b, nblk=1, bc=lambda blk_id, ch: (0, ch))
            outs = _run_mixer(parts, 1, nc)
        elif b == 0:
            g_h1, g_tm, g_tf, g_tiles, g_nf = guest
            parts = c['make_parts'](seq0=b, nblk=1, bc=fused_bc)
            outs, guest_out = _run_mixer_mlp(parts, g_tiles, g_nf, g_h1, W['w_up'], W['w_down'], layer, g2, b2,
                                             g_tm, g_tf, alpha, 0, jnp.zeros(g_h1.shape, F32))
        else:
            parts = c['make_parts'](seq0=b, nblk=1, bc=fused_bc)
            outs, h = _run_mixer_mlp(parts, tiles, nf, h1, W['w_up'], W['w_down'], layer, g2, b2,
                                     G['tm_mlp'], G['tf'], alpha, (b - 1) * Tp, h)
        per_seq.append(outs[1:])
        h1 = _outproj_ln(outs[0], G['x2'], c['w_out'], c['j'], *c['ln1'], G['tm'], alpha, x_row0=b * Tp)
    h = _mlp_ln(h1, W['w_up'], W['w_down'], layer, g2, b2, G['tm_mlp'], G['tf'], alpha,
                out_row0=(B - 1) * Tp, into=h)
    c['states'] = [jnp.concatenate([st[k] for st in per_seq], axis=0) for k in range(len(per_seq[0]))]
    c['conv_off'] = parts.get('conv_off')
    return h, guest_out


def _layer_back(G, c, W, h):
    st, out = c['states'], G['out']
    if c['layer'] % 2 == 0:
        off = c['conv_off']
        out['conv'].append(jnp.transpose(st[0], (1, 0, 2)) if off is None else
                           st[0].reshape(G['B'], G['hist'], CONV_CH)[:, off:off + CONV_W - 1])
        out['delta'].append(st[1])
        out['lru'].append(st[2][:, 0])
    else:
        out['mc'].append(st[0])
        out['mn'].append(st[1])
        out['mm'].append(st[2][:, 0, :HEADS])
    G['x2'] = _ple(h, G['p2'], c['layer'], W['w_ple_gate'], W['w_ple'], G['tm'])
    G['spare'] = h


def _results(G):
    out = G['out']
    y = G['x2'].reshape(G['B'], G['Tp'], G['D'])[:, :G['T']]
    return (y,) + tuple(jnp.stack(out[k]) for k in ('conv', 'delta', 'lru', 'mc', 'mn', 'mm'))


def kernel(x_prompt, x_sample, p_prompt, p_sample, state_conv, state_delta, state_lru,
           state_mlstm_c, state_mlstm_n, state_mlstm_m, w_in_e, w_conv_e, b_conv_e, a_log_e,
           dt_bias_e, delta_norm_e, lru_wr_e, lru_br_e, lru_wi_e, lru_bi_e, lru_lambda_e, w_out_e,
           w_in_o, b_ig_o, b_fg_o, mlstm_norm_o, w_out_o, ln1_g, ln1_b, ln2_g, ln2_b, w_up, w_down,
           w_ple, w_ple_gate):
    depth = ln1_g.shape[0]
    n_even, n_odd = w_in_e.shape[0], w_in_o.shape[0]
    alpha = (2 * depth) ** 0.25
    wt_e = jnp.swapaxes(w_in_e, 1, 2)
    wt_o = jnp.swapaxes(w_in_o, 1, 2)
    W = dict(
        w_in_e=wt_e, w_in_o=wt_o,
        wg_e=[_gate_weight_t(wt_e[j]) for j in range(n_even)],
        wg_o=[_gate_weight_t(wt_o[j]) for j in range(n_odd)],
        w_out_e=w_out_e.astype(BF16), w_out_o=w_out_o.astype(BF16),
        w_up=w_up, w_down=w_down,
        w_ple=w_ple.astype(BF16), w_ple_gate=w_ple_gate,
        w_conv_e=w_conv_e, b_conv_e=b_conv_e, a_log_e=a_log_e, dt_bias_e=dt_bias_e,
        delta_norm_e=delta_norm_e, lru_br_e=lru_br_e,
        lru_wri_e=jnp.concatenate([lru_wr_e, lru_wi_e], axis=-1).astype(BF16),
        lru_bi_e=lru_bi_e, lru_lambda_e=lru_lambda_e, b_ig_o=b_ig_o, b_fg_o=b_fg_o,
        mlstm_norm_o=mlstm_norm_o, ln1_g=ln1_g, ln1_b=ln1_b, ln2_g=ln2_g, ln2_b=ln2_b)
    bp = x_prompt.shape[0]
    zeros = lambda a: jnp.zeros((a.shape[0], bp) + a.shape[2:], F32)
    P = _group(x_prompt, p_prompt, zeros(state_conv), zeros(state_delta), zeros(state_lru),
               zeros(state_mlstm_c), zeros(state_mlstm_n), zeros(state_mlstm_m), W, depth)
    S = _group(x_sample, p_sample, state_conv, state_delta, state_lru,
               state_mlstm_c, state_mlstm_n, state_mlstm_m, W, depth)
    for layer in range(depth):
        cs = _layer_front(S, W, layer)
        h1_s = _plain_mixer(S, cs, alpha)
        cp = _layer_front(P, W, layer)
        if P['pipelined']:
            tiles_s = S['M'] // S['tm_mlp']
            nf_s = P['blk']['nc'] // tiles_s if P['blk']['nc'] % tiles_s == 0 else 0
            ok = nf_s > 0 and S['FF'] % nf_s == 0 and (S['FF'] // nf_s) % MXU_N == 0
            guest = (h1_s, S['tm_mlp'], S['FF'] // nf_s, tiles_s, nf_s) if ok else None
            h_p, h_s = _pipelined_layer(P, cp, W, alpha, guest)
        else:
            h_p, h_s = _mlp(P, cp, W, _plain_mixer(P, cp, alpha), alpha), None
        if h_s is None:
            h_s = _mlp(S, cs, W, h1_s, alpha)
        _layer_back(S, cs, W, h_s)
        _layer_back(P, cp, W, h_p)
    out_p, out_s = _results(P), _results(S)
    return (out_p[0], out_s[0]) + tuple(out_p[1:]) + tuple(out_s[1:])
```
